```python
import math
import jax, jax.numpy as jnp
from jax import lax
import numpy as np


D_MODEL = 1024
BATCH = 32
SEQ = 256
DEPTH = 4
DEC_BATCH = 2
DEC_SEQ = 4096
PAST_LEN = 256

GRID_W = 64
N_EVEN = (DEPTH + 1) // 2
N_ODD = DEPTH // 2
A_WIDTH = D_MODEL // 2
A_HEADS = 4
A_HEAD_DIM = A_WIDTH // A_HEADS
F_MIN = 1e-30
B_WIDTH = D_MODEL - A_WIDTH
HYENA_ORDER = 2
FILTER_EMB = 33
FILTER_HID = 64
SHORT_CONV = 3
DECAY_TARGET = 1e-2
FAST_DECAY = 0.3
SLOW_DECAY = 1.5
C_HEADS = 4
C_HEAD_DIM = D_MODEL // C_HEADS
ROPE_BASE = 10000.0
N_EXPERTS = 16
EXPERT_FF = 1408
CAP_FACTOR = 2
CHUNK = 64
LN_EPS = 1e-5
DN_ALPHA = (2 * DEPTH) ** 0.25
DN_BETA = (8 * DEPTH) ** -0.25
EVEN_IN = 5 * A_WIDTH + 3 * B_WIDTH
ODD_IN = 4 * D_MODEL
F32 = jnp.float32

kernel_name = 'hgrn2_hyena_retention_ecmoe_diffusion_step'


def layer_norm(x, g, b):
    xf = x.astype(F32)
    mu = jnp.mean(xf, -1, keepdims=True)
    var = jnp.mean(jnp.square(xf - mu), -1, keepdims=True)
    return ((xf - mu) * lax.rsqrt(var + LN_EPS) * g.astype(F32) + b.astype(F32)).astype(x.dtype)


def modulation(cond, w, b):
    m = jax.nn.silu(cond) @ w + b
    return jnp.split(m[..., None, :], 6, axis=-1)


def chunk_scan(q, k, v, log_f, s0):
    bn, L, H, _ = q.shape
    n = L // CHUNK

    def blocks(t):
        return t.astype(F32).reshape(bn, n, CHUNK, H, t.shape[-1]).transpose(1, 0, 3, 2, 4)

    qc, kc, vc, gc = blocks(q), blocks(k), blocks(v), blocks(log_f)
    mask = jnp.tril(jnp.ones((CHUNK, CHUNK), dtype=bool))[:, :, None]
    scalar = log_f.shape[-1] == 1

    def step(S, inp):
        qb, kb, vb, gb = inp
        b = jnp.cumsum(gb, axis=-2)
        inter = jnp.einsum('bhtd,bhde->bhte', qb * jnp.exp(b), S)
        diff = b[..., :, None, :] - b[..., None, :, :]
        rel = jnp.where(mask, jnp.exp(jnp.minimum(diff, 0.0)), 0.0)
        if scalar:
            att = jnp.einsum('bhtd,bhsd->bhts', qb, kb) * rel[..., 0]
        else:
            att = jnp.einsum('bhtd,bhsd,bhtsd->bhts', qb, kb, rel)
        intra = jnp.einsum('bhts,bhse->bhte', att, vb)
        b_end = b[..., -1:, :]
        S = S * jnp.swapaxes(jnp.exp(b_end), -1, -2) + jnp.einsum('bhsd,bhse->bhde', kb * jnp.exp(b_end - b), vb)
        return S, inter + intra

    s_fin, o = lax.scan(step, s0.astype(F32), (qc, kc, vc, gc))
    o = o.transpose(1, 0, 3, 2, 4).reshape(bn, L, H, v.shape[-1])
    return o, s_fin


def bidir_scan(q, k_f, k_b, v, lf_f, lf_b, s0):
    o_f, s_f = chunk_scan(q, k_f, v, lf_f, s0[:, 0])
    rev = lambda t: jnp.flip(t, axis=1)
    o_b, s_b = chunk_scan(rev(q), rev(k_b), rev(v), rev(lf_b), s0[:, 1])
    return o_f + rev(o_b), jnp.stack([s_f, s_b], axis=1)


def short_conv(x, w, b):
    L = x.shape[1]
    pad = SHORT_CONV // 2
    xp = jnp.pad(x, ((0, 0), (pad, pad), (0, 0)))
    return sum(xp[:, j:j + L] * w[j] for j in range(SHORT_CONV)) + b


def hyena_filters(L, w1, b1, w2, b2, w3, freq):
    t = jnp.linspace(0.0, 1.0, L, dtype=F32)[:, None]
    bands = (FILTER_EMB - 1) // 2
    ang = (2.0 * math.pi / L) * jnp.arange(L, dtype=F32)[:, None] * jnp.linspace(1e-4, bands - 1, bands, dtype=F32)[None, :]
    z = jnp.concatenate([t, jnp.cos(ang), -jnp.sin(ang)], axis=-1)
    f = freq.astype(F32)
    hdn = jnp.sin(f[0] * (z @ w1.astype(F32) + b1.astype(F32)))
    hdn = jnp.sin(f[1] * (hdn @ w2.astype(F32) + b2.astype(F32)))
    filt = (hdn @ w3.astype(F32)).reshape(L, HYENA_ORDER, 2, B_WIDTH)
    deltas = jnp.abs(jnp.linspace(math.log(DECAY_TARGET) / SLOW_DECAY, math.log(DECAY_TARGET) / FAST_DECAY, B_WIDTH, dtype=F32))
    return filt * jnp.exp(-t * deltas)[:, None, None, :]


def bidir_long_conv(u, h_fwd, h_bwd, d):
    L = u.shape[1]
    k = jnp.concatenate([h_fwd, jnp.zeros_like(h_fwd[:1]), jnp.flip(h_bwd[1:], axis=0)], axis=0)
    y = jnp.fft.irfft(jnp.fft.rfft(u, n=2 * L, axis=1) * jnp.fft.rfft(k, axis=0)[None], n=2 * L, axis=1)[:, :L]
    return y + u * d


def hyena(v, gates, filt, d):
    z = v
    for o in range(HYENA_ORDER):
        z = gates[o] * bidir_long_conv(z, filt[:, o, 0], filt[:, o, 1], d[o].astype(F32))
    return z


def rope_2d(x):
    L = x.shape[1]
    n_rows = L // GRID_W
    rows = jnp.repeat(jnp.arange(n_rows), GRID_W)
    cols = jnp.tile(jnp.arange(GRID_W), n_rows)
    half = C_HEAD_DIM // 2
    quarter = half // 2
    inv = ROPE_BASE ** (-jnp.arange(quarter, dtype=F32) / quarter)

    def rot(xa, pos):
        a = pos.astype(F32)[:, None] * inv
        cos, sin = jnp.cos(a)[None, :, None, :], jnp.sin(a)[None, :, None, :]
        x1, x2 = xa[..., :quarter], xa[..., quarter:]
        return jnp.concatenate([x1 * cos - x2 * sin, x1 * sin + x2 * cos], axis=-1)

    xf = x.astype(F32)
    return jnp.concatenate([rot(xf[..., :half], rows), rot(xf[..., half:], cols)], axis=-1).astype(x.dtype)


def ec_moe(h, router, w1, w3, w2):
    bn, N, _ = h.shape
    cap = CAP_FACTOR * N // N_EXPERTS
    aff = jax.nn.softmax(jnp.einsum('bnd,de->ben', h, router).astype(F32), axis=1)
    gate, idx = lax.top_k(aff, cap)
    bidx = jnp.arange(bn)[:, None, None]
    xs = h[bidx, idx]
    hid = jax.nn.silu(jnp.einsum('becd,edf->becf', xs, w1)) * jnp.einsum('becd,edf->becf', xs, w3)
    out = jnp.einsum('becf,efd->becd', hid, w2) * gate[..., None].astype(h.dtype)
    return jnp.zeros_like(h).at[bidx, idx].add(out)


def even_mixer(u, s0, w_in, w_out, lb, norm_w, conv_w, conv_b, fw1, fb1, fw2, fb2, fw3, freq, fbias):
    bn, L, _ = u.shape
    proj = u @ w_in
    q, i, zf, zb, g = jnp.split(proj[..., :5 * A_WIDTH], 5, axis=-1)
    hv, hx1, hx2 = jnp.split(short_conv(proj[..., 5 * A_WIDTH:], conv_w, conv_b).astype(F32), 3, axis=-1)
    heads = lambda t: t.reshape(bn, L, A_HEADS, A_HEAD_DIM)

    def forget(z, lbd):
        z = z.astype(F32)
        f = lbd + (1.0 - lbd) * jax.nn.sigmoid(z)
        log_f = jnp.log(jnp.maximum(f, F_MIN))
        return heads(log_f), heads((1.0 - lbd) * jax.nn.sigmoid(-z))

    lf_f, k_f = forget(zf, lb[0])
    lf_b, k_b = forget(zb, lb[1])
    o, s_new = bidir_scan(heads(jax.nn.silu(q.astype(F32))), k_f, k_b, heads(i), lf_f, lf_b, s0)
    o = o * lax.rsqrt(jnp.mean(o * o, -1, keepdims=True) + LN_EPS) * norm_w.astype(F32).reshape(A_HEADS, A_HEAD_DIM)
    y_a = o.reshape(bn, L, A_WIDTH) * jax.nn.silu(g.astype(F32))
    filt = hyena_filters(L, fw1, fb1, fw2, fb2, fw3, freq)
    y_b = hyena(hv, (hx1, hx2), filt, fbias)
    y = jnp.concatenate([y_a, y_b], axis=-1).astype(u.dtype) @ w_out
    return y, s_new


def retention_mixer(u, s0, latent, w_in, w_out, decay_logit):
    bn, L, _ = u.shape
    q, k, v, g = jnp.split(u @ w_in, 4, axis=-1)
    heads = lambda t: t.reshape(bn, L, C_HEADS, C_HEAD_DIM)
    q, k, v = heads(q), heads(k) * (C_HEAD_DIM ** -0.5), heads(v)
    if latent:
        q, k = rope_2d(q), rope_2d(k)
    log_gamma = jax.nn.log_sigmoid(decay_logit.astype(F32))
    lf = lambda d: jnp.broadcast_to(log_gamma[d][None, None, :, None], (bn, L, C_HEADS, 1))
    o, s_new = bidir_scan(q, k, k, v, lf(0), lf(1), s0)
    mu = jnp.mean(o, -1, keepdims=True)
    var = jnp.mean(jnp.square(o - mu), -1, keepdims=True)
    o = (o - mu) * lax.rsqrt(var + LN_EPS)
    y = (o.reshape(bn, L, D_MODEL) * jax.nn.silu(g.astype(F32))).astype(u.dtype) @ w_out
    return y, s_new


def setup_inputs(seed: int = 0) -> dict:
    key = jax.random.key(seed)
    ks = iter(jax.random.split(key, 40))

    def nrm(shape, scale=1.0):
        return jax.random.normal(next(ks), shape, F32) * scale

    even_cols = jnp.concatenate([jnp.ones(A_WIDTH), jnp.full(A_WIDTH, DN_BETA), jnp.ones(3 * A_WIDTH),
                                 jnp.full(B_WIDTH, DN_BETA), jnp.ones(2 * B_WIDTH)]).astype(F32)
    odd_cols = jnp.concatenate([jnp.ones(2 * D_MODEL), jnp.full(D_MODEL, DN_BETA), jnp.ones(D_MODEL)]).astype(F32)
    ret_init = jnp.log(2.0 ** (5.0 + jnp.arange(C_HEADS, dtype=F32)) - 1.0)
    return {
        'x_prompt': nrm((BATCH, SEQ, D_MODEL)),
        'x_sample': nrm((DEC_BATCH, DEC_SEQ, D_MODEL)),
        'state_hgrn': nrm((DEC_BATCH, N_EVEN, 2, A_HEADS, A_HEAD_DIM, A_HEAD_DIM), 0.5),
        'state_ret': nrm((DEC_BATCH, N_ODD, 2, C_HEADS, C_HEAD_DIM, C_HEAD_DIM), 0.3),
        'c': nrm((DEC_BATCH, D_MODEL)),
        'c_ctx': nrm((D_MODEL,)),
        'ada_w': nrm((DEPTH, D_MODEL, 6 * D_MODEL), D_MODEL ** -0.5),
        'ada_b': nrm((DEPTH, 6 * D_MODEL), 0.02),
        'ln_g': 1.0 + nrm((DEPTH, 2, D_MODEL), 0.02),
        'ln_b': nrm((DEPTH, 2, D_MODEL), 0.02),
        'even_w_in': nrm((N_EVEN, D_MODEL, EVEN_IN), D_MODEL ** -0.5) * even_cols,
        'even_w_out': nrm((N_EVEN, D_MODEL, D_MODEL), DN_BETA * D_MODEL ** -0.5),
        'hgrn_lb': nrm((N_EVEN, 2, A_WIDTH), 0.5),
        'hgrn_norm_w': 1.0 + nrm((N_EVEN, A_WIDTH), 0.02),
        'hyena_conv_w': nrm((N_EVEN, SHORT_CONV, 3 * B_WIDTH), SHORT_CONV ** -0.5),
        'hyena_conv_b': nrm((N_EVEN, 3 * B_WIDTH), 0.02),
        'hyena_w1': nrm((N_EVEN, FILTER_EMB, FILTER_HID), FILTER_EMB ** -0.5),
        'hyena_b1': nrm((N_EVEN, FILTER_HID), 0.02),
        'hyena_w2': nrm((N_EVEN, FILTER_HID, FILTER_HID), FILTER_HID ** -0.5),
        'hyena_b2': nrm((N_EVEN, FILTER_HID), 0.02),
        'hyena_w3': nrm((N_EVEN, FILTER_HID, HYENA_ORDER * 2 * B_WIDTH), 0.1 * FILTER_HID ** -0.5),
        'hyena_freq': 1.0 + nrm((N_EVEN, 2, FILTER_HID), 0.02),
        'hyena_bias': nrm((N_EVEN, HYENA_ORDER, B_WIDTH)),
        'odd_w_in': nrm((N_ODD, D_MODEL, ODD_IN), D_MODEL ** -0.5) * odd_cols,
        'odd_w_out': nrm((N_ODD, D_MODEL, D_MODEL), DN_BETA * D_MODEL ** -0.5),
        'ret_decay': ret_init + nrm((N_ODD, 2, C_HEADS), 0.1),
        'moe_router': nrm((DEPTH, D_MODEL, N_EXPERTS), D_MODEL ** -0.5),
        'moe_w1': nrm((DEPTH, N_EXPERTS, D_MODEL, EXPERT_FF), D_MODEL ** -0.5),
        'moe_w3': nrm((DEPTH, N_EXPERTS, D_MODEL, EXPERT_FF), D_MODEL ** -0.5),
        'moe_w2': nrm((DEPTH, N_EXPERTS, EXPERT_FF, D_MODEL), DN_BETA * EXPERT_FF ** -0.5),
    }


def reference(x_prompt, x_sample, state_hgrn, state_ret, c, c_ctx, ada_w, ada_b, ln_g, ln_b,
              even_w_in, even_w_out, hgrn_lb, hgrn_norm_w, hyena_conv_w, hyena_conv_b,
              hyena_w1, hyena_b1, hyena_w2, hyena_b2, hyena_w3, hyena_freq, hyena_bias,
              odd_w_in, odd_w_out, ret_decay, moe_router, moe_w1, moe_w3, moe_w2):
    lb_soft = jax.nn.softmax(hgrn_lb.astype(F32), axis=0)
    lb_all = jnp.cumsum(lb_soft, axis=0) - lb_soft[0]

    def trunk(h, cond, latent, hgrn_cache, ret_cache):
        bn = h.shape[0]
        hgrn_states, ret_states = [], []
        for l in range(DEPTH):
            j = l // 2
            sh1, sc1, g1, sh2, sc2, g2 = modulation(cond, ada_w[l], ada_b[l])
            u = h * (1.0 + sc1) + sh1
            if l % 2 == 0:
                s0 = hgrn_cache[:, j] if latent else jnp.zeros((bn, 2, A_HEADS, A_HEAD_DIM, A_HEAD_DIM), F32)
                mix, s_new = even_mixer(u, s0, even_w_in[j], even_w_out[j], lb_all[j], hgrn_norm_w[j],
                                        hyena_conv_w[j], hyena_conv_b[j], hyena_w1[j], hyena_b1[j],
                                        hyena_w2[j], hyena_b2[j], hyena_w3[j], hyena_freq[j], hyena_bias[j])
                hgrn_states.append(s_new)
            else:
                s0 = ret_cache[:, j] if latent else jnp.zeros((bn, 2, C_HEADS, C_HEAD_DIM, C_HEAD_DIM), F32)
                mix, s_new = retention_mixer(u, s0, latent, odd_w_in[j], odd_w_out[j], ret_decay[j])
                ret_states.append(s_new)
            h = layer_norm(DN_ALPHA * h + g1 * mix, ln_g[l, 0], ln_b[l, 0])
            u = h * (1.0 + sc2) + sh2
            h = layer_norm(DN_ALPHA * h + g2 * ec_moe(u, moe_router[l], moe_w1[l], moe_w3[l], moe_w2[l]),
                           ln_g[l, 1], ln_b[l, 1])
        return h, hgrn_states, ret_states

    y_prompt, hs_p, rs_p = trunk(x_prompt, c_ctx, False, None, None)
    new_state_hgrn = jnp.stack(hs_p, axis=1).astype(x_prompt.dtype)
    new_state_ret = jnp.stack(rs_p, axis=1).astype(x_prompt.dtype)
    y_sample, _, _ = trunk(x_sample, c, True, state_hgrn, state_ret)
    return (y_prompt, y_sample, new_state_hgrn, new_state_ret)
```

```python
import functools
import math

import jax
import jax.numpy as jnp
import numpy as np
from jax import lax
from jax.experimental import pallas as pl
from jax.experimental.pallas import tpu as pltpu

F32 = jnp.float32
BF16 = jnp.bfloat16

D_MODEL = 1024
DEPTH = 4
GRID_W = 64
A_WIDTH = D_MODEL // 2
A_HEADS = 4
A_HEAD_DIM = A_WIDTH // A_HEADS
F_MIN = 1e-30
B_WIDTH = D_MODEL - A_WIDTH
HYENA_ORDER = 2
FILTER_EMB = 33
SHORT_CONV = 3
DECAY_TARGET = 1e-2
FAST_DECAY = 0.3
SLOW_DECAY = 1.5
C_HEADS = 4
C_HEAD_DIM = D_MODEL // C_HEADS
ROPE_BASE = 10000.0
N_EXPERTS = 16
CAP_FACTOR = 2
LN_EPS = 1e-5
DN_ALPHA = (2 * DEPTH) ** 0.25

LANES = 128
VMEM_LIMIT = 56 * 1024 * 1024
ROW_TILE = 512
SCAN_TILE = 256
HGRN_SUB = 16


def _cparams(sem):
    return pltpu.CompilerParams(dimension_semantics=sem, vmem_limit_bytes=VMEM_LIMIT)


def _cond_index(tile, tile_rows, n_ctx_rows, lat_seq):
    row = tile * tile_rows
    return jnp.where(row < n_ctx_rows, 0, 1 + (row - n_ctx_rows) // lat_seq)


def _mod_kernel(c_ref, w_ref, b_ref, o_ref):
    c = c_ref[...]
    s = c * jax.nn.sigmoid(c)
    o_ref[0] = jnp.dot(s.astype(BF16), w_ref[0].astype(BF16), preferred_element_type=F32) + b_ref[0]


def modulation_table(cond, ada_w, ada_b):
    r, d = cond.shape
    depth, _, n = ada_w.shape
    tn = 1024
    return pl.pallas_call(
        _mod_kernel,
        grid=(depth, n // tn),
        in_specs=[pl.BlockSpec((r, d), lambda l, j: (0, 0)),
                  pl.BlockSpec((1, d, tn), lambda l, j: (l, 0, j)),
                  pl.BlockSpec((1, 1, tn), lambda l, j: (l, 0, j))],
        out_specs=pl.BlockSpec((1, r, tn), lambda l, j: (l, 0, j)),
        out_shape=jax.ShapeDtypeStruct((depth, r, n), F32),
        compiler_params=_cparams(("parallel", "parallel")),
        name="modulation",
    )(cond, ada_w, ada_b.reshape(depth, 1, n))


def _inproj_kernel(x_ref, mod_ref, w_ref, o_ref):
    shift = mod_ref[0, 0:1, :]
    scale = mod_ref[0, 1:2, :]
    u = (x_ref[...] * (1.0 + scale) + shift).astype(BF16)
    o_ref[...] = jnp.dot(u, w_ref[...].astype(BF16), preferred_element_type=F32)


def in_projection(x, mod, w, n_ctx_rows, lat_seq):
    m, d = x.shape
    n = w.shape[1]
    tm, tn = ROW_TILE, 1024
    cidx = functools.partial(_cond_index, tile_rows=tm, n_ctx_rows=n_ctx_rows, lat_seq=lat_seq)
    return pl.pallas_call(
        _inproj_kernel,
        grid=(n // tn, m // tm),
        in_specs=[pl.BlockSpec((tm, d), lambda j, i: (i, 0)),
                  pl.BlockSpec((1, 6, d), lambda j, i: (cidx(i), 0, 0)),
                  pl.BlockSpec((d, tn), lambda j, i: (0, j))],
        out_specs=pl.BlockSpec((tm, tn), lambda j, i: (i, j)),
        out_shape=jax.ShapeDtypeStruct((m, n), F32),
        compiler_params=_cparams(("parallel", "parallel")),
        name="in_projection",
    )(x, mod, w)


def _layer_norm_rows(z, g, b):
    mu = jnp.mean(z, axis=-1, keepdims=True)
    zc = z - mu
    var = jnp.mean(zc * zc, axis=-1, keepdims=True)
    return zc * lax.rsqrt(var + LN_EPS) * g + b


def _post_mixer(y, h_ref, mod_ref, w_ref, lng_ref, lnb_ref, r_ref, h_out, u_out, aff_out):
    gate1 = mod_ref[0, 2:3, :]
    shift2 = mod_ref[0, 3:4, :]
    scale2 = mod_ref[0, 4:5, :]
    mix = jnp.dot(y.astype(BF16), w_ref[...].astype(BF16), preferred_element_type=F32)
    hn = _layer_norm_rows(DN_ALPHA * h_ref[...] + gate1 * mix, lng_ref[...], lnb_ref[...])
    h_out[...] = hn
    u2 = (hn * (1.0 + scale2) + shift2).astype(BF16)
    u_out[...] = u2
    logits = lax.dot_general(r_ref[...].astype(BF16), u2, (((1,), (1,)), ((), ())), preferred_element_type=F32)
    mx = jnp.max(logits, axis=0, keepdims=True)
    ex = jnp.exp(logits - mx)
    aff_out[...] = ex / jnp.sum(ex, axis=0, keepdims=True)


def _post_even_kernel(of_ref, ob_ref, g_ref, nw_ref, yb_ref, h_ref, mod_ref, w_ref, lng_ref, lnb_ref, r_ref,
                      h_out, u_out, aff_out):
    o = of_ref[...] + ob_ref[...]
    g = g_ref[...]
    parts = []
    for hd in range(A_HEADS):
        sl = slice(hd * A_HEAD_DIM, (hd + 1) * A_HEAD_DIM)
        oh = o[:, sl]
        parts.append(oh * lax.rsqrt(jnp.mean(oh * oh, axis=-1, keepdims=True) + LN_EPS))
    ya = jnp.concatenate(parts, axis=-1) * nw_ref[...] * (g * jax.nn.sigmoid(g))
    y = jnp.concatenate([ya, yb_ref[...]], axis=-1)
    _post_mixer(y, h_ref, mod_ref, w_ref, lng_ref, lnb_ref, r_ref, h_out, u_out, aff_out)


def _post_odd_kernel(of_ref, ob_ref, g_ref, h_ref, mod_ref, w_ref, lng_ref, lnb_ref, r_ref,
                     h_out, u_out, aff_out):
    o = of_ref[...] + ob_ref[...]
    g = g_ref[...]
    parts = []
    for hd in range(C_HEADS):
        sl = slice(hd * C_HEAD_DIM, (hd + 1) * C_HEAD_DIM)
        oh = o[:, sl]
        mu = jnp.mean(oh, axis=-1, keepdims=True)
        oc = oh - mu
        parts.append(oc * lax.rsqrt(jnp.mean(oc * oc, axis=-1, keepdims=True) + LN_EPS))
    y = jnp.concatenate(parts, axis=-1) * (g * jax.nn.sigmoid(g))
    _post_mixer(y, h_ref, mod_ref, w_ref, lng_ref, lnb_ref, r_ref, h_out, u_out, aff_out)


def post_mixer(even, scan_out, proj, extra, h, mod, w_out, ln_g, ln_b, router, n_ctx_rows, lat_seq):
    m, d = h.shape
    e = router.shape[1]
    tm = ROW_TILE
    cidx = functools.partial(_cond_index, tile_rows=tm, n_ctx_rows=n_ctx_rows, lat_seq=lat_seq)
    width = scan_out.shape[-1]
    row = lambda i: (i, 0)
    fixed = lambda i: (0, 0)
    common_specs = [pl.BlockSpec((tm, d), row),
                    pl.BlockSpec((1, 6, d), lambda i: (cidx(i), 0, 0)),
                    pl.BlockSpec((d, d), fixed),
                    pl.BlockSpec((1, d), fixed),
                    pl.BlockSpec((1, d), fixed),
                    pl.BlockSpec((e, d), fixed)]
    common_args = [h, mod, w_out, ln_g.reshape(1, d), ln_b.reshape(1, d), router.T]
    scan_specs = [pl.BlockSpec((None, tm, width), lambda i: (0, i, 0)),
                  pl.BlockSpec((None, tm, width), lambda i: (1, i, 0))]
    if even:
        norm_w, y_b = extra
        gate_col = 4 * A_WIDTH // width
        specs = scan_specs + [pl.BlockSpec((tm, width), lambda i: (i, gate_col)),
                              pl.BlockSpec((1, width), fixed),
                              pl.BlockSpec((tm, B_WIDTH), row)] + common_specs
        args = [scan_out, scan_out, proj, norm_w.reshape(1, width), y_b] + common_args
        body = _post_even_kernel
    else:
        gate_col = 3
        specs = scan_specs + [pl.BlockSpec((tm, width), lambda i: (i, gate_col))] + common_specs
        args = [scan_out, scan_out, proj] + common_args
        body = _post_odd_kernel
    return pl.pallas_call(
        body,
        grid=(m // tm,),
        in_specs=specs,
        out_specs=[pl.BlockSpec((tm, d), row), pl.BlockSpec((tm, d), row), pl.BlockSpec((e, tm), lambda i: (0, i))],
        out_shape=[jax.ShapeDtypeStruct((m, d), F32), jax.ShapeDtypeStruct((m, d), BF16),
                   jax.ShapeDtypeStruct((e, m), F32)],
        compiler_params=_cparams(("parallel",)),
        name="post_mixer_even" if even else "post_mixer_odd",
    )(*args)


def _post_moe_kernel(m_ref, h_ref, mod_ref, lng_ref, lnb_ref, o_ref):
    gate2 = mod_ref[0, 5:6, :]
    o_ref[...] = _layer_norm_rows(DN_ALPHA * h_ref[...] + gate2 * m_ref[...], lng_ref[...], lnb_ref[...])


def post_moe(moe_out, h, mod, ln_g, ln_b, n_ctx_rows, lat_seq):
    m, d = h.shape
    tm = ROW_TILE
    cidx = functools.partial(_cond_index, tile_rows=tm, n_ctx_rows=n_ctx_rows, lat_seq=lat_seq)
    row = lambda i: (i, 0)
    fixed = lambda i: (0, 0)
    return pl.pallas_call(
        _post_moe_kernel,
        grid=(m // tm,),
        in_specs=[pl.BlockSpec((tm, d), row), pl.BlockSpec((tm, d), row),
                  pl.BlockSpec((1, 6, d), lambda i: (cidx(i), 0, 0)),
                  pl.BlockSpec((1, d), fixed), pl.BlockSpec((1, d), fixed)],
        out_specs=pl.BlockSpec((tm, d), row),
        out_shape=jax.ShapeDtypeStruct((m, d), F32),
        compiler_params=_cparams(("parallel",)),
        name="post_moe",
    )(moe_out, h, mod, ln_g.reshape(1, d), ln_b.reshape(1, d))


def _rope_halves(x, cos, sin_signed):
    outs = []
    for p in range(2):
        sl = slice(p * LANES, (p + 1) * LANES)
        xp = x[:, sl]
        outs.append(xp * cos[:, sl] + pltpu.roll(xp, LANES // 2, axis=1) * sin_signed[:, sl])
    return jnp.concatenate(outs, axis=-1)


def _retention_kernel(lg_ref, q_ref, k_ref, v_ref, cos_ref, sin_ref, s0_ref, o_ref, sfin_ref, s_scr, *, rope):
    d = pl.program_id(1)
    t = pl.program_id(2)
    c = q_ref.shape[0]
    dk = C_HEAD_DIM

    @pl.when(t == 0)
    def _():
        s_scr[...] = s0_ref[0, 0]

    ti = lax.broadcasted_iota(jnp.int32, (c, c), 0)
    si = lax.broadcasted_iota(jnp.int32, (c, c), 1)
    diff = jnp.where(d == 0, ti - si, si - ti).astype(F32)
    jt = lax.broadcasted_iota(jnp.int32, (c, dk), 0)
    eq = jnp.where(d == 0, jt + 1, c - jt).astype(F32)
    ek = jnp.where(d == 0, c - 1 - jt, jt).astype(F32)
    for hd in range(C_HEADS):
        lg = lg_ref[d, hd]
        sl = slice(hd * dk, (hd + 1) * dk)
        q = q_ref[:, sl]
        k = k_ref[:, sl] * (dk ** -0.5)
        if rope:
            q = _rope_halves(q, cos_ref[...], sin_ref[...])
            k = _rope_halves(k, cos_ref[...], sin_ref[...])
        v = v_ref[:, sl].astype(BF16)
        rel = jnp.where(diff >= 0.0, jnp.exp(lg * jnp.maximum(diff, 0.0)), 0.0)
        att = lax.dot_general(q.astype(BF16), k.astype(BF16), (((1,), (1,)), ((), ())),
                              preferred_element_type=F32) * rel
        intra = jnp.dot(att.astype(BF16), v, preferred_element_type=F32)
        s_old = s_scr[hd]
        inter = jnp.dot((q * jnp.exp(lg * eq)).astype(BF16), s_old.astype(BF16), preferred_element_type=F32)
        o_ref[:, sl] = inter + intra
        kd = (k * jnp.exp(lg * ek)).astype(BF16)
        s_new = s_old * jnp.exp(lg * c) + lax.dot_general(kd, v, (((0,), (0,)), ((), ())),
                                                          preferred_element_type=F32)
        s_scr[hd] = s_new

    @pl.when(t == pl.num_programs(2) - 1)
    def _():
        sfin_ref[0, 0] = s_scr[...]


def retention_scan(proj, row0, n_seq, seq_len, log_gamma, s0, rope_tabs):
    d = D_MODEL
    c = SCAN_TILE
    nt = seq_len // c
    t0 = row0 // c
    rope = rope_tabs is not None
    if rope:
        cos, sin = rope_tabs
    else:
        cos = sin = jnp.zeros((c, C_HEAD_DIM), F32)
    tile = lambda dd, t: t + dd * (nt - 1 - 2 * t)
    tok = lambda col: pl.BlockSpec((c, d), lambda b, dd, t: (t0 + b * nt + tile(dd, t), col))
    rope_spec = pl.BlockSpec((c, C_HEAD_DIM), (lambda b, dd, t: (tile(dd, t), 0)) if rope else (lambda b, dd, t: (0, 0)))
    shared_s0 = s0.shape[0] == 1
    st_shape = (1, 1, C_HEADS, C_HEAD_DIM, C_HEAD_DIM)
    return pl.pallas_call(
        functools.partial(_retention_kernel, rope=rope),
        grid=(n_seq, 2, nt),
        in_specs=[pl.BlockSpec(memory_space=pltpu.SMEM), tok(0), tok(1), tok(2), rope_spec, rope_spec,
                  pl.BlockSpec(st_shape, lambda b, dd, t: (0 if shared_s0 else b, dd, 0, 0, 0))],
        out_specs=[pl.BlockSpec((None, c, d), lambda b, dd, t: (dd, b * nt + tile(dd, t), 0)),
                   pl.BlockSpec(st_shape, lambda b, dd, t: (b, dd, 0, 0, 0))],
        out_shape=[jax.ShapeDtypeStruct((2, n_seq * seq_len, d), F32),
                   jax.ShapeDtypeStruct((n_seq, 2, C_HEADS, C_HEAD_DIM, C_HEAD_DIM), F32)],
        scratch_shapes=[pltpu.VMEM((C_HEADS, C_HEAD_DIM, C_HEAD_DIM), F32)],
        compiler_params=_cparams(("parallel", "parallel", "arbitrary")),
        name="retention_scan",
    )(log_gamma, proj, proj, proj, cos, sin, s0)


def _split3(x):
    hi = x.astype(BF16)
    r1 = x - hi.astype(F32)
    mid = r1.astype(BF16)
    lo = (r1 - mid.astype(F32)).astype(BF16)
    return hi, mid, lo


def _hgrn_kernel(q_ref, v_ref, z_ref, lb_ref, s0_ref, o_ref, sfin_ref, st_scr, qe_scr, k_scr, b_scr, *, sub):
    d = pl.program_id(1)
    t = pl.program_id(2)
    tl, w = q_ref.shape
    dh = A_HEAD_DIM
    nsub = tl // sub

    @pl.when(t == 0)
    def _():
        for hd in range(A_HEADS):
            st_scr[hd] = s0_ref[0, 0, hd].T

    lb = lb_ref[0]
    z = z_ref[...]
    sg = jax.nn.sigmoid(z)
    lf = jnp.log(jnp.maximum(lb + (1.0 - lb) * sg, F_MIN))
    k_scr[...] = (1.0 - lb) * jax.nn.sigmoid(-z)
    ti = lax.broadcasted_iota(jnp.int32, (tl, tl), 0)
    si = lax.broadcasted_iota(jnp.int32, (tl, tl), 1)
    sign = jnp.where(d == 0, 1, -1)
    seen = ((ti - si) * sign >= 0) & ((ti // sub) == (si // sub))
    tri = jnp.where(seen, 1.0, 0.0).astype(BF16)
    b = sum(jnp.dot(tri, part, preferred_element_type=F32) for part in _split3(lf))
    b_scr[...] = b
    q = q_ref[...]
    q = q * jax.nn.sigmoid(q)
    qe_scr[...] = q * jnp.exp(b)

    ones = jnp.ones((dh, dh), BF16)
    rowi = lax.broadcasted_iota(jnp.int32, (sub, dh), 0)

    def step(i, carry):
        blk = jnp.where(d == 0, i, nsub - 1 - i)
        r0 = pl.multiple_of(blk * sub, sub)
        rows = pl.ds(r0, sub)
        for hd in range(A_HEADS):
            sl = slice(hd * dh, (hd + 1) * dh)
            qs = q_ref[rows, sl]
            qs = qs * jax.nn.sigmoid(qs)
            ks = k_scr[rows, sl]
            vs = v_ref[rows, sl]
            bs = b_scr[rows, sl]
            st = st_scr[hd]
            inter = lax.dot_general(qe_scr[rows, sl].astype(BF16), st.astype(BF16), (((1,), (1,)), ((), ())),
                                    preferred_element_type=F32)
            es = []
            for s in range(sub):
                vis = (rowi - s) * sign >= 0
                e = qs * ks[s:s + 1, :] * jnp.exp(jnp.minimum(bs - bs[s:s + 1, :], 0.0))
                es.append(jnp.where(vis, e, 0.0).astype(BF16))
            att = jnp.dot(jnp.concatenate(es, axis=0), ones, preferred_element_type=F32)
            intra = att[0:sub] * vs[0:1, :]
            for s in range(1, sub):
                intra = intra + att[s * sub:(s + 1) * sub] * vs[s:s + 1, :]
            o_ref[rows, sl] = inter + intra
            b_end = jnp.where(d == 0, bs[sub - 1:sub, :], bs[0:1, :])
            kd = (ks * jnp.exp(b_end - bs)).astype(BF16)
            st_scr[hd] = st * jnp.exp(b_end) + lax.dot_general(vs.astype(BF16), kd, (((0,), (0,)), ((), ())),
                                                               preferred_element_type=F32)
        return carry

    lax.fori_loop(0, nsub, step, 0)

    @pl.when(t == pl.num_programs(2) - 1)
    def _():
        for hd in range(A_HEADS):
            sfin_ref[0, 0, hd] = st_scr[hd].T


def hgrn_scan(proj, row0, n_seq, seq_len, lb, s0):
    w = A_WIDTH
    tl = SCAN_TILE
    nt = seq_len // tl
    t0 = row0 // tl
    tile = lambda dd, t: t + dd * (nt - 1 - 2 * t)
    tok = lambda colfn: pl.BlockSpec((tl, w), lambda b, dd, t: (t0 + b * nt + tile(dd, t), colfn(dd)))
    shared_s0 = s0.shape[0] == 1
    st_shape = (1, 1, A_HEADS, A_HEAD_DIM, A_HEAD_DIM)
    return pl.pallas_call(
        functools.partial(_hgrn_kernel, sub=HGRN_SUB),
        grid=(n_seq, 2, nt),
        in_specs=[tok(lambda dd: 0), tok(lambda dd: 1), tok(lambda dd: 2 + dd),
                  pl.BlockSpec((1, 1, w), lambda b, dd, t: (dd, 0, 0)),
                  pl.BlockSpec(st_shape, lambda b, dd, t: (0 if shared_s0 else b, dd, 0, 0, 0))],
        out_specs=[pl.BlockSpec((None, tl, w), lambda b, dd, t: (dd, b * nt + tile(dd, t), 0)),
                   pl.BlockSpec(st_shape, lambda b, dd, t: (b, dd, 0, 0, 0))],
        out_shape=[jax.ShapeDtypeStruct((2, n_seq * seq_len, w), F32),
                   jax.ShapeDtypeStruct((n_seq, 2, A_HEADS, A_HEAD_DIM, A_HEAD_DIM), F32)],
        scratch_shapes=[pltpu.VMEM((A_HEADS, A_HEAD_DIM, A_HEAD_DIM), F32),
                        pltpu.VMEM((tl, w), F32), pltpu.VMEM((tl, w), F32), pltpu.VMEM((tl, w), F32)],
        compiler_params=_cparams(("parallel", "parallel", "arbitrary")),
        name="hgrn_scan",
    )(proj, proj, proj, lb.reshape(2, 1, w), s0)


def _expert_kernel(x_ref, w1_ref, w3_ref, w2_ref, o_ref):
    x = x_ref[0]
    a = jnp.dot(x, w1_ref[0].astype(BF16), preferred_element_type=F32)
    g = jnp.dot(x, w3_ref[0].astype(BF16), preferred_element_type=F32)
    hid = (a * jax.nn.sigmoid(a) * g).astype(BF16)
    o_ref[0] = jnp.dot(hid, w2_ref[0].astype(BF16), preferred_element_type=F32)


def expert_ffn(xs, w1, w3, w2):
    e, m, d = xs.shape
    f = w1.shape[2]
    tm = ROW_TILE
    return pl.pallas_call(
        _expert_kernel,
        grid=(e, m // tm),
        in_specs=[pl.BlockSpec((1, tm, d), lambda ei, i: (ei, i, 0)),
                  pl.BlockSpec((1, d, f), lambda ei, i: (ei, 0, 0)),
                  pl.BlockSpec((1, d, f), lambda ei, i: (ei, 0, 0)),
                  pl.BlockSpec((1, f, d), lambda ei, i: (ei, 0, 0))],
        out_specs=pl.BlockSpec((1, tm, d), lambda ei, i: (ei, i, 0)),
        out_shape=jax.ShapeDtypeStruct((e, m, d), F32),
        compiler_params=_cparams(("parallel", "arbitrary")),
        name="expert_ffn",
    )(xs, w1, w3, w2)


def _short_conv(x, w, b):
    L = x.shape[1]
    pad = SHORT_CONV // 2
    xp = jnp.pad(x, ((0, 0), (pad, pad), (0, 0)))
    return sum(xp[:, j:j + L] * w[j] for j in range(SHORT_CONV)) + b


def _hyena_filters(L, w1, b1, w2, b2, w3, freq):
    t = jnp.linspace(0.0, 1.0, L, dtype=F32)[:, None]
    bands = (FILTER_EMB - 1) // 2
    ang = (2.0 * math.pi / L) * jnp.arange(L, dtype=F32)[:, None] * jnp.linspace(1e-4, bands - 1, bands, dtype=F32)[None, :]
    z = jnp.concatenate([t, jnp.cos(ang), -jnp.sin(ang)], axis=-1)
    hdn = jnp.sin(freq[0] * (z @ w1 + b1))
    hdn = jnp.sin(freq[1] * (hdn @ w2 + b2))
    filt = (hdn @ w3).reshape(L, HYENA_ORDER, 2, B_WIDTH)
    deltas = jnp.abs(jnp.linspace(math.log(DECAY_TARGET) / SLOW_DECAY, math.log(DECAY_TARGET) / FAST_DECAY, B_WIDTH, dtype=F32))
    return filt * jnp.exp(-t * deltas)[:, None, None, :]


def _bidir_long_conv(u, h_fwd, h_bwd, dd):
    L = u.shape[1]
    k = jnp.concatenate([h_fwd, jnp.zeros_like(h_fwd[:1]), jnp.flip(h_bwd[1:], axis=0)], axis=0)
    y = jnp.fft.irfft(jnp.fft.rfft(u, n=2 * L, axis=1) * jnp.fft.rfft(k, axis=0)[None], n=2 * L, axis=1)[:, :L]
    return y + u * dd


def _hyena_jax(hp, conv_w, conv_b, fw1, fb1, fw2, fb2, fw3, freq, fbias):
    L = hp.shape[1]
    hv, hx1, hx2 = jnp.split(_short_conv(hp, conv_w, conv_b), 3, axis=-1)
    filt = _hyena_filters(L, fw1, fb1, fw2, fb2, fw3, freq)
    z = hv
    for o, gate in enumerate((hx1, hx2)):
        z = gate * _bidir_long_conv(z, filt[:, o, 0], filt[:, o, 1], fbias[o])
    return z


def _route_jax(aff_t, u2, n_seq, seq_len, row0):
    e = aff_t.shape[0]
    cap = CAP_FACTOR * seq_len // e
    a = aff_t[:, row0:row0 + n_seq * seq_len].reshape(e, n_seq, seq_len)
    gate, idx = lax.top_k(a, cap)
    flat = row0 + idx + (jnp.arange(n_seq) * seq_len)[None, :, None]
    return gate.reshape(e, n_seq * cap), flat.reshape(e, n_seq * cap)


def _rope_tables(seq_len):
    n_rows = seq_len // GRID_W
    rows = np.repeat(np.arange(n_rows), GRID_W).astype(np.float32)
    cols = np.tile(np.arange(GRID_W), n_rows).astype(np.float32)
    quarter = C_HEAD_DIM // 4
    inv = (ROPE_BASE ** (-np.arange(quarter, dtype=np.float32) / quarter)).astype(np.float32)
    cos_parts, sin_parts = [], []
    for pos in (rows, cols):
        a = jnp.asarray(pos)[:, None] * jnp.asarray(inv)[None, :]
        cos_parts += [jnp.cos(a), jnp.cos(a)]
        sin_parts += [-jnp.sin(a), jnp.sin(a)]
    return jnp.concatenate(cos_parts, axis=-1), jnp.concatenate(sin_parts, axis=-1)


def kernel(x_prompt, x_sample, state_hgrn, state_ret, c, c_ctx, ada_w, ada_b, ln_g, ln_b, even_w_in, even_w_out, hgrn_lb, hgrn_norm_w, hyena_conv_w, hyena_conv_b, hyena_w1, hyena_b1, hyena_w2, hyena_b2, hyena_w3, hyena_freq, hyena_bias, odd_w_in, odd_w_out, ret_decay, moe_router, moe_w1, moe_w3, moe_w2):
    nb, seq, d = x_prompt.shape
    nl, lseq, _ = x_sample.shape
    n_ctx = nb * seq
    n_lat = nl * lseq
    h = jnp.concatenate([x_prompt.reshape(n_ctx, d), x_sample.reshape(n_lat, d)], axis=0)

    cond = jnp.concatenate([c_ctx[None, :], c], axis=0)
    mods = modulation_table(cond, ada_w, ada_b).reshape(DEPTH, 1 + nl, 6, d)

    lb_soft = jax.nn.softmax(hgrn_lb.astype(F32), axis=0)
    lb_all = jnp.cumsum(lb_soft, axis=0) - lb_soft[0]
    rope_tabs = _rope_tables(lseq)
    zero_hgrn = jnp.zeros((1, 2, A_HEADS, A_HEAD_DIM, A_HEAD_DIM), F32)
    zero_ret = jnp.zeros((1, 2, C_HEADS, C_HEAD_DIM, C_HEAD_DIM), F32)

    hgrn_states, ret_states = [], []
    for l in range(DEPTH):
        j = l // 2
        mod = mods[l]
        if l % 2 == 0:
            proj = in_projection(h, mod, even_w_in[j], n_ctx, lseq)
            o_p, s_p = hgrn_scan(proj, 0, nb, seq, lb_all[j], zero_hgrn)
            o_l, _ = hgrn_scan(proj, n_ctx, nl, lseq, lb_all[j], state_hgrn[:, j])
            hgrn_states.append(s_p)
            scan_out = jnp.concatenate([o_p, o_l], axis=1)
            hy_args = (hyena_conv_w[j], hyena_conv_b[j], hyena_w1[j], hyena_b1[j], hyena_w2[j], hyena_b2[j],
                       hyena_w3[j], hyena_freq[j], hyena_bias[j])
            hp = proj[:, 5 * A_WIDTH:]
            yb_p = _hyena_jax(hp[:n_ctx].reshape(nb, seq, -1), *hy_args).reshape(n_ctx, B_WIDTH)
            yb_l = _hyena_jax(hp[n_ctx:].reshape(nl, lseq, -1), *hy_args).reshape(n_lat, B_WIDTH)
            y_b = jnp.concatenate([yb_p, yb_l], axis=0)
            h, u2, aff_t = post_mixer(True, scan_out, proj, (hgrn_norm_w[j], y_b), h, mod, even_w_out[j],
                                      ln_g[l, 0], ln_b[l, 0], moe_router[l], n_ctx, lseq)
        else:
            proj = in_projection(h, mod, odd_w_in[j], n_ctx, lseq)
            log_gamma = jax.nn.log_sigmoid(ret_decay[j].astype(F32))
            o_p, s_p = retention_scan(proj, 0, nb, seq, log_gamma, zero_ret, None)
            o_l, _ = retention_scan(proj, n_ctx, nl, lseq, log_gamma, state_ret[:, j], rope_tabs)
            ret_states.append(s_p)
            scan_out = jnp.concatenate([o_p, o_l], axis=1)
            h, u2, aff_t = post_mixer(False, scan_out, proj, None, h, mod, odd_w_out[j],
                                      ln_g[l, 0], ln_b[l, 0], moe_router[l], n_ctx, lseq)
        g_p, i_p = _route_jax(aff_t, u2, nb, seq, 0)
        g_l, i_l = _route_jax(aff_t, u2, nl, lseq, n_ctx)
        gate = jnp.concatenate([g_p, g_l], axis=1)
        idx = jnp.concatenate([i_p, i_l], axis=1)
        xs = u2[idx]
        ys = expert_ffn(xs, moe_w1[l], moe_w3[l], moe_w2[l]) * gate[..., None]
        moe_out = jnp.zeros((n_ctx + n_lat, d), F32).at[idx.reshape(-1)].add(ys.reshape(-1, d))
        h = post_moe(moe_out, h, mod, ln_g[l, 1], ln_b[l, 1], n_ctx, lseq)

    y_prompt = h[:n_ctx].reshape(nb, seq, d)
    y_sample = h[n_ctx:].reshape(nl, lseq, d)
    new_state_hgrn = jnp.stack(hgrn_states, axis=1)
    new_state_ret = jnp.stack(ret_states, axis=1)
    return (y_prompt, y_sample, new_state_hgrn, new_state_ret)
```

```python
import functools
import math

import jax
import jax.numpy as jnp
import numpy as np
from jax import lax
from jax.experimental import pallas as pl
from jax.experimental.pallas import tpu as pltpu

F32 = jnp.float32
BF16 = jnp.bfloat16

D_MODEL = 1024
DEPTH = 4
GRID_W = 64
A_WIDTH = D_MODEL // 2
A_HEADS = 4
A_HEAD_DIM = A_WIDTH // A_HEADS
F_MIN = 1e-30
B_WIDTH = D_MODEL - A_WIDTH
HYENA_ORDER = 2
FILTER_EMB = 33
SHORT_CONV = 3
DECAY_TARGET = 1e-2
FAST_DECAY = 0.3
SLOW_DECAY = 1.5
C_HEADS = 4
C_HEAD_DIM = D_MODEL // C_HEADS
ROPE_BASE = 10000.0
N_EXPERTS = 16
CAP_FACTOR = 2
LN_EPS = 1e-5
DN_ALPHA = (2 * DEPTH) ** 0.25

LANES = 128
VMEM_LIMIT = 56 * 1024 * 1024
ROW_TILE = 512
SCAN_TILE = 256
HGRN_SUB = 16


def _cparams(sem):
    return pltpu.CompilerParams(dimension_semantics=sem, vmem_limit_bytes=VMEM_LIMIT)


def _cond_index(tile, tile_rows, n_ctx_rows, lat_seq):
    row = tile * tile_rows
    return jnp.where(row < n_ctx_rows, 0, 1 + (row - n_ctx_rows) // lat_seq)


def _mod_kernel(c_ref, w_ref, b_ref, o_ref):
    c = c_ref[...]
    s = c * jax.nn.sigmoid(c)
    o_ref[0] = jnp.dot(s.astype(BF16), w_ref[0].astype(BF16), preferred_element_type=F32) + b_ref[0]


def modulation_table(cond, ada_w, ada_b):
    r, d = cond.shape
    depth, _, n = ada_w.shape
    tn = 1024
    return pl.pallas_call(
        _mod_kernel,
        grid=(depth, n // tn),
        in_specs=[pl.BlockSpec((r, d), lambda l, j: (0, 0)),
                  pl.BlockSpec((1, d, tn), lambda l, j: (l, 0, j)),
                  pl.BlockSpec((1, 1, tn), lambda l, j: (l, 0, j))],
        out_specs=pl.BlockSpec((1, r, tn), lambda l, j: (l, 0, j)),
        out_shape=jax.ShapeDtypeStruct((depth, r, n), F32),
        compiler_params=_cparams(("parallel", "parallel")),
        name="modulation",
    )(cond, ada_w, ada_b.reshape(depth, 1, n))


def _inproj_kernel(x_ref, mod_ref, w_ref, o_ref):
    shift = mod_ref[0, 0:1, :]
    scale = mod_ref[0, 1:2, :]
    u = (x_ref[...] * (1.0 + scale) + shift).astype(BF16)
    o_ref[...] = jnp.dot(u, w_ref[...].astype(BF16), preferred_element_type=F32)


def in_projection(x, mod, w, n_ctx_rows, lat_seq):
    m, d = x.shape
    n = w.shape[1]
    tm, tn = ROW_TILE, 1024
    cidx = functools.partial(_cond_index, tile_rows=tm, n_ctx_rows=n_ctx_rows, lat_seq=lat_seq)
    return pl.pallas_call(
        _inproj_kernel,
        grid=(n // tn, m // tm),
        in_specs=[pl.BlockSpec((tm, d), lambda j, i: (i, 0)),
                  pl.BlockSpec((1, 6, d), lambda j, i: (cidx(i), 0, 0)),
                  pl.BlockSpec((d, tn), lambda j, i: (0, j))],
        out_specs=pl.BlockSpec((tm, tn), lambda j, i: (i, j)),
        out_shape=jax.ShapeDtypeStruct((m, n), F32),
        compiler_params=_cparams(("parallel", "parallel")),
        name="in_projection",
    )(x, mod, w)


def _layer_norm_rows(z, g, b):
    mu = jnp.mean(z, axis=-1, keepdims=True)
    zc = z - mu
    var = jnp.mean(zc * zc, axis=-1, keepdims=True)
    return zc * lax.rsqrt(var + LN_EPS) * g + b


def _post_mixer(y, h_ref, mod_ref, w_ref, lng_ref, lnb_ref, r_ref, h_out, u_out, aff_out):
    gate1 = mod_ref[0, 2:3, :]
    shift2 = mod_ref[0, 3:4, :]
    scale2 = mod_ref[0, 4:5, :]
    mix = jnp.dot(y.astype(BF16), w_ref[...].astype(BF16), preferred_element_type=F32)
    hn = _layer_norm_rows(DN_ALPHA * h_ref[...] + gate1 * mix, lng_ref[...], lnb_ref[...])
    h_out[...] = hn
    u2 = (hn * (1.0 + scale2) + shift2).astype(BF16)
    u_out[...] = u2
    logits = lax.dot_general(r_ref[...].astype(BF16), u2, (((1,), (1,)), ((), ())), preferred_element_type=F32)
    mx = jnp.max(logits, axis=0, keepdims=True)
    ex = jnp.exp(logits - mx)
    aff_out[...] = ex / jnp.sum(ex, axis=0, keepdims=True)


def _post_even_kernel(of_ref, ob_ref, g_ref, nw_ref, yb_ref, h_ref, mod_ref, w_ref, lng_ref, lnb_ref, r_ref,
                      h_out, u_out, aff_out):
    o = of_ref[...] + ob_ref[...]
    g = g_ref[...]
    parts = []
    for hd in range(A_HEADS):
        sl = slice(hd * A_HEAD_DIM, (hd + 1) * A_HEAD_DIM)
        oh = o[:, sl]
        parts.append(oh * lax.rsqrt(jnp.mean(oh * oh, axis=-1, keepdims=True) + LN_EPS))
    ya = jnp.concatenate(parts, axis=-1) * nw_ref[...] * (g * jax.nn.sigmoid(g))
    y = jnp.concatenate([ya, yb_ref[...]], axis=-1)
    _post_mixer(y, h_ref, mod_ref, w_ref, lng_ref, lnb_ref, r_ref, h_out, u_out, aff_out)


def _post_odd_kernel(of_ref, ob_ref, g_ref, h_ref, mod_ref, w_ref, lng_ref, lnb_ref, r_ref,
                     h_out, u_out, aff_out):
    o = of_ref[...] + ob_ref[...]
    g = g_ref[...]
    parts = []
    for hd in range(C_HEADS):
        sl = slice(hd * C_HEAD_DIM, (hd + 1) * C_HEAD_DIM)
        oh = o[:, sl]
        mu = jnp.mean(oh, axis=-1, keepdims=True)
        oc = oh - mu
        parts.append(oc * lax.rsqrt(jnp.mean(oc * oc, axis=-1, keepdims=True) + LN_EPS))
    y = jnp.concatenate(parts, axis=-1) * (g * jax.nn.sigmoid(g))
    _post_mixer(y, h_ref, mod_ref, w_ref, lng_ref, lnb_ref, r_ref, h_out, u_out, aff_out)


def post_mixer(even, scan_out, proj, extra, h, mod, w_out, ln_g, ln_b, router, n_ctx_rows, lat_seq):
    m, d = h.shape
    e = router.shape[1]
    tm = ROW_TILE
    cidx = functools.partial(_cond_index, tile_rows=tm, n_ctx_rows=n_ctx_rows, lat_seq=lat_seq)
    width = scan_out.shape[-1]
    row = lambda i: (i, 0)
    fixed = lambda i: (0, 0)
    common_specs = [pl.BlockSpec((tm, d), row),
                    pl.BlockSpec((1, 6, d), lambda i: (cidx(i), 0, 0)),
                    pl.BlockSpec((d, d), fixed),
                    pl.BlockSpec((1, d), fixed),
                    pl.BlockSpec((1, d), fixed),
                    pl.BlockSpec((e, d), fixed)]
    common_args = [h, mod, w_out, ln_g.reshape(1, d), ln_b.reshape(1, d), router.T]
    scan_specs = [pl.BlockSpec((None, tm, width), lambda i: (0, i, 0)),
                  pl.BlockSpec((None, tm, width), lambda i: (1, i, 0))]
    if even:
        norm_w, y_b = extra
        gate_col = 4 * A_WIDTH // width
        specs = scan_specs + [pl.BlockSpec((tm, width), lambda i: (i, gate_col)),
                              pl.BlockSpec((1, width), fixed),
                              pl.BlockSpec((tm, B_WIDTH), row)] + common_specs
        args = [scan_out, scan_out, proj, norm_w.reshape(1, width), y_b] + common_args
        body = _post_even_kernel
    else:
        gate_col = 3
        specs = scan_specs + [pl.BlockSpec((tm, width), lambda i: (i, gate_col))] + common_specs
        args = [scan_out, scan_out, proj] + common_args
        body = _post_odd_kernel
    return pl.pallas_call(
        body,
        grid=(m // tm,),
        in_specs=specs,
        out_specs=[pl.BlockSpec((tm, d), row), pl.BlockSpec((tm, d), row), pl.BlockSpec((e, tm), lambda i: (0, i))],
        out_shape=[jax.ShapeDtypeStruct((m, d), F32), jax.ShapeDtypeStruct((m, d), BF16),
                   jax.ShapeDtypeStruct((e, m), F32)],
        compiler_params=_cparams(("parallel",)),
        name="post_mixer_even" if even else "post_mixer_odd",
    )(*args)


def _post_moe_kernel(m_ref, h_ref, mod_ref, lng_ref, lnb_ref, o_ref):
    gate2 = mod_ref[0, 5:6, :]
    o_ref[...] = _layer_norm_rows(DN_ALPHA * h_ref[...] + gate2 * m_ref[...], lng_ref[...], lnb_ref[...])


def post_moe(moe_out, h, mod, ln_g, ln_b, n_ctx_rows, lat_seq):
    m, d = h.shape
    tm = ROW_TILE
    cidx = functools.partial(_cond_index, tile_rows=tm, n_ctx_rows=n_ctx_rows, lat_seq=lat_seq)
    row = lambda i: (i, 0)
    fixed = lambda i: (0, 0)
    return pl.pallas_call(
        _post_moe_kernel,
        grid=(m // tm,),
        in_specs=[pl.BlockSpec((tm, d), row), pl.BlockSpec((tm, d), row),
                  pl.BlockSpec((1, 6, d), lambda i: (cidx(i), 0, 0)),
                  pl.BlockSpec((1, d), fixed), pl.BlockSpec((1, d), fixed)],
        out_specs=pl.BlockSpec((tm, d), row),
        out_shape=jax.ShapeDtypeStruct((m, d), F32),
        compiler_params=_cparams(("parallel",)),
        name="post_moe",
    )(moe_out, h, mod, ln_g.reshape(1, d), ln_b.reshape(1, d))


def _rope_halves(x, cos, sin_signed):
    outs = []
    for p in range(2):
        sl = slice(p * LANES, (p + 1) * LANES)
        xp = x[:, sl]
        outs.append(xp * cos[:, sl] + pltpu.roll(xp, LANES // 2, axis=1) * sin_signed[:, sl])
    return jnp.concatenate(outs, axis=-1)


def _retention_kernel(lg_ref, q_ref, k_ref, v_ref, cos_ref, sin_ref, s0_ref, o_ref, sfin_ref, s_scr, *, rope):
    d = pl.program_id(1)
    t = pl.program_id(2)
    c = q_ref.shape[0]
    dk = C_HEAD_DIM

    @pl.when(t == 0)
    def _():
        s_scr[...] = s0_ref[0, 0]

    ti = lax.broadcasted_iota(jnp.int32, (c, c), 0)
    si = lax.broadcasted_iota(jnp.int32, (c, c), 1)
    diff = jnp.where(d == 0, ti - si, si - ti).astype(F32)
    jt = lax.broadcasted_iota(jnp.int32, (c, dk), 0)
    eq = jnp.where(d == 0, jt + 1, c - jt).astype(F32)
    ek = jnp.where(d == 0, c - 1 - jt, jt).astype(F32)
    for hd in range(C_HEADS):
        lg = lg_ref[d, hd]
        sl = slice(hd * dk, (hd + 1) * dk)
        q = q_ref[:, sl]
        k = k_ref[:, sl] * (dk ** -0.5)
        if rope:
            q = _rope_halves(q, cos_ref[...], sin_ref[...])
            k = _rope_halves(k, cos_ref[...], sin_ref[...])
        v = v_ref[:, sl].astype(BF16)
        rel = jnp.where(diff >= 0.0, jnp.exp(lg * jnp.maximum(diff, 0.0)), 0.0)
        att = lax.dot_general(q.astype(BF16), k.astype(BF16), (((1,), (1,)), ((), ())),
                              preferred_element_type=F32) * rel
        intra = jnp.dot(att.astype(BF16), v, preferred_element_type=F32)
        s_old = s_scr[hd]
        inter = jnp.dot((q * jnp.exp(lg * eq)).astype(BF16), s_old.astype(BF16), preferred_element_type=F32)
        o_ref[:, sl] = inter + intra
        kd = (k * jnp.exp(lg * ek)).astype(BF16)
        s_new = s_old * jnp.exp(lg * c) + lax.dot_general(kd, v, (((0,), (0,)), ((), ())),
                                                          preferred_element_type=F32)
        s_scr[hd] = s_new

    @pl.when(t == pl.num_programs(2) - 1)
    def _():
        sfin_ref[0, 0] = s_scr[...]


def retention_scan(proj, row0, n_seq, seq_len, log_gamma, s0, rope_tabs):
    d = D_MODEL
    c = SCAN_TILE
    nt = seq_len // c
    t0 = row0 // c
    rope = rope_tabs is not None
    if rope:
        cos, sin = rope_tabs
    else:
        cos = sin = jnp.zeros((c, C_HEAD_DIM), F32)
    tile = lambda dd, t: t + dd * (nt - 1 - 2 * t)
    tok = lambda col: pl.BlockSpec((c, d), lambda b, dd, t: (t0 + b * nt + tile(dd, t), col))
    rope_spec = pl.BlockSpec((c, C_HEAD_DIM), (lambda b, dd, t: (tile(dd, t), 0)) if rope else (lambda b, dd, t: (0, 0)))
    shared_s0 = s0.shape[0] == 1
    st_shape = (1, 1, C_HEADS, C_HEAD_DIM, C_HEAD_DIM)
    return pl.pallas_call(
        functools.partial(_retention_kernel, rope=rope),
        grid=(n_seq, 2, nt),
        in_specs=[pl.BlockSpec(memory_space=pltpu.SMEM), tok(0), tok(1), tok(2), rope_spec, rope_spec,
                  pl.BlockSpec(st_shape, lambda b, dd, t: (0 if shared_s0 else b, dd, 0, 0, 0))],
        out_specs=[pl.BlockSpec((None, c, d), lambda b, dd, t: (dd, b * nt + tile(dd, t), 0)),
                   pl.BlockSpec(st_shape, lambda b, dd, t: (b, dd, 0, 0, 0))],
        out_shape=[jax.ShapeDtypeStruct((2, n_seq * seq_len, d), F32),
                   jax.ShapeDtypeStruct((n_seq, 2, C_HEADS, C_HEAD_DIM, C_HEAD_DIM), F32)],
        scratch_shapes=[pltpu.VMEM((C_HEADS, C_HEAD_DIM, C_HEAD_DIM), F32)],
        compiler_params=_cparams(("parallel", "parallel", "arbitrary")),
        name="retention_scan",
    )(log_gamma, proj, proj, proj, cos, sin, s0)


def _split3(x):
    hi = x.astype(BF16)
    r1 = x - hi.astype(F32)
    mid = r1.astype(BF16)
    lo = (r1 - mid.astype(F32)).astype(BF16)
    return hi, mid, lo


def _hgrn_kernel(q_ref, v_ref, z_ref, lb_ref, s0_ref, o_ref, sfin_ref, st_scr, qe_scr, k_scr, b_scr, *, sub):
    d = pl.program_id(1)
    t = pl.program_id(2)
    tl, w = q_ref.shape
    dh = A_HEAD_DIM
    nsub = tl // sub

    @pl.when(t == 0)
    def _():
        for hd in range(A_HEADS):
            st_scr[hd] = s0_ref[0, 0, hd].T

    lb = lb_ref[0]
    z = z_ref[...]
    sg = jax.nn.sigmoid(z)
    lf = jnp.log(jnp.maximum(lb + (1.0 - lb) * sg, F_MIN))
    k_scr[...] = (1.0 - lb) * jax.nn.sigmoid(-z)
    ti = lax.broadcasted_iota(jnp.int32, (tl, tl), 0)
    si = lax.broadcasted_iota(jnp.int32, (tl, tl), 1)
    sign = jnp.where(d == 0, 1, -1)
    seen = ((ti - si) * sign >= 0) & ((ti // sub) == (si // sub))
    tri = jnp.where(seen, 1.0, 0.0).astype(BF16)
    b = sum(jnp.dot(tri, part, preferred_element_type=F32) for part in _split3(lf))
    b_scr[...] = b
    q = q_ref[...]
    q = q * jax.nn.sigmoid(q)
    qe_scr[...] = q * jnp.exp(b)

    ones = jnp.ones((dh, dh), BF16)
    rowi = lax.broadcasted_iota(jnp.int32, (sub, dh), 0)

    def step(i, carry):
        blk = jnp.where(d == 0, i, nsub - 1 - i)
        r0 = pl.multiple_of(blk * sub, sub)
        rows = pl.ds(r0, sub)
        for hd in range(A_HEADS):
            sl = slice(hd * dh, (hd + 1) * dh)
            qs = q_ref[rows, sl]
            qs = qs * jax.nn.sigmoid(qs)
            ks = k_scr[rows, sl]
            vs = v_ref[rows, sl]
            bs = b_scr[rows, sl]
            st = st_scr[hd]
            inter = lax.dot_general(qe_scr[rows, sl].astype(BF16), st.astype(BF16), (((1,), (1,)), ((), ())),
                                    preferred_element_type=F32)
            es = []
            for s in range(sub):
                vis = (rowi - s) * sign >= 0
                e = qs * ks[s:s + 1, :] * jnp.exp(jnp.minimum(bs - bs[s:s + 1, :], 0.0))
                es.append(jnp.where(vis, e, 0.0).astype(BF16))
            att = jnp.dot(jnp.concatenate(es, axis=0), ones, preferred_element_type=F32)
            intra = att[0:sub] * vs[0:1, :]
            for s in range(1, sub):
                intra = intra + att[s * sub:(s + 1) * sub] * vs[s:s + 1, :]
            o_ref[rows, sl] = inter + intra
            b_end = jnp.where(d == 0, bs[sub - 1:sub, :], bs[0:1, :])
            kd = (ks * jnp.exp(b_end - bs)).astype(BF16)
            st_scr[hd] = st * jnp.exp(b_end) + lax.dot_general(vs.astype(BF16), kd, (((0,), (0,)), ((), ())),
                                                               preferred_element_type=F32)
        return carry

    lax.fori_loop(0, nsub, step, 0)

    @pl.when(t == pl.num_programs(2) - 1)
    def _():
        for hd in range(A_HEADS):
            sfin_ref[0, 0, hd] = st_scr[hd].T


def hgrn_scan(proj, row0, n_seq, seq_len, lb, s0):
    w = A_WIDTH
    tl = SCAN_TILE
    nt = seq_len // tl
    t0 = row0 // tl
    tile = lambda dd, t: t + dd * (nt - 1 - 2 * t)
    tok = lambda colfn: pl.BlockSpec((tl, w), lambda b, dd, t: (t0 + b * nt + tile(dd, t), colfn(dd)))
    shared_s0 = s0.shape[0] == 1
    st_shape = (1, 1, A_HEADS, A_HEAD_DIM, A_HEAD_DIM)
    return pl.pallas_call(
        functools.partial(_hgrn_kernel, sub=HGRN_SUB),
        grid=(n_seq, 2, nt),
        in_specs=[tok(lambda dd: 0), tok(lambda dd: 1), tok(lambda dd: 2 + dd),
                  pl.BlockSpec((1, 1, w), lambda b, dd, t: (dd, 0, 0)),
                  pl.BlockSpec(st_shape, lambda b, dd, t: (0 if shared_s0 else b, dd, 0, 0, 0))],
        out_specs=[pl.BlockSpec((None, tl, w), lambda b, dd, t: (dd, b * nt + tile(dd, t), 0)),
                   pl.BlockSpec(st_shape, lambda b, dd, t: (b, dd, 0, 0, 0))],
        out_shape=[jax.ShapeDtypeStruct((2, n_seq * seq_len, w), F32),
                   jax.ShapeDtypeStruct((n_seq, 2, A_HEADS, A_HEAD_DIM, A_HEAD_DIM), F32)],
        scratch_shapes=[pltpu.VMEM((A_HEADS, A_HEAD_DIM, A_HEAD_DIM), F32),
                        pltpu.VMEM((tl, w), F32), pltpu.VMEM((tl, w), F32), pltpu.VMEM((tl, w), F32)],
        compiler_params=_cparams(("parallel", "parallel", "arbitrary")),
        name="hgrn_scan",
    )(proj, proj, proj, lb.reshape(2, 1, w), s0)


def _expert_kernel(x_ref, w1_ref, w3_ref, w2_ref, o_ref):
    x = x_ref[0]
    a = jnp.dot(x, w1_ref[0].astype(BF16), preferred_element_type=F32)
    g = jnp.dot(x, w3_ref[0].astype(BF16), preferred_element_type=F32)
    hid = (a * jax.nn.sigmoid(a) * g).astype(BF16)
    o_ref[0] = jnp.dot(hid, w2_ref[0].astype(BF16), preferred_element_type=F32)


def expert_ffn(xs, w1, w3, w2):
    e, m, d = xs.shape
    f = w1.shape[2]
    tm = ROW_TILE
    return pl.pallas_call(
        _expert_kernel,
        grid=(e, m // tm),
        in_specs=[pl.BlockSpec((1, tm, d), lambda ei, i: (ei, i, 0)),
                  pl.BlockSpec((1, d, f), lambda ei, i: (ei, 0, 0)),
                  pl.BlockSpec((1, d, f), lambda ei, i: (ei, 0, 0)),
                  pl.BlockSpec((1, f, d), lambda ei, i: (ei, 0, 0))],
        out_specs=pl.BlockSpec((1, tm, d), lambda ei, i: (ei, i, 0)),
        out_shape=jax.ShapeDtypeStruct((e, m, d), F32),
        compiler_params=_cparams(("parallel", "arbitrary")),
        name="expert_ffn",
    )(xs, w1, w3, w2)


FFT_MINOR = 128
FFT_LANES = FFT_MINOR * B_WIDTH
FFT_LANE_TILE = 4096


def _dot_split(a, b, passes=3):
    ah, bh = a.astype(BF16), b.astype(BF16)
    out = jnp.dot(ah, bh, preferred_element_type=F32)
    if passes >= 3:
        al = (a - ah.astype(F32)).astype(BF16)
        bl = (b - bh.astype(F32)).astype(BF16)
        out = out + jnp.dot(ah, bl, preferred_element_type=F32) + jnp.dot(al, bh, preferred_element_type=F32)
    return out


def _short_conv_kernel(x_ref, w_ref, b_ref, o_ref, *, ctx_tiles, ctx_seq, lat_seq):
    rows = x_ref.shape[0]
    x = x_ref[...]
    seq = jnp.where(pl.program_id(0) < ctx_tiles, ctx_seq, lat_seq)
    pos = lax.broadcasted_iota(jnp.int32, x.shape, 0) % seq
    prev = jnp.where(pos == 0, 0.0, pltpu.roll(x, 1, axis=0))
    nxt = jnp.where(pos == seq - 1, 0.0, pltpu.roll(x, rows - 1, axis=0))
    o_ref[...] = prev * w_ref[0:1, :] + x * w_ref[1:2, :] + nxt * w_ref[2:3, :] + b_ref[...]


def hyena_short_conv(proj, conv_w, conv_b, n_ctx_rows, ctx_seq, lat_seq):
    m = proj.shape[0]
    rb = lat_seq
    col0 = 5 * A_WIDTH // LANES
    per = B_WIDTH // LANES
    return pl.pallas_call(
        functools.partial(_short_conv_kernel, ctx_tiles=n_ctx_rows // rb, ctx_seq=ctx_seq, lat_seq=lat_seq),
        grid=(m // rb, 3 * per),
        in_specs=[pl.BlockSpec((rb, LANES), lambda i, j: (i, col0 + j)),
                  pl.BlockSpec((SHORT_CONV, LANES), lambda i, j: (0, j)),
                  pl.BlockSpec((1, LANES), lambda i, j: (0, j))],
        out_specs=pl.BlockSpec((None, rb, LANES), lambda i, j: (j // per, i, j % per)),
        out_shape=jax.ShapeDtypeStruct((3, m, B_WIDTH), F32),
        compiler_params=_cparams(("parallel", "parallel")),
        name="hyena_short_conv",
    )(proj, conv_w, conv_b.reshape(1, -1))


def _filter_kernel(z_ref, meta_ref, w1_ref, b1_ref, w2_ref, b2_ref, w3_ref, f_ref, dl_ref, o_ref):
    hdn = jnp.sin(f_ref[0:1, :] * (_dot_split(z_ref[...], w1_ref[...]) + b1_ref[...]))
    hdn = jnp.sin(f_ref[1:2, :] * (_dot_split(hdn, w2_ref[...]) + b2_ref[...]))
    filt = _dot_split(hdn, w3_ref[...])
    t = meta_ref[:, 0:1]
    fwd = meta_ref[:, 1:2]
    bwd = meta_ref[:, 2:3]
    win = jnp.exp(-t * dl_ref[...])
    for o in range(HYENA_ORDER):
        hf = filt[:, (2 * o) * B_WIDTH:(2 * o + 1) * B_WIDTH]
        hb = filt[:, (2 * o + 1) * B_WIDTH:(2 * o + 2) * B_WIDTH]
        o_ref[o] = (fwd * hf + bwd * hb) * win


def hyena_filter_taps(L, w1, b1, w2, b2, w3, freq):
    n = 2 * L
    t = np.linspace(0.0, 1.0, L, dtype=np.float32)
    bands = (FILTER_EMB - 1) // 2
    ang = (np.float32(2.0 * math.pi / L) * np.arange(L, dtype=np.float32)[:, None]
           * np.linspace(1e-4, bands - 1, bands, dtype=np.float32)[None, :]).astype(np.float32)
    z = np.concatenate([t[:, None], np.cos(ang), -np.sin(ang)], axis=-1).astype(np.float32)
    lag = np.concatenate([np.arange(L), [0], np.arange(L - 1, 0, -1)])
    meta = np.zeros((n, 8), np.float32)
    meta[:, 0] = t[lag]
    meta[:L, 1] = 1.0
    meta[L + 1:, 2] = 1.0
    deltas = np.abs(np.linspace(math.log(DECAY_TARGET) / SLOW_DECAY, math.log(DECAY_TARGET) / FAST_DECAY, B_WIDTH,
                                dtype=np.float32)).reshape(1, B_WIDTH)
    tr = min(n, 512)
    hid = w1.shape[1]
    fixed = lambda i: (0, 0)
    emb = LANES
    z = np.pad(z, ((0, 0), (0, emb - FILTER_EMB)))
    w1 = jnp.pad(w1, ((0, emb - FILTER_EMB), (0, 0)))
    return pl.pallas_call(
        _filter_kernel,
        grid=(n // tr,),
        in_specs=[pl.BlockSpec((tr, emb), lambda i: (i, 0)), pl.BlockSpec((tr, 8), lambda i: (i, 0)),
                  pl.BlockSpec((emb, hid), fixed), pl.BlockSpec((1, hid), fixed),
                  pl.BlockSpec((hid, hid), fixed), pl.BlockSpec((1, hid), fixed),
                  pl.BlockSpec((hid, HYENA_ORDER * 2 * B_WIDTH), fixed), pl.BlockSpec((2, hid), fixed),
                  pl.BlockSpec((1, B_WIDTH), fixed)],
        out_specs=pl.BlockSpec((HYENA_ORDER, tr, B_WIDTH), lambda i: (0, i, 0)),
        out_shape=jax.ShapeDtypeStruct((HYENA_ORDER, n, B_WIDTH), F32),
        compiler_params=_cparams(("parallel",)),
        name="hyena_filter_taps",
    )(jnp.asarray(z[lag]), jnp.asarray(meta), w1, b1.reshape(1, hid), w2, b2.reshape(1, hid), w3, freq,
      jnp.asarray(deltas))


class _FftPlan:
    def __init__(self, L):
        n = 2 * L
        n1, n2 = n // FFT_MINOR, FFT_MINOR
        h1, k_real = n1 // 2, n1 // 2 + 1
        group = max(1, 32 // h1)
        k1 = k_real
        while (group * 2 * k1) % 8:
            k1 += 1
        self.L, self.n1, self.h1, self.k1, self.group = L, n1, h1, k1, group
        live = (np.arange(k1) < k_real).astype(np.float64)
        th = 2.0 * np.pi / n1
        kk = np.arange(k1)[:, None]
        fwd = np.zeros((2 * k1, n1))
        fwd[0::2] = live[:, None] * np.cos(th * ((kk * np.arange(n1)[None, :]) % n1))
        fwd[1::2] = -live[:, None] * np.sin(th * ((kk * np.arange(n1)[None, :]) % n1))
        self.first_full = fwd.astype(np.float32)
        self.first_half = fwd[:, :h1].astype(np.float32)
        wgt = live * np.where((np.arange(k1) == 0) | (np.arange(k1) == n1 // 2), 1.0, 2.0) / n
        inv = np.zeros((h1, 2 * k1))
        ph = th * ((np.arange(h1)[:, None] * np.arange(k1)[None, :]) % n1)
        inv[:, 0::2] = wgt * np.cos(ph)
        inv[:, 1::2] = -wgt * np.sin(ph)
        self.last = inv.astype(np.float32)
        k = np.arange(k1)[:, None, None] + n1 * np.arange(n2)[None, :, None]
        ph = 2.0 * np.pi * ((k * np.arange(n2)[None, None, :]) % n) / n
        c, s = np.cos(ph), np.sin(ph)
        mf = np.concatenate([np.concatenate([c, s], axis=2), np.concatenate([-s, c], axis=2)], axis=1)
        mf = mf * live[:, None, None]
        self.mid_fwd = mf.astype(np.float32)
        self.mid_inv = np.ascontiguousarray(np.swapaxes(mf, 1, 2)).astype(np.float32)


def _lmul_kernel(f_ref, x_ref, o_ref):
    o_ref[...] = _dot_split(f_ref[...], x_ref[...])


def _lmul_gate_kernel(f_ref, x_ref, g_ref, u_ref, d_ref, o_ref):
    o_ref[...] = g_ref[...] * (_dot_split(f_ref[...], x_ref[...]) + d_ref[...] * u_ref[...])


def _left_multiply(fmat, x2d, x_row0, n_blocks, gate=None):
    r_out, r_in = fmat.shape
    nl = x2d.shape[1]
    tl = FFT_LANE_TILE
    xb0 = x_row0 // r_in
    specs = [pl.BlockSpec((r_out, r_in), lambda b, j: (0, 0)),
             pl.BlockSpec((r_in, tl), lambda b, j: (xb0 + b, j))]
    args = [jnp.asarray(fmat), x2d]
    body = _lmul_kernel
    if gate is not None:
        g2d, g_row0, u2d, u_row0, bias = gate
        specs += [pl.BlockSpec((r_out, tl), lambda b, j: (g_row0 // r_out + b, j)),
                  pl.BlockSpec((r_out, tl), lambda b, j: (u_row0 // r_out + b, j)),
                  pl.BlockSpec((1, tl), lambda b, j: (0, 0))]
        args += [g2d, u2d, jnp.tile(bias.reshape(1, -1), (1, tl // bias.shape[-1]))]
        body = _lmul_gate_kernel
    return pl.pallas_call(
        body,
        grid=(n_blocks, nl // tl),
        in_specs=specs,
        out_specs=pl.BlockSpec((r_out, tl), lambda b, j: (b, j)),
        out_shape=jax.ShapeDtypeStruct((n_blocks * r_out, nl), F32),
        compiler_params=_cparams(("parallel", "parallel")),
        name="fft_outer_stage",
    )(*args)


def _fft_mid_kernel(a_ref, mf_ref, mi_ref, k_ref, o_ref):
    half = FFT_MINOR
    kr = k_ref[0, 0:half, :]
    ki = k_ref[0, half:, :]
    for b in range(a_ref.shape[0]):
        x = _dot_split(mf_ref[0], a_ref[b, 0])
        xr, xi = x[0:half], x[half:]
        y = jnp.concatenate([xr * kr - xi * ki, xr * ki + xi * kr], axis=0)
        o_ref[b, 0] = _dot_split(mi_ref[0], y)


def _fft_spec_kernel(a_ref, mf_ref, o_ref):
    o_ref[0] = _dot_split(mf_ref[0], a_ref[0, 0])


def _fft_spectrum(plan, taps):
    a = _left_multiply(plan.first_full, taps.reshape(plan.n1, FFT_LANES), 0, 1)
    a = a.reshape(1, plan.k1, 2 * FFT_MINOR, B_WIDTH)
    blk = (2 * FFT_MINOR, B_WIDTH)
    return pl.pallas_call(
        _fft_spec_kernel,
        grid=(plan.k1,),
        in_specs=[pl.BlockSpec((1, 1) + blk, lambda k: (0, k, 0, 0)),
                  pl.BlockSpec((1, 2 * FFT_MINOR, 2 * FFT_MINOR), lambda k: (k, 0, 0))],
        out_specs=pl.BlockSpec((1,) + blk, lambda k: (k, 0, 0)),
        out_shape=jax.ShapeDtypeStruct((plan.k1,) + blk, F32),
        compiler_params=_cparams(("parallel",)),
        name="fft_tap_spectrum",
    )(a, jnp.asarray(plan.mid_fwd))


def _long_conv_gated(plan, n_seq, group, u2d, u_row0, g2d, g_row0, spec, bias):
    h1, k1 = plan.h1, plan.k1
    eye = np.eye(group, dtype=np.float32)
    a = _left_multiply(np.kron(eye, plan.first_half), u2d, u_row0, n_seq // group)
    a = a.reshape(n_seq, k1, 2 * FFT_MINOR, B_WIDTH)
    bb = min(n_seq, 8)
    blk = (2 * FFT_MINOR, B_WIDTH)
    mat = pl.BlockSpec((1, 2 * FFT_MINOR, 2 * FFT_MINOR), lambda k, b: (k, 0, 0))
    g = pl.pallas_call(
        _fft_mid_kernel,
        grid=(k1, n_seq // bb),
        in_specs=[pl.BlockSpec((bb, 1) + blk, lambda k, b: (b, k, 0, 0)), mat, mat,
                  pl.BlockSpec((1,) + blk, lambda k, b: (k, 0, 0))],
        out_specs=pl.BlockSpec((bb, 1) + blk, lambda k, b: (b, k, 0, 0)),
        out_shape=jax.ShapeDtypeStruct((n_seq, k1) + blk, F32),
        compiler_params=_cparams(("parallel", "parallel")),
        name="fft_inner_stage",
    )(a, jnp.asarray(plan.mid_fwd), jnp.asarray(plan.mid_inv), spec)
    g = g.reshape(n_seq // group * group * 2 * k1, FFT_LANES)
    return _left_multiply(np.kron(eye, plan.last), g, 0, n_seq // group, gate=(g2d, g_row0, u2d, u_row0, bias))


def hyena_mixer(proj, n_ctx_rows, ctx_seq, lat_seq, conv_w, conv_b, fw1, fb1, fw2, fb2, fw3, freq, fbias):
    m = proj.shape[0]
    sc = hyena_short_conv(proj, conv_w, conv_b, n_ctx_rows, ctx_seq, lat_seq)
    views = sc.reshape(3, m // FFT_MINOR, FFT_LANES)
    outs = []
    for L, row0, n_rows in ((ctx_seq, 0, n_ctx_rows), (lat_seq, n_ctx_rows, m - n_ctx_rows)):
        plan = _FftPlan(L)
        n_seq = n_rows // L
        group = plan.group
        taps = hyena_filter_taps(L, fw1, fb1, fw2, fb2, fw3, freq)
        z2d, z_row0 = views[0], row0 // FFT_MINOR
        for o in range(HYENA_ORDER):
            spec = _fft_spectrum(plan, taps[o])
            z2d = _long_conv_gated(plan, n_seq, group, z2d, z_row0, views[1 + o], row0 // FFT_MINOR, spec, fbias[o])
            z_row0 = 0
        outs.append(z2d.reshape(n_rows, B_WIDTH))
    return jnp.concatenate(outs, axis=0)


def _route_jax(aff_t, u2, n_seq, seq_len, row0):
    e = aff_t.shape[0]
    cap = CAP_FACTOR * seq_len // e
    a = aff_t[:, row0:row0 + n_seq * seq_len].reshape(e, n_seq, seq_len)
    gate, idx = lax.top_k(a, cap)
    flat = row0 + idx + (jnp.arange(n_seq) * seq_len)[None, :, None]
    return gate.reshape(e, n_seq * cap), flat.reshape(e, n_seq * cap)


def _rope_tables(seq_len):
    n_rows = seq_len // GRID_W
    rows = np.repeat(np.arange(n_rows), GRID_W).astype(np.float32)
    cols = np.tile(np.arange(GRID_W), n_rows).astype(np.float32)
    quarter = C_HEAD_DIM // 4
    inv = (ROPE_BASE ** (-np.arange(quarter, dtype=np.float32) / quarter)).astype(np.float32)
    cos_parts, sin_parts = [], []
    for pos in (rows, cols):
        a = jnp.asarray(pos)[:, None] * jnp.asarray(inv)[None, :]
        cos_parts += [jnp.cos(a), jnp.cos(a)]
        sin_parts += [-jnp.sin(a), jnp.sin(a)]
    return jnp.concatenate(cos_parts, axis=-1), jnp.concatenate(sin_parts, axis=-1)


def kernel(x_prompt, x_sample, state_hgrn, state_ret, c, c_ctx, ada_w, ada_b, ln_g, ln_b, even_w_in, even_w_out, hgrn_lb, hgrn_norm_w, hyena_conv_w, hyena_conv_b, hyena_w1, hyena_b1, hyena_w2, hyena_b2, hyena_w3, hyena_freq, hyena_bias, odd_w_in, odd_w_out, ret_decay, moe_router, moe_w1, moe_w3, moe_w2):
    nb, seq, d = x_prompt.shape
    nl, lseq, _ = x_sample.shape
    n_ctx = nb * seq
    n_lat = nl * lseq
    h = jnp.concatenate([x_prompt.reshape(n_ctx, d), x_sample.reshape(n_lat, d)], axis=0)

    cond = jnp.concatenate([c_ctx[None, :], c], axis=0)
    mods = modulation_table(cond, ada_w, ada_b).reshape(DEPTH, 1 + nl, 6, d)

    lb_soft = jax.nn.softmax(hgrn_lb.astype(F32), axis=0)
    lb_all = jnp.cumsum(lb_soft, axis=0) - lb_soft[0]
    rope_tabs = _rope_tables(lseq)
    zero_hgrn = jnp.zeros((1, 2, A_HEADS, A_HEAD_DIM, A_HEAD_DIM), F32)
    zero_ret = jnp.zeros((1, 2, C_HEADS, C_HEAD_DIM, C_HEAD_DIM), F32)

    hgrn_states, ret_states = [], []
    for l in range(DEPTH):
        j = l // 2
        mod = mods[l]
        if l % 2 == 0:
            proj = in_projection(h, mod, even_w_in[j], n_ctx, lseq)
            o_p, s_p = hgrn_scan(proj, 0, nb, seq, lb_all[j], zero_hgrn)
            o_l, _ = hgrn_scan(proj, n_ctx, nl, lseq, lb_all[j], state_hgrn[:, j])
            hgrn_states.append(s_p)
            scan_out = jnp.concatenate([o_p, o_l], axis=1)
            y_b = hyena_mixer(proj, n_ctx, seq, lseq, hyena_conv_w[j], hyena_conv_b[j], hyena_w1[j], hyena_b1[j],
                              hyena_w2[j], hyena_b2[j], hyena_w3[j], hyena_freq[j], hyena_bias[j])
            h, u2, aff_t = post_mixer(True, scan_out, proj, (hgrn_norm_w[j], y_b), h, mod, even_w_out[j],
                                      ln_g[l, 0], ln_b[l, 0], moe_router[l], n_ctx, lseq)
        else:
            proj = in_projection(h, mod, odd_w_in[j], n_ctx, lseq)
            log_gamma = jax.nn.log_sigmoid(ret_decay[j].astype(F32))
            o_p, s_p = retention_scan(proj, 0, nb, seq, log_gamma, zero_ret, None)
            o_l, _ = retention_scan(proj, n_ctx, nl, lseq, log_gamma, state_ret[:, j], rope_tabs)
            ret_states.append(s_p)
            scan_out = jnp.concatenate([o_p, o_l], axis=1)
            h, u2, aff_t = post_mixer(False, scan_out, proj, None, h, mod, odd_w_out[j],
                                      ln_g[l, 0], ln_b[l, 0], moe_router[l], n_ctx, lseq)
        g_p, i_p = _route_jax(aff_t, u2, nb, seq, 0)
        g_l, i_l = _route_jax(aff_t, u2, nl, lseq, n_ctx)
        gate = jnp.concatenate([g_p, g_l], axis=1)
        idx = jnp.concatenate([i_p, i_l], axis=1)
        xs = u2[idx]
        ys = expert_ffn(xs, moe_w1[l], moe_w3[l], moe_w2[l]) * gate[..., None]
        moe_out = jnp.zeros((n_ctx + n_lat, d), F32).at[idx.reshape(-1)].add(ys.reshape(-1, d))
        h = post_moe(moe_out, h, mod, ln_g[l, 1], ln_b[l, 1], n_ctx, lseq)

    y_prompt = h[:n_ctx].reshape(nb, seq, d)
    y_sample = h[n_ctx:].reshape(nl, lseq, d)
    new_state_hgrn = jnp.stack(hgrn_states, axis=1)
    new_state_ret = jnp.stack(ret_states, axis=1)
    return (y_prompt, y_sample, new_state_hgrn, new_state_ret)
```

```python
import functools
import math

import jax
import jax.numpy as jnp
import numpy as np
from jax import lax
from jax.experimental import pallas as pl
from jax.experimental.pallas import tpu as pltpu

F32 = jnp.float32
BF16 = jnp.bfloat16

D_MODEL = 1024
DEPTH = 4
GRID_W = 64
A_WIDTH = D_MODEL // 2
A_HEADS = 4
A_HEAD_DIM = A_WIDTH // A_HEADS
F_MIN = 1e-30
B_WIDTH = D_MODEL - A_WIDTH
HYENA_ORDER = 2
FILTER_EMB = 33
SHORT_CONV = 3
DECAY_TARGET = 1e-2
FAST_DECAY = 0.3
SLOW_DECAY = 1.5
C_HEADS = 4
C_HEAD_DIM = D_MODEL // C_HEADS
ROPE_BASE = 10000.0
N_EXPERTS = 16
CAP_FACTOR = 2
LN_EPS = 1e-5
DN_ALPHA = (2 * DEPTH) ** 0.25

LANES = 128
SUBLANES = 8
VMEM_LIMIT = 56 * 1024 * 1024
ROW_TILE = 512
SCAN_TILE = 256
HGRN_SUB = 16


def _cparams(sem):
    return pltpu.CompilerParams(dimension_semantics=sem, vmem_limit_bytes=VMEM_LIMIT)


def _cond_index(tile, tile_rows, n_ctx_rows, lat_seq):
    row = tile * tile_rows
    return jnp.where(row < n_ctx_rows, 0, 1 + (row - n_ctx_rows) // lat_seq)


def _mod_kernel(c_ref, w_ref, b_ref, o_ref):
    c = c_ref[...]
    s = c * jax.nn.sigmoid(c)
    o_ref[0] = jnp.dot(s.astype(BF16), w_ref[0].astype(BF16), preferred_element_type=F32) + b_ref[0]


def modulation_table(cond, ada_w, ada_b):
    r, d = cond.shape
    depth, _, n = ada_w.shape
    tn = 1024
    return pl.pallas_call(
        _mod_kernel,
        grid=(depth, n // tn),
        in_specs=[pl.BlockSpec((r, d), lambda l, j: (0, 0)),
                  pl.BlockSpec((1, d, tn), lambda l, j: (l, 0, j)),
                  pl.BlockSpec((1, 1, tn), lambda l, j: (l, 0, j))],
        out_specs=pl.BlockSpec((1, r, tn), lambda l, j: (l, 0, j)),
        out_shape=jax.ShapeDtypeStruct((depth, r, n), F32),
        compiler_params=_cparams(("parallel", "parallel")),
        name="modulation",
    )(cond, ada_w, ada_b.reshape(depth, 1, n))


def _inproj_kernel(x_ref, mod_ref, w_ref, o_ref):
    shift = mod_ref[0, 0:1, :]
    scale = mod_ref[0, 1:2, :]
    u = (x_ref[...] * (1.0 + scale) + shift).astype(BF16)
    o_ref[...] = jnp.dot(u, w_ref[...].astype(BF16), preferred_element_type=F32)


def in_projection(x, mod, w, n_ctx_rows, lat_seq):
    m, d = x.shape
    n = w.shape[1]
    tm, tn = ROW_TILE, 1024
    cidx = functools.partial(_cond_index, tile_rows=tm, n_ctx_rows=n_ctx_rows, lat_seq=lat_seq)
    return pl.pallas_call(
        _inproj_kernel,
        grid=(n // tn, m // tm),
        in_specs=[pl.BlockSpec((tm, d), lambda j, i: (i, 0)),
                  pl.BlockSpec((1, 6, d), lambda j, i: (cidx(i), 0, 0)),
                  pl.BlockSpec((d, tn), lambda j, i: (0, j))],
        out_specs=pl.BlockSpec((tm, tn), lambda j, i: (i, j)),
        out_shape=jax.ShapeDtypeStruct((m, n), F32),
        compiler_params=_cparams(("parallel", "parallel")),
        name="in_projection",
    )(x, mod, w)


def _layer_norm_rows(z, g, b):
    mu = jnp.mean(z, axis=-1, keepdims=True)
    zc = z - mu
    var = jnp.mean(zc * zc, axis=-1, keepdims=True)
    return zc * lax.rsqrt(var + LN_EPS) * g + b


def _post_mixer(y, h_ref, mod_ref, w_ref, lng_ref, lnb_ref, r_ref, h_out, u_out, aff_out):
    gate1 = mod_ref[0, 2:3, :]
    shift2 = mod_ref[0, 3:4, :]
    scale2 = mod_ref[0, 4:5, :]
    mix = jnp.dot(y.astype(BF16), w_ref[...].astype(BF16), preferred_element_type=F32)
    hn = _layer_norm_rows(DN_ALPHA * h_ref[...] + gate1 * mix, lng_ref[...], lnb_ref[...])
    h_out[...] = hn
    u2 = (hn * (1.0 + scale2) + shift2).astype(BF16)
    u_out[...] = u2
    logits = lax.dot_general(r_ref[...].astype(BF16), u2, (((1,), (1,)), ((), ())), preferred_element_type=F32)
    mx = jnp.max(logits, axis=0, keepdims=True)
    ex = jnp.exp(logits - mx)
    aff_out[...] = ex / jnp.sum(ex, axis=0, keepdims=True)


def _post_even_kernel(of_ref, ob_ref, g_ref, nw_ref, yb_ref, h_ref, mod_ref, w_ref, lng_ref, lnb_ref, r_ref,
                      h_out, u_out, aff_out):
    o = of_ref[...] + ob_ref[...]
    g = g_ref[...]
    parts = []
    for hd in range(A_HEADS):
        sl = slice(hd * A_HEAD_DIM, (hd + 1) * A_HEAD_DIM)
        oh = o[:, sl]
        parts.append(oh * lax.rsqrt(jnp.mean(oh * oh, axis=-1, keepdims=True) + LN_EPS))
    ya = jnp.concatenate(parts, axis=-1) * nw_ref[...] * (g * jax.nn.sigmoid(g))
    y = jnp.concatenate([ya, yb_ref[...]], axis=-1)
    _post_mixer(y, h_ref, mod_ref, w_ref, lng_ref, lnb_ref, r_ref, h_out, u_out, aff_out)


def _post_odd_kernel(of_ref, ob_ref, g_ref, h_ref, mod_ref, w_ref, lng_ref, lnb_ref, r_ref,
                     h_out, u_out, aff_out):
    o = of_ref[...] + ob_ref[...]
    g = g_ref[...]
    parts = []
    for hd in range(C_HEADS):
        sl = slice(hd * C_HEAD_DIM, (hd + 1) * C_HEAD_DIM)
        oh = o[:, sl]
        mu = jnp.mean(oh, axis=-1, keepdims=True)
        oc = oh - mu
        parts.append(oc * lax.rsqrt(jnp.mean(oc * oc, axis=-1, keepdims=True) + LN_EPS))
    y = jnp.concatenate(parts, axis=-1) * (g * jax.nn.sigmoid(g))
    _post_mixer(y, h_ref, mod_ref, w_ref, lng_ref, lnb_ref, r_ref, h_out, u_out, aff_out)


def post_mixer(even, scan_out, proj, extra, h, mod, w_out, ln_g, ln_b, router, n_ctx_rows, lat_seq):
    m, d = h.shape
    e = router.shape[1]
    tm = ROW_TILE
    cidx = functools.partial(_cond_index, tile_rows=tm, n_ctx_rows=n_ctx_rows, lat_seq=lat_seq)
    width = scan_out.shape[-1]
    row = lambda i: (i, 0)
    fixed = lambda i: (0, 0)
    common_specs = [pl.BlockSpec((tm, d), row),
                    pl.BlockSpec((1, 6, d), lambda i: (cidx(i), 0, 0)),
                    pl.BlockSpec((d, d), fixed),
                    pl.BlockSpec((1, d), fixed),
                    pl.BlockSpec((1, d), fixed),
                    pl.BlockSpec((e, d), fixed)]
    common_args = [h, mod, w_out, ln_g.reshape(1, d), ln_b.reshape(1, d), router.T]
    scan_specs = [pl.BlockSpec((None, tm, width), lambda i: (0, i, 0)),
                  pl.BlockSpec((None, tm, width), lambda i: (1, i, 0))]
    if even:
        norm_w, y_b = extra
        gate_col = 4 * A_WIDTH // width
        specs = scan_specs + [pl.BlockSpec((tm, width), lambda i: (i, gate_col)),
                              pl.BlockSpec((1, width), fixed),
                              pl.BlockSpec((tm, B_WIDTH), row)] + common_specs
        args = [scan_out, scan_out, proj, norm_w.reshape(1, width), y_b] + common_args
        body = _post_even_kernel
    else:
        gate_col = 3
        specs = scan_specs + [pl.BlockSpec((tm, width), lambda i: (i, gate_col))] + common_specs
        args = [scan_out, scan_out, proj] + common_args
        body = _post_odd_kernel
    return pl.pallas_call(
        body,
        grid=(m // tm,),
        in_specs=specs,
        out_specs=[pl.BlockSpec((tm, d), row), pl.BlockSpec((tm, d), row), pl.BlockSpec((e, tm), lambda i: (0, i))],
        out_shape=[jax.ShapeDtypeStruct((m, d), F32), jax.ShapeDtypeStruct((m, d), BF16),
                   jax.ShapeDtypeStruct((e, m), F32)],
        compiler_params=_cparams(("parallel",)),
        name="post_mixer_even" if even else "post_mixer_odd",
    )(*args)


def _rope_halves(x, cos, sin_signed):
    outs = []
    for p in range(2):
        sl = slice(p * LANES, (p + 1) * LANES)
        xp = x[:, sl]
        outs.append(xp * cos[:, sl] + pltpu.roll(xp, LANES // 2, axis=1) * sin_signed[:, sl])
    return jnp.concatenate(outs, axis=-1)


def _retention_kernel(lg_ref, q_ref, k_ref, v_ref, cos_ref, sin_ref, s0_ref, o_ref, sfin_ref, s_scr, *, rope):
    d = pl.program_id(1)
    t = pl.program_id(2)
    c = q_ref.shape[0]
    dk = C_HEAD_DIM

    @pl.when(t == 0)
    def _():
        s_scr[...] = s0_ref[0, 0]

    ti = lax.broadcasted_iota(jnp.int32, (c, c), 0)
    si = lax.broadcasted_iota(jnp.int32, (c, c), 1)
    diff = jnp.where(d == 0, ti - si, si - ti).astype(F32)
    jt = lax.broadcasted_iota(jnp.int32, (c, dk), 0)
    eq = jnp.where(d == 0, jt + 1, c - jt).astype(F32)
    ek = jnp.where(d == 0, c - 1 - jt, jt).astype(F32)
    for hd in range(C_HEADS):
        lg = lg_ref[d, hd]
        sl = slice(hd * dk, (hd + 1) * dk)
        q = q_ref[:, sl]
        k = k_ref[:, sl] * (dk ** -0.5)
        if rope:
            q = _rope_halves(q, cos_ref[...], sin_ref[...])
            k = _rope_halves(k, cos_ref[...], sin_ref[...])
        v = v_ref[:, sl].astype(BF16)
        rel = jnp.where(diff >= 0.0, jnp.exp(lg * jnp.maximum(diff, 0.0)), 0.0)
        att = lax.dot_general(q.astype(BF16), k.astype(BF16), (((1,), (1,)), ((), ())),
                              preferred_element_type=F32) * rel
        intra = jnp.dot(att.astype(BF16), v, preferred_element_type=F32)
        s_old = s_scr[hd]
        inter = jnp.dot((q * jnp.exp(lg * eq)).astype(BF16), s_old.astype(BF16), preferred_element_type=F32)
        o_ref[:, sl] = inter + intra
        kd = (k * jnp.exp(lg * ek)).astype(BF16)
        s_new = s_old * jnp.exp(lg * c) + lax.dot_general(kd, v, (((0,), (0,)), ((), ())),
                                                          preferred_element_type=F32)
        s_scr[hd] = s_new

    @pl.when(t == pl.num_programs(2) - 1)
    def _():
        sfin_ref[0, 0] = s_scr[...]


def retention_scan(proj, row0, n_seq, seq_len, log_gamma, s0, rope_tabs):
    d = D_MODEL
    c = SCAN_TILE
    nt = seq_len // c
    t0 = row0 // c
    rope = rope_tabs is not None
    if rope:
        cos, sin = rope_tabs
    else:
        cos = sin = jnp.zeros((c, C_HEAD_DIM), F32)
    tile = lambda dd, t: t + dd * (nt - 1 - 2 * t)
    tok = lambda col: pl.BlockSpec((c, d), lambda b, dd, t: (t0 + b * nt + tile(dd, t), col))
    rope_spec = pl.BlockSpec((c, C_HEAD_DIM), (lambda b, dd, t: (tile(dd, t), 0)) if rope else (lambda b, dd, t: (0, 0)))
    shared_s0 = s0.shape[0] == 1
    st_shape = (1, 1, C_HEADS, C_HEAD_DIM, C_HEAD_DIM)
    return pl.pallas_call(
        functools.partial(_retention_kernel, rope=rope),
        grid=(n_seq, 2, nt),
        in_specs=[pl.BlockSpec(memory_space=pltpu.SMEM), tok(0), tok(1), tok(2), rope_spec, rope_spec,
                  pl.BlockSpec(st_shape, lambda b, dd, t: (0 if shared_s0 else b, dd, 0, 0, 0))],
        out_specs=[pl.BlockSpec((None, c, d), lambda b, dd, t: (dd, b * nt + tile(dd, t), 0)),
                   pl.BlockSpec(st_shape, lambda b, dd, t: (b, dd, 0, 0, 0))],
        out_shape=[jax.ShapeDtypeStruct((2, n_seq * seq_len, d), F32),
                   jax.ShapeDtypeStruct((n_seq, 2, C_HEADS, C_HEAD_DIM, C_HEAD_DIM), F32)],
        scratch_shapes=[pltpu.VMEM((C_HEADS, C_HEAD_DIM, C_HEAD_DIM), F32)],
        compiler_params=_cparams(("parallel", "parallel", "arbitrary")),
        name="retention_scan",
    )(log_gamma, proj, proj, proj, cos, sin, s0)


def _split3(x):
    hi = x.astype(BF16)
    r1 = x - hi.astype(F32)
    mid = r1.astype(BF16)
    lo = (r1 - mid.astype(F32)).astype(BF16)
    return hi, mid, lo


def _hgrn_kernel(q_ref, v_ref, z_ref, lb_ref, s0_ref, o_ref, sfin_ref, st_scr, qe_scr, k_scr, b_scr, *, sub):
    d = pl.program_id(1)
    t = pl.program_id(2)
    tl, w = q_ref.shape
    dh = A_HEAD_DIM
    nsub = tl // sub

    @pl.when(t == 0)
    def _():
        for hd in range(A_HEADS):
            st_scr[hd] = s0_ref[0, 0, hd].T

    lb = lb_ref[0]
    z = z_ref[...]
    sg = jax.nn.sigmoid(z)
    lf = jnp.log(jnp.maximum(lb + (1.0 - lb) * sg, F_MIN))
    k_scr[...] = (1.0 - lb) * jax.nn.sigmoid(-z)
    ti = lax.broadcasted_iota(jnp.int32, (tl, tl), 0)
    si = lax.broadcasted_iota(jnp.int32, (tl, tl), 1)
    sign = jnp.where(d == 0, 1, -1)
    seen = ((ti - si) * sign >= 0) & ((ti // sub) == (si // sub))
    tri = jnp.where(seen, 1.0, 0.0).astype(BF16)
    b = sum(jnp.dot(tri, part, preferred_element_type=F32) for part in _split3(lf))
    b_scr[...] = b
    q = q_ref[...]
    q = q * jax.nn.sigmoid(q)
    qe_scr[...] = q * jnp.exp(b)

    ones = jnp.ones((dh, dh), BF16)
    rowi = lax.broadcasted_iota(jnp.int32, (sub, dh), 0)

    def step(i, carry):
        blk = jnp.where(d == 0, i, nsub - 1 - i)
        r0 = pl.multiple_of(blk * sub, sub)
        rows = pl.ds(r0, sub)
        for hd in range(A_HEADS):
            sl = slice(hd * dh, (hd + 1) * dh)
            qs = q_ref[rows, sl]
            qs = qs * jax.nn.sigmoid(qs)
            ks = k_scr[rows, sl]
            vs = v_ref[rows, sl]
            bs = b_scr[rows, sl]
            st = st_scr[hd]
            inter = lax.dot_general(qe_scr[rows, sl].astype(BF16), st.astype(BF16), (((1,), (1,)), ((), ())),
                                    preferred_element_type=F32)
            es = []
            for s in range(sub):
                vis = (rowi - s) * sign >= 0
                e = qs * ks[s:s + 1, :] * jnp.exp(jnp.minimum(bs - bs[s:s + 1, :], 0.0))
                es.append(jnp.where(vis, e, 0.0).astype(BF16))
            att = jnp.dot(jnp.concatenate(es, axis=0), ones, preferred_element_type=F32)
            intra = att[0:sub] * vs[0:1, :]
            for s in range(1, sub):
                intra = intra + att[s * sub:(s + 1) * sub] * vs[s:s + 1, :]
            o_ref[rows, sl] = inter + intra
            b_end = jnp.where(d == 0, bs[sub - 1:sub, :], bs[0:1, :])
            kd = (ks * jnp.exp(b_end - bs)).astype(BF16)
            st_scr[hd] = st * jnp.exp(b_end) + lax.dot_general(vs.astype(BF16), kd, (((0,), (0,)), ((), ())),
                                                               preferred_element_type=F32)
        return carry

    lax.fori_loop(0, nsub, step, 0)

    @pl.when(t == pl.num_programs(2) - 1)
    def _():
        for hd in range(A_HEADS):
            sfin_ref[0, 0, hd] = st_scr[hd].T


def hgrn_scan(proj, row0, n_seq, seq_len, lb, s0):
    w = A_WIDTH
    tl = SCAN_TILE
    nt = seq_len // tl
    t0 = row0 // tl
    tile = lambda dd, t: t + dd * (nt - 1 - 2 * t)
    tok = lambda colfn: pl.BlockSpec((tl, w), lambda b, dd, t: (t0 + b * nt + tile(dd, t), colfn(dd)))
    shared_s0 = s0.shape[0] == 1
    st_shape = (1, 1, A_HEADS, A_HEAD_DIM, A_HEAD_DIM)
    return pl.pallas_call(
        functools.partial(_hgrn_kernel, sub=HGRN_SUB),
        grid=(n_seq, 2, nt),
        in_specs=[tok(lambda dd: 0), tok(lambda dd: 1), tok(lambda dd: 2 + dd),
                  pl.BlockSpec((1, 1, w), lambda b, dd, t: (dd, 0, 0)),
                  pl.BlockSpec(st_shape, lambda b, dd, t: (0 if shared_s0 else b, dd, 0, 0, 0))],
        out_specs=[pl.BlockSpec((None, tl, w), lambda b, dd, t: (dd, b * nt + tile(dd, t), 0)),
                   pl.BlockSpec(st_shape, lambda b, dd, t: (b, dd, 0, 0, 0))],
        out_shape=[jax.ShapeDtypeStruct((2, n_seq * seq_len, w), F32),
                   jax.ShapeDtypeStruct((n_seq, 2, A_HEADS, A_HEAD_DIM, A_HEAD_DIM), F32)],
        scratch_shapes=[pltpu.VMEM((A_HEADS, A_HEAD_DIM, A_HEAD_DIM), F32),
                        pltpu.VMEM((tl, w), F32), pltpu.VMEM((tl, w), F32), pltpu.VMEM((tl, w), F32)],
        compiler_params=_cparams(("parallel", "parallel", "arbitrary")),
        name="hgrn_scan",
    )(proj, proj, proj, lb.reshape(2, 1, w), s0)


def _exclusive_count(x):
    r, n = x.shape
    ji = lax.broadcasted_iota(jnp.int32, (LANES, LANES), 0)
    ii = lax.broadcasted_iota(jnp.int32, (LANES, LANES), 1)
    upper = jnp.where(ji < ii, 1.0, 0.0).astype(BF16)
    carry = jnp.zeros((r, 1), F32)
    outs = []
    for blk in range(n // LANES):
        xb = x[:, blk * LANES:(blk + 1) * LANES]
        outs.append(jnp.dot(xb.astype(BF16), upper, preferred_element_type=F32) + carry)
        carry = carry + jnp.sum(xb, axis=1, keepdims=True)
    return jnp.concatenate(outs, axis=1)


def _select_kernel(a_ref, pos_ref, *, cap):
    a = a_ref[...]

    def count(mask):
        return jnp.sum(jnp.where(mask, 1.0, 0.0), axis=1, keepdims=True)

    def body(i, t):
        cand = t | lax.shift_left(jnp.int32(1), 30 - i)
        return jnp.where(count(a >= pltpu.bitcast(cand, F32)) >= cap, cand, t)

    t = lax.fori_loop(0, 31, body, jnp.zeros((a.shape[0], 1), jnp.int32))
    v = jnp.min(jnp.where(a >= pltpu.bitcast(t, F32), a, jnp.inf), axis=1, keepdims=True)

    def too_low(v):
        return jnp.max(count(a > v)) >= cap

    def step_up(v):
        nxt = jnp.min(jnp.where(a > v, a, jnp.inf), axis=1, keepdims=True)
        return jnp.where(count(a > v) >= cap, nxt, v)

    v = lax.while_loop(too_low, step_up, v)
    gt = jnp.where(a > v, 1.0, 0.0)
    eq = jnp.where(a == v, 1.0, 0.0)
    need = cap - jnp.sum(gt, axis=1, keepdims=True)
    sel = gt + eq * jnp.where(_exclusive_count(eq) < need, 1.0, 0.0)
    pos = _exclusive_count(sel)
    pos_ref[...] = jnp.where(sel > 0.0, pos, -1.0).astype(jnp.int32)


def route_select(aff_rows, cap):
    r, n = aff_rows.shape
    rb = min(r, LANES)
    return pl.pallas_call(
        functools.partial(_select_kernel, cap=cap),
        grid=(r // rb,),
        in_specs=[pl.BlockSpec((rb, n), lambda i: (i, 0))],
        out_specs=pl.BlockSpec((rb, n), lambda i: (i, 0)),
        out_shape=jax.ShapeDtypeStruct((r, n), jnp.int32),
        compiler_params=_cparams(("parallel",)),
        name="route_select",
    )(aff_rows)


def _gather_ctx_kernel(pos_ref, u_ref, o_ref):
    e, n = pos_ref.shape
    cap = o_ref.shape[1]
    slot = lax.broadcasted_iota(jnp.int32, (cap, n), 0)
    onehot = jnp.concatenate([jnp.where(pos_ref[ei:ei + 1, :] == slot, 1.0, 0.0).astype(BF16) for ei in range(e)],
                             axis=0)
    x = jnp.dot(onehot, u_ref[...], preferred_element_type=F32)
    for ei in range(e):
        o_ref[ei] = x[ei * cap:(ei + 1) * cap].astype(BF16)


def _gather_lat_kernel(pos_ref, u_ref, o_ref):
    cap = o_ref.shape[1]
    n = u_ref.shape[0]
    chunk = min(n, 2 * ROW_TILE)
    slot = lax.broadcasted_iota(jnp.int32, (cap, chunk), 0)
    acc = jnp.zeros(o_ref.shape[1:], F32)
    for c0 in range(0, n, chunk):
        onehot = jnp.where(pos_ref[0, :, c0:c0 + chunk] == slot, 1.0, 0.0).astype(BF16)
        acc = acc + jnp.dot(onehot, u_ref[c0:c0 + chunk, :], preferred_element_type=F32)
    o_ref[0] = acc.astype(BF16)


def route_gather(pos_ctx, pos_lat, u2, nb, seq, nl, lseq, n_exp):
    d = u2.shape[1]
    cap_c = CAP_FACTOR * seq // n_exp
    cap_l = CAP_FACTOR * lseq // n_exp
    xs_c = pl.pallas_call(
        _gather_ctx_kernel,
        grid=(nb,),
        in_specs=[pl.BlockSpec((n_exp, seq), lambda b: (b, 0)), pl.BlockSpec((seq, d), lambda b: (b, 0))],
        out_specs=pl.BlockSpec((n_exp, cap_c, d), lambda b: (0, b, 0)),
        out_shape=jax.ShapeDtypeStruct((n_exp, nb * cap_c, d), BF16),
        compiler_params=_cparams(("parallel",)),
        name="route_gather_ctx",
    )(pos_ctx, u2)
    lat_blk0 = nb * seq // lseq
    xs_l = pl.pallas_call(
        _gather_lat_kernel,
        grid=(nl, n_exp),
        in_specs=[pl.BlockSpec((1, 1, lseq), lambda b, ei: (b * n_exp + ei, 0, 0)),
                  pl.BlockSpec((lseq, d), lambda b, ei: (lat_blk0 + b, 0))],
        out_specs=pl.BlockSpec((1, cap_l, d), lambda b, ei: (ei, b, 0)),
        out_shape=jax.ShapeDtypeStruct((n_exp, nl * cap_l, d), BF16),
        compiler_params=_cparams(("parallel", "arbitrary")),
        name="route_gather_lat",
    )(pos_lat.reshape(nl * n_exp, 1, lseq), u2)
    return xs_c, xs_l


def _combine_kernel(pos_ref, gate_ref, yc_ref, yl_ref, h_ref, mod_ref, lng_ref, lnb_ref, o_ref, *, ctx_tiles):
    tn, e = pos_ref.shape
    d = h_ref.shape[1]

    def finish(moe):
        gate2 = mod_ref[0, 5:6, :]
        o_ref[...] = _layer_norm_rows(DN_ALPHA * h_ref[...] + gate2 * moe, lng_ref[...], lnb_ref[...])

    @pl.when(pl.program_id(0) < ctx_tiles)
    def _():
        cap = yc_ref.shape[1]
        lane = lax.broadcasted_iota(jnp.int32, (e, e * cap), 1)
        owner = lax.broadcasted_iota(jnp.int32, (e, e * cap), 0)
        expand = jnp.where(lane // cap == owner, 1.0, 0.0).astype(BF16)
        posx = jnp.dot(pos_ref[...].astype(F32).astype(BF16), expand, preferred_element_type=F32)
        gatex = sum(jnp.dot(part, expand, preferred_element_type=F32) for part in _split3(gate_ref[...]))
        slot = (lax.broadcasted_iota(jnp.int32, (tn, e * cap), 1) % cap).astype(F32)
        w = jnp.where(posx == slot, gatex, 0.0)
        w_hi = w.astype(BF16)
        w_lo = (w - w_hi.astype(F32)).astype(BF16)
        y = yc_ref[...].reshape(e * cap, d)
        finish(jnp.dot(w_hi, y, preferred_element_type=F32) + jnp.dot(w_lo, y, preferred_element_type=F32))

    @pl.when(pl.program_id(0) >= ctx_tiles)
    def _():
        cap = yl_ref.shape[1]
        slot = lax.broadcasted_iota(jnp.int32, (tn, cap), 1)
        acc = jnp.zeros((tn, d), F32)
        for ei in range(e):
            onehot = jnp.where(pos_ref[:, ei:ei + 1] == slot, 1.0, 0.0).astype(BF16)
            acc = acc + gate_ref[:, ei:ei + 1] * jnp.dot(onehot, yl_ref[ei], preferred_element_type=F32)
        finish(acc)


def route_combine(pos_t, gate_t, ys, h, mod, ln_g, ln_b, nb, seq, nl, lseq):
    m, d = h.shape
    e = pos_t.shape[1]
    tn = seq
    cap_c = CAP_FACTOR * seq // e
    cap_l = CAP_FACTOR * lseq // e
    lat_blk0 = nb * cap_c // cap_l
    per_seq = lseq // tn
    cidx = functools.partial(_cond_index, tile_rows=tn, n_ctx_rows=nb * seq, lat_seq=lseq)
    row = lambda i: (i, 0)
    fixed = lambda i: (0, 0)
    return pl.pallas_call(
        functools.partial(_combine_kernel, ctx_tiles=nb),
        grid=(m // tn,),
        in_specs=[pl.BlockSpec((tn, e), row), pl.BlockSpec((tn, e), row),
                  pl.BlockSpec((e, cap_c, d), lambda i: (0, jnp.minimum(i, nb - 1), 0)),
                  pl.BlockSpec((e, cap_l, d), lambda i: (0, lat_blk0 + jnp.maximum(i - nb, 0) // per_seq, 0)),
                  pl.BlockSpec((tn, d), row),
                  pl.BlockSpec((1, 6, d), lambda i: (cidx(i), 0, 0)),
                  pl.BlockSpec((1, d), fixed), pl.BlockSpec((1, d), fixed)],
        out_specs=pl.BlockSpec((tn, d), row),
        out_shape=jax.ShapeDtypeStruct((m, d), F32),
        compiler_params=_cparams(("arbitrary",)),
        name="route_combine",
    )(pos_t, gate_t, ys, ys, h, mod, ln_g.reshape(1, d), ln_b.reshape(1, d))


def _expert_kernel(xc_ref, xl_ref, w1_ref, w3_ref, w2_ref, o_ref, *, ctx_tiles):
    x = jnp.where(pl.program_id(1) < ctx_tiles, xc_ref[0], xl_ref[0])
    a = jnp.dot(x, w1_ref[0].astype(BF16), preferred_element_type=F32)
    g = jnp.dot(x, w3_ref[0].astype(BF16), preferred_element_type=F32)
    hid = (a * jax.nn.sigmoid(a) * g).astype(BF16)
    o_ref[0] = jnp.dot(hid, w2_ref[0].astype(BF16), preferred_element_type=F32).astype(BF16)


def expert_ffn(xs_c, xs_l, w1, w3, w2):
    e, mc, d = xs_c.shape
    ml = xs_l.shape[1]
    f = w1.shape[2]
    tm = ROW_TILE
    ct, lt = mc // tm, ml // tm
    return pl.pallas_call(
        functools.partial(_expert_kernel, ctx_tiles=ct),
        grid=(e, ct + lt),
        in_specs=[pl.BlockSpec((1, tm, d), lambda ei, i: (ei, jnp.minimum(i, ct - 1), 0)),
                  pl.BlockSpec((1, tm, d), lambda ei, i: (ei, jnp.maximum(i - ct, 0), 0)),
                  pl.BlockSpec((1, d, f), lambda ei, i: (ei, 0, 0)),
                  pl.BlockSpec((1, d, f), lambda ei, i: (ei, 0, 0)),
                  pl.BlockSpec((1, f, d), lambda ei, i: (ei, 0, 0))],
        out_specs=pl.BlockSpec((1, tm, d), lambda ei, i: (ei, i, 0)),
        out_shape=jax.ShapeDtypeStruct((e, mc + ml, d), BF16),
        compiler_params=_cparams(("parallel", "arbitrary")),
        name="expert_ffn",
    )(xs_c, xs_l, w1, w3, w2)


def moe_layer(aff_t, u2, h, mod, w1, w3, w2, ln_g, ln_b, nb, seq, nl, lseq):
    e = aff_t.shape[0]
    n_ctx = nb * seq
    seq_rows = lambda a, n, length: a.reshape(e, n, length).transpose(1, 0, 2).reshape(n * e, length)
    pos_c = route_select(seq_rows(aff_t[:, :n_ctx], nb, seq), CAP_FACTOR * seq // e)
    pos_l = route_select(seq_rows(aff_t[:, n_ctx:], nl, lseq), CAP_FACTOR * lseq // e)
    xs_c, xs_l = route_gather(pos_c, pos_l, u2, nb, seq, nl, lseq, e)
    ys = expert_ffn(xs_c, xs_l, w1, w3, w2)
    tok_rows = lambda p, n, length: p.reshape(n, e, length).transpose(0, 2, 1).reshape(n * length, e)
    pos_t = jnp.concatenate([tok_rows(pos_c, nb, seq), tok_rows(pos_l, nl, lseq)], axis=0)
    return route_combine(pos_t, aff_t.T, ys, h, mod, ln_g, ln_b, nb, seq, nl, lseq)


FFT_MINOR = 128
FFT_LANES = FFT_MINOR * B_WIDTH
FFT_LANE_TILE = 4096


def _dot_split(a, b, passes=3):
    ah, bh = a.astype(BF16), b.astype(BF16)
    out = jnp.dot(ah, bh, preferred_element_type=F32)
    if passes >= 3:
        al = (a - ah.astype(F32)).astype(BF16)
        bl = (b - bh.astype(F32)).astype(BF16)
        out = out + jnp.dot(ah, bl, preferred_element_type=F32) + jnp.dot(al, bh, preferred_element_type=F32)
    return out


def _short_conv_kernel(x_ref, before_ref, after_ref, w_ref, b_ref, o_ref, *, ctx_tiles, ctx_seq, lat_seq):
    rows = x_ref.shape[0]
    x = x_ref[...]
    seq = jnp.where(pl.program_id(0) < ctx_tiles, ctx_seq, lat_seq)
    row = lax.broadcasted_iota(jnp.int32, x.shape, 0)
    pos = (pl.program_id(0) * rows + row) % seq
    prev = jnp.where(row == 0, before_ref[SUBLANES - 1:SUBLANES, :], pltpu.roll(x, 1, axis=0))
    nxt = jnp.where(row == rows - 1, after_ref[0:1, :], pltpu.roll(x, rows - 1, axis=0))
    prev = jnp.where(pos == 0, 0.0, prev)
    nxt = jnp.where(pos == seq - 1, 0.0, nxt)
    o_ref[...] = prev * w_ref[0:1, :] + x * w_ref[1:2, :] + nxt * w_ref[2:3, :] + b_ref[...]


def hyena_short_conv(proj, conv_w, conv_b, n_ctx_rows, ctx_seq, lat_seq):
    m = proj.shape[0]
    rb = 2 * ROW_TILE
    col0 = 5 * A_WIDTH // B_WIDTH
    halo = rb // SUBLANES
    last = m // SUBLANES - 1
    return pl.pallas_call(
        functools.partial(_short_conv_kernel, ctx_tiles=n_ctx_rows // rb, ctx_seq=ctx_seq, lat_seq=lat_seq),
        grid=(m // rb, 3),
        in_specs=[pl.BlockSpec((rb, B_WIDTH), lambda i, j: (i, col0 + j)),
                  pl.BlockSpec((SUBLANES, B_WIDTH), lambda i, j: (jnp.maximum(i * halo - 1, 0), col0 + j)),
                  pl.BlockSpec((SUBLANES, B_WIDTH), lambda i, j: (jnp.minimum((i + 1) * halo, last), col0 + j)),
                  pl.BlockSpec((SHORT_CONV, B_WIDTH), lambda i, j: (0, j)),
                  pl.BlockSpec((1, B_WIDTH), lambda i, j: (0, j))],
        out_specs=pl.BlockSpec((None, rb, B_WIDTH), lambda i, j: (j, i, 0)),
        out_shape=jax.ShapeDtypeStruct((3, m, B_WIDTH), F32),
        compiler_params=_cparams(("parallel", "parallel")),
        name="hyena_short_conv",
    )(proj, proj, proj, conv_w, conv_b.reshape(1, -1))


def _filter_kernel(z_ref, meta_ref, w1_ref, b1_ref, w2_ref, b2_ref, w3_ref, f_ref, dl_ref, o_ref):
    hdn = jnp.sin(f_ref[0:1, :] * (_dot_split(z_ref[...], w1_ref[...]) + b1_ref[...]))
    hdn = jnp.sin(f_ref[1:2, :] * (_dot_split(hdn, w2_ref[...]) + b2_ref[...]))
    filt = _dot_split(hdn, w3_ref[...])
    t = meta_ref[:, 0:1]
    fwd = meta_ref[:, 1:2]
    bwd = meta_ref[:, 2:3]
    win = jnp.exp(-t * dl_ref[...])
    for o in range(HYENA_ORDER):
        hf = filt[:, (2 * o) * B_WIDTH:(2 * o + 1) * B_WIDTH]
        hb = filt[:, (2 * o + 1) * B_WIDTH:(2 * o + 2) * B_WIDTH]
        o_ref[o] = (fwd * hf + bwd * hb) * win


def hyena_filter_taps(L, w1, b1, w2, b2, w3, freq):
    n = 2 * L
    t = np.linspace(0.0, 1.0, L, dtype=np.float32)
    bands = (FILTER_EMB - 1) // 2
    ang = (np.float32(2.0 * math.pi / L) * np.arange(L, dtype=np.float32)[:, None]
           * np.linspace(1e-4, bands - 1, bands, dtype=np.float32)[None, :]).astype(np.float32)
    z = np.concatenate([t[:, None], np.cos(ang), -np.sin(ang)], axis=-1).astype(np.float32)
    lag = np.concatenate([np.arange(L), [0], np.arange(L - 1, 0, -1)])
    meta = np.zeros((n, 8), np.float32)
    meta[:, 0] = t[lag]
    meta[:L, 1] = 1.0
    meta[L + 1:, 2] = 1.0
    deltas = np.abs(np.linspace(math.log(DECAY_TARGET) / SLOW_DECAY, math.log(DECAY_TARGET) / FAST_DECAY, B_WIDTH,
                                dtype=np.float32)).reshape(1, B_WIDTH)
    tr = min(n, 512)
    hid = w1.shape[1]
    fixed = lambda i: (0, 0)
    emb = LANES
    z = np.pad(z, ((0, 0), (0, emb - FILTER_EMB)))
    w1 = jnp.pad(w1, ((0, emb - FILTER_EMB), (0, 0)))
    return pl.pallas_call(
        _filter_kernel,
        grid=(n // tr,),
        in_specs=[pl.BlockSpec((tr, emb), lambda i: (i, 0)), pl.BlockSpec((tr, 8), lambda i: (i, 0)),
                  pl.BlockSpec((emb, hid), fixed), pl.BlockSpec((1, hid), fixed),
                  pl.BlockSpec((hid, hid), fixed), pl.BlockSpec((1, hid), fixed),
                  pl.BlockSpec((hid, HYENA_ORDER * 2 * B_WIDTH), fixed), pl.BlockSpec((2, hid), fixed),
                  pl.BlockSpec((1, B_WIDTH), fixed)],
        out_specs=pl.BlockSpec((HYENA_ORDER, tr, B_WIDTH), lambda i: (0, i, 0)),
        out_shape=jax.ShapeDtypeStruct((HYENA_ORDER, n, B_WIDTH), F32),
        compiler_params=_cparams(("parallel",)),
        name="hyena_filter_taps",
    )(jnp.asarray(z[lag]), jnp.asarray(meta), w1, b1.reshape(1, hid), w2, b2.reshape(1, hid), w3, freq,
      jnp.asarray(deltas))


class _FftPlan:
    def __init__(self, L):
        n = 2 * L
        n1, n2 = n // FFT_MINOR, FFT_MINOR
        h1, k_real = n1 // 2, n1 // 2 + 1
        group = max(1, 32 // h1)
        k1 = k_real
        while (group * 2 * k1) % 8:
            k1 += 1
        self.L, self.n1, self.h1, self.k1, self.group = L, n1, h1, k1, group
        live = (np.arange(k1) < k_real).astype(np.float64)
        th = 2.0 * np.pi / n1
        kk = np.arange(k1)[:, None]
        fwd = np.zeros((2 * k1, n1))
        fwd[0::2] = live[:, None] * np.cos(th * ((kk * np.arange(n1)[None, :]) % n1))
        fwd[1::2] = -live[:, None] * np.sin(th * ((kk * np.arange(n1)[None, :]) % n1))
        self.first_full = fwd.astype(np.float32)
        self.first_half = fwd[:, :h1].astype(np.float32)
        wgt = live * np.where((np.arange(k1) == 0) | (np.arange(k1) == n1 // 2), 1.0, 2.0) / n
        inv = np.zeros((h1, 2 * k1))
        ph = th * ((np.arange(h1)[:, None] * np.arange(k1)[None, :]) % n1)
        inv[:, 0::2] = wgt * np.cos(ph)
        inv[:, 1::2] = -wgt * np.sin(ph)
        self.last = inv.astype(np.float32)
        k = np.arange(k1)[:, None, None] + n1 * np.arange(n2)[None, :, None]
        ph = 2.0 * np.pi * ((k * np.arange(n2)[None, None, :]) % n) / n
        c, s = np.cos(ph), np.sin(ph)
        mf = np.concatenate([np.concatenate([c, s], axis=2), np.concatenate([-s, c], axis=2)], axis=1)
        mf = mf * live[:, None, None]
        self.mid_fwd = mf.astype(np.float32)
        self.mid_inv = np.ascontiguousarray(np.swapaxes(mf, 1, 2)).astype(np.float32)


def _lmul_kernel(f_ref, x_ref, o_ref):
    o_ref[...] = _dot_split(f_ref[...], x_ref[...])


def _lmul_gate_kernel(f_ref, x_ref, g_ref, u_ref, d_ref, o_ref):
    o_ref[...] = g_ref[...] * (_dot_split(f_ref[...], x_ref[...]) + d_ref[...] * u_ref[...])


def _left_multiply(fmat, x2d, x_row0, n_blocks, gate=None):
    r_out, r_in = fmat.shape
    nl = x2d.shape[1]
    tl = FFT_LANE_TILE
    xb0 = x_row0 // r_in
    specs = [pl.BlockSpec((r_out, r_in), lambda b, j: (0, 0)),
             pl.BlockSpec((r_in, tl), lambda b, j: (xb0 + b, j))]
    args = [jnp.asarray(fmat), x2d]
    body = _lmul_kernel
    if gate is not None:
        g2d, g_row0, u2d, u_row0, bias = gate
        specs += [pl.BlockSpec((r_out, tl), lambda b, j: (g_row0 // r_out + b, j)),
                  pl.BlockSpec((r_out, tl), lambda b, j: (u_row0 // r_out + b, j)),
                  pl.BlockSpec((1, tl), lambda b, j: (0, 0))]
        args += [g2d, u2d, jnp.tile(bias.reshape(1, -1), (1, tl // bias.shape[-1]))]
        body = _lmul_gate_kernel
    return pl.pallas_call(
        body,
        grid=(n_blocks, nl // tl),
        in_specs=specs,
        out_specs=pl.BlockSpec((r_out, tl), lambda b, j: (b, j)),
        out_shape=jax.ShapeDtypeStruct((n_blocks * r_out, nl), F32),
        compiler_params=_cparams(("parallel", "parallel")),
        name="fft_outer_stage",
    )(*args)


def _fft_mid_kernel(a_ref, mf_ref, mi_ref, k_ref, o_ref):
    half = FFT_MINOR
    kr = k_ref[0, 0:half, :]
    ki = k_ref[0, half:, :]
    for b in range(a_ref.shape[0]):
        x = _dot_split(mf_ref[0], a_ref[b, 0])
        xr, xi = x[0:half], x[half:]
        y = jnp.concatenate([xr * kr - xi * ki, xr * ki + xi * kr], axis=0)
        o_ref[b, 0] = _dot_split(mi_ref[0], y)


def _fft_spec_kernel(a_ref, mf_ref, o_ref):
    o_ref[0] = _dot_split(mf_ref[0], a_ref[0, 0])


def _fft_spectrum(plan, taps):
    a = _left_multiply(plan.first_full, taps.reshape(plan.n1, FFT_LANES), 0, 1)
    a = a.reshape(1, plan.k1, 2 * FFT_MINOR, B_WIDTH)
    blk = (2 * FFT_MINOR, B_WIDTH)
    return pl.pallas_call(
        _fft_spec_kernel,
        grid=(plan.k1,),
        in_specs=[pl.BlockSpec((1, 1) + blk, lambda k: (0, k, 0, 0)),
                  pl.BlockSpec((1, 2 * FFT_MINOR, 2 * FFT_MINOR), lambda k: (k, 0, 0))],
        out_specs=pl.BlockSpec((1,) + blk, lambda k: (k, 0, 0)),
        out_shape=jax.ShapeDtypeStruct((plan.k1,) + blk, F32),
        compiler_params=_cparams(("parallel",)),
        name="fft_tap_spectrum",
    )(a, jnp.asarray(plan.mid_fwd))


def _long_conv_gated(plan, n_seq, group, u2d, u_row0, g2d, g_row0, spec, bias):
    h1, k1 = plan.h1, plan.k1
    eye = np.eye(group, dtype=np.float32)
    a = _left_multiply(np.kron(eye, plan.first_half), u2d, u_row0, n_seq // group)
    a = a.reshape(n_seq, k1, 2 * FFT_MINOR, B_WIDTH)
    bb = min(n_seq, 8)
    blk = (2 * FFT_MINOR, B_WIDTH)
    mat = pl.BlockSpec((1, 2 * FFT_MINOR, 2 * FFT_MINOR), lambda k, b: (k, 0, 0))
    g = pl.pallas_call(
        _fft_mid_kernel,
        grid=(k1, n_seq // bb),
        in_specs=[pl.BlockSpec((bb, 1) + blk, lambda k, b: (b, k, 0, 0)), mat, mat,
                  pl.BlockSpec((1,) + blk, lambda k, b: (k, 0, 0))],
        out_specs=pl.BlockSpec((bb, 1) + blk, lambda k, b: (b, k, 0, 0)),
        out_shape=jax.ShapeDtypeStruct((n_seq, k1) + blk, F32),
        compiler_params=_cparams(("parallel", "parallel")),
        name="fft_inner_stage",
    )(a, jnp.asarray(plan.mid_fwd), jnp.asarray(plan.mid_inv), spec)
    g = g.reshape(n_seq // group * group * 2 * k1, FFT_LANES)
    return _left_multiply(np.kron(eye, plan.last), g, 0, n_seq // group, gate=(g2d, g_row0, u2d, u_row0, bias))


def hyena_mixer(proj, n_ctx_rows, ctx_seq, lat_seq, conv_w, conv_b, fw1, fb1, fw2, fb2, fw3, freq, fbias):
    m = proj.shape[0]
    sc = hyena_short_conv(proj, conv_w, conv_b, n_ctx_rows, ctx_seq, lat_seq)
    views = sc.reshape(3, m // FFT_MINOR, FFT_LANES)
    outs = []
    for L, row0, n_rows in ((ctx_seq, 0, n_ctx_rows), (lat_seq, n_ctx_rows, m - n_ctx_rows)):
        plan = _FftPlan(L)
        n_seq = n_rows // L
        group = plan.group
        taps = hyena_filter_taps(L, fw1, fb1, fw2, fb2, fw3, freq)
        z2d, z_row0 = views[0], row0 // FFT_MINOR
        for o in range(HYENA_ORDER):
            spec = _fft_spectrum(plan, taps[o])
            z2d = _long_conv_gated(plan, n_seq, group, z2d, z_row0, views[1 + o], row0 // FFT_MINOR, spec, fbias[o])
            z_row0 = 0
        outs.append(z2d.reshape(n_rows, B_WIDTH))
    return jnp.concatenate(outs, axis=0)


def _rope_tables(seq_len):
    n_rows = seq_len // GRID_W
    rows = np.repeat(np.arange(n_rows), GRID_W).astype(np.float32)
    cols = np.tile(np.arange(GRID_W), n_rows).astype(np.float32)
    quarter = C_HEAD_DIM // 4
    inv = (ROPE_BASE ** (-np.arange(quarter, dtype=np.float32) / quarter)).astype(np.float32)
    cos_parts, sin_parts = [], []
    for pos in (rows, cols):
        a = jnp.asarray(pos)[:, None] * jnp.asarray(inv)[None, :]
        cos_parts += [jnp.cos(a), jnp.cos(a)]
        sin_parts += [-jnp.sin(a), jnp.sin(a)]
    return jnp.concatenate(cos_parts, axis=-1), jnp.concatenate(sin_parts, axis=-1)


def kernel(x_prompt, x_sample, state_hgrn, state_ret, c, c_ctx, ada_w, ada_b, ln_g, ln_b, even_w_in, even_w_out, hgrn_lb, hgrn_norm_w, hyena_conv_w, hyena_conv_b, hyena_w1, hyena_b1, hyena_w2, hyena_b2, hyena_w3, hyena_freq, hyena_bias, odd_w_in, odd_w_out, ret_decay, moe_router, moe_w1, moe_w3, moe_w2):
    nb, seq, d = x_prompt.shape
    nl, lseq, _ = x_sample.shape
    n_ctx = nb * seq
    n_lat = nl * lseq
    h = jnp.concatenate([x_prompt.reshape(n_ctx, d), x_sample.reshape(n_lat, d)], axis=0)

    cond = jnp.concatenate([c_ctx[None, :], c], axis=0)
    mods = modulation_table(cond, ada_w, ada_b).reshape(DEPTH, 1 + nl, 6, d)

    lb_soft = jax.nn.softmax(hgrn_lb.astype(F32), axis=0)
    lb_all = jnp.cumsum(lb_soft, axis=0) - lb_soft[0]
    rope_tabs = _rope_tables(lseq)
    zero_hgrn = jnp.zeros((1, 2, A_HEADS, A_HEAD_DIM, A_HEAD_DIM), F32)
    zero_ret = jnp.zeros((1, 2, C_HEADS, C_HEAD_DIM, C_HEAD_DIM), F32)

    hgrn_states, ret_states = [], []
    for l in range(DEPTH):
        j = l // 2
        mod = mods[l]
        if l % 2 == 0:
            proj = in_projection(h, mod, even_w_in[j], n_ctx, lseq)
            o_p, s_p = hgrn_scan(proj, 0, nb, seq, lb_all[j], zero_hgrn)
            o_l, _ = hgrn_scan(proj, n_ctx, nl, lseq, lb_all[j], state_hgrn[:, j])
            hgrn_states.append(s_p)
            scan_out = jnp.concatenate([o_p, o_l], axis=1)
            y_b = hyena_mixer(proj, n_ctx, seq, lseq, hyena_conv_w[j], hyena_conv_b[j], hyena_w1[j], hyena_b1[j],
                              hyena_w2[j], hyena_b2[j], hyena_w3[j], hyena_freq[j], hyena_bias[j])
            h, u2, aff_t = post_mixer(True, scan_out, proj, (hgrn_norm_w[j], y_b), h, mod, even_w_out[j],
                                      ln_g[l, 0], ln_b[l, 0], moe_router[l], n_ctx, lseq)
        else:
            proj = in_projection(h, mod, odd_w_in[j], n_ctx, lseq)
            log_gamma = jax.nn.log_sigmoid(ret_decay[j].astype(F32))
            o_p, s_p = retention_scan(proj, 0, nb, seq, log_gamma, zero_ret, None)
            o_l, _ = retention_scan(proj, n_ctx, nl, lseq, log_gamma, state_ret[:, j], rope_tabs)
            ret_states.append(s_p)
            scan_out = jnp.concatenate([o_p, o_l], axis=1)
            h, u2, aff_t = post_mixer(False, scan_out, proj, None, h, mod, odd_w_out[j],
                                      ln_g[l, 0], ln_b[l, 0], moe_router[l], n_ctx, lseq)
        h = moe_layer(aff_t, u2, h, mod, moe_w1[l], moe_w3[l], moe_w2[l], ln_g[l, 1], ln_b[l, 1], nb, seq, nl, lseq)

    y_prompt = h[:n_ctx].reshape(nb, seq, d)
    y_sample = h[n_ctx:].reshape(nl, lseq, d)
    new_state_hgrn = jnp.stack(hgrn_states, axis=1)
    new_state_ret = jnp.stack(ret_states, axis=1)
    return (y_prompt, y_sample, new_state_hgrn, new_state_ret)
```

```python
import functools
import math

import jax
import jax.numpy as jnp
import numpy as np
from jax import lax
from jax.experimental import pallas as pl
from jax.experimental.pallas import tpu as pltpu

F32 = jnp.float32
BF16 = jnp.bfloat16

D_MODEL = 1024
DEPTH = 4
GRID_W = 64
A_WIDTH = D_MODEL // 2
A_HEADS = 4
A_HEAD_DIM = A_WIDTH // A_HEADS
F_MIN = 1e-30
B_WIDTH = D_MODEL - A_WIDTH
HYENA_ORDER = 2
FILTER_EMB = 33
SHORT_CONV = 3
DECAY_TARGET = 1e-2
FAST_DECAY = 0.3
SLOW_DECAY = 1.5
C_HEADS = 4
C_HEAD_DIM = D_MODEL // C_HEADS
ROPE_BASE = 10000.0
N_EXPERTS = 16
CAP_FACTOR = 2
LN_EPS = 1e-5
DN_ALPHA = (2 * DEPTH) ** 0.25

LANES = 128
SUBLANES = 8
VMEM_LIMIT = 56 * 1024 * 1024
ROW_TILE = 512
SCAN_TILE = 256
HGRN_SUB = 16


def _cparams(sem):
    return pltpu.CompilerParams(dimension_semantics=sem, vmem_limit_bytes=VMEM_LIMIT)


def _cond_index(tile, tile_rows, n_ctx_rows, lat_seq):
    row = tile * tile_rows
    return jnp.where(row < n_ctx_rows, 0, 1 + (row - n_ctx_rows) // lat_seq)


def _mod_kernel(c_ref, w_ref, b_ref, o_ref):
    c = c_ref[...]
    s = c * jax.nn.sigmoid(c)
    o_ref[0] = jnp.dot(s.astype(BF16), w_ref[0].astype(BF16), preferred_element_type=F32) + b_ref[0]


def modulation_table(cond, ada_w, ada_b):
    r, d = cond.shape
    depth, _, n = ada_w.shape
    tn = 1024
    return pl.pallas_call(
        _mod_kernel,
        grid=(depth, n // tn),
        in_specs=[pl.BlockSpec((r, d), lambda l, j: (0, 0)),
                  pl.BlockSpec((1, d, tn), lambda l, j: (l, 0, j)),
                  pl.BlockSpec((1, 1, tn), lambda l, j: (l, 0, j))],
        out_specs=pl.BlockSpec((1, r, tn), lambda l, j: (l, 0, j)),
        out_shape=jax.ShapeDtypeStruct((depth, r, n), F32),
        compiler_params=_cparams(("parallel", "parallel")),
        name="modulation",
    )(cond, ada_w, ada_b.reshape(depth, 1, n))


def _inproj_kernel(x_ref, mod_ref, w_ref, o_ref):
    shift = mod_ref[0, 0:1, :]
    scale = mod_ref[0, 1:2, :]
    u = (x_ref[...] * (1.0 + scale) + shift).astype(BF16)
    o_ref[...] = jnp.dot(u, w_ref[...].astype(BF16), preferred_element_type=F32)


def in_projection(x, mod, w, n_ctx_rows, lat_seq):
    m, d = x.shape
    n = w.shape[1]
    tm, tn = ROW_TILE, 1024
    cidx = functools.partial(_cond_index, tile_rows=tm, n_ctx_rows=n_ctx_rows, lat_seq=lat_seq)
    return pl.pallas_call(
        _inproj_kernel,
        grid=(n // tn, m // tm),
        in_specs=[pl.BlockSpec((tm, d), lambda j, i: (i, 0)),
                  pl.BlockSpec((1, 6, d), lambda j, i: (cidx(i), 0, 0)),
                  pl.BlockSpec((d, tn), lambda j, i: (0, j))],
        out_specs=pl.BlockSpec((tm, tn), lambda j, i: (i, j)),
        out_shape=jax.ShapeDtypeStruct((m, n), F32),
        compiler_params=_cparams(("parallel", "parallel")),
        name="in_projection",
    )(x, mod, w)


def _layer_norm_rows(z, g, b):
    mu = jnp.mean(z, axis=-1, keepdims=True)
    zc = z - mu
    var = jnp.mean(zc * zc, axis=-1, keepdims=True)
    return zc * lax.rsqrt(var + LN_EPS) * g + b


def _post_mixer(y, h_ref, mod_ref, w_ref, lng_ref, lnb_ref, r_ref, h_out, u_out, aff_out):
    gate1 = mod_ref[0, 2:3, :]
    shift2 = mod_ref[0, 3:4, :]
    scale2 = mod_ref[0, 4:5, :]
    mix = jnp.dot(y.astype(BF16), w_ref[...].astype(BF16), preferred_element_type=F32)
    hn = _layer_norm_rows(DN_ALPHA * h_ref[...] + gate1 * mix, lng_ref[...], lnb_ref[...])
    h_out[...] = hn
    u2 = (hn * (1.0 + scale2) + shift2).astype(BF16)
    u_out[...] = u2
    logits = lax.dot_general(r_ref[...].astype(BF16), u2, (((1,), (1,)), ((), ())), preferred_element_type=F32)
    mx = jnp.max(logits, axis=0, keepdims=True)
    ex = jnp.exp(logits - mx)
    aff_out[...] = ex / jnp.sum(ex, axis=0, keepdims=True)


def _group_pick(ctx_tiles, ctx_refs, lat_refs):
    is_ctx = pl.program_id(0) < ctx_tiles
    return jnp.where(is_ctx, sum(r[...] for r in ctx_refs), sum(r[...] for r in lat_refs))


def _post_even_kernel(ofc_ref, obc_ref, ofl_ref, obl_ref, ybc_ref, ybl_ref, g_ref, nw_ref, h_ref, mod_ref, w_ref,
                      lng_ref, lnb_ref, r_ref, h_out, u_out, aff_out, *, ctx_tiles):
    o = _group_pick(ctx_tiles, (ofc_ref, obc_ref), (ofl_ref, obl_ref))
    g = g_ref[...]
    parts = []
    for hd in range(A_HEADS):
        sl = slice(hd * A_HEAD_DIM, (hd + 1) * A_HEAD_DIM)
        oh = o[:, sl]
        parts.append(oh * lax.rsqrt(jnp.mean(oh * oh, axis=-1, keepdims=True) + LN_EPS))
    ya = jnp.concatenate(parts, axis=-1) * nw_ref[...] * (g * jax.nn.sigmoid(g))
    y = jnp.concatenate([ya, _group_pick(ctx_tiles, (ybc_ref,), (ybl_ref,))], axis=-1)
    _post_mixer(y, h_ref, mod_ref, w_ref, lng_ref, lnb_ref, r_ref, h_out, u_out, aff_out)


def _post_odd_kernel(ofc_ref, obc_ref, ofl_ref, obl_ref, g_ref, h_ref, mod_ref, w_ref, lng_ref, lnb_ref, r_ref,
                     h_out, u_out, aff_out, *, ctx_tiles):
    o = _group_pick(ctx_tiles, (ofc_ref, obc_ref), (ofl_ref, obl_ref))
    g = g_ref[...]
    parts = []
    for hd in range(C_HEADS):
        sl = slice(hd * C_HEAD_DIM, (hd + 1) * C_HEAD_DIM)
        oh = o[:, sl]
        mu = jnp.mean(oh, axis=-1, keepdims=True)
        oc = oh - mu
        parts.append(oc * lax.rsqrt(jnp.mean(oc * oc, axis=-1, keepdims=True) + LN_EPS))
    y = jnp.concatenate(parts, axis=-1) * (g * jax.nn.sigmoid(g))
    _post_mixer(y, h_ref, mod_ref, w_ref, lng_ref, lnb_ref, r_ref, h_out, u_out, aff_out)


def post_mixer(even, scan_ctx, scan_lat, proj, extra, h, mod, w_out, ln_g, ln_b, router, n_ctx_rows, lat_seq):
    m, d = h.shape
    e = router.shape[1]
    tm = ROW_TILE
    ct = n_ctx_rows // tm
    cidx = functools.partial(_cond_index, tile_rows=tm, n_ctx_rows=n_ctx_rows, lat_seq=lat_seq)
    width = scan_ctx.shape[-1]
    row = lambda i: (i, 0)
    fixed = lambda i: (0, 0)
    ctx_row = lambda i: jnp.minimum(i, ct - 1)
    lat_row = lambda i: jnp.maximum(i - ct, 0)
    plane = lambda p, rowfn, w: pl.BlockSpec((None, tm, w), lambda i: (p, rowfn(i), 0))
    common_specs = [pl.BlockSpec((tm, d), row),
                    pl.BlockSpec((1, 6, d), lambda i: (cidx(i), 0, 0)),
                    pl.BlockSpec((d, d), fixed),
                    pl.BlockSpec((1, d), fixed),
                    pl.BlockSpec((1, d), fixed),
                    pl.BlockSpec((e, d), fixed)]
    common_args = [h, mod, w_out, ln_g.reshape(1, d), ln_b.reshape(1, d), router.T]
    specs = [plane(0, ctx_row, width), plane(1, ctx_row, width), plane(0, lat_row, width), plane(1, lat_row, width)]
    args = [scan_ctx, scan_ctx, scan_lat, scan_lat]
    if even:
        norm_w, yb_ctx, yb_lat = extra
        gate_col = 4 * A_WIDTH // width
        specs += [plane(0, ctx_row, B_WIDTH), plane(0, lat_row, B_WIDTH),
                  pl.BlockSpec((tm, width), lambda i: (i, gate_col)), pl.BlockSpec((1, width), fixed)]
        args += [yb_ctx, yb_lat, proj, norm_w.reshape(1, width)]
        body = _post_even_kernel
    else:
        gate_col = 3
        specs += [pl.BlockSpec((tm, width), lambda i: (i, gate_col))]
        args += [proj]
        body = _post_odd_kernel
    return pl.pallas_call(
        functools.partial(body, ctx_tiles=ct),
        grid=(m // tm,),
        in_specs=specs + common_specs,
        out_specs=[pl.BlockSpec((tm, d), row), pl.BlockSpec((tm, d), row), pl.BlockSpec((e, tm), lambda i: (0, i))],
        out_shape=[jax.ShapeDtypeStruct((m, d), F32), jax.ShapeDtypeStruct((m, d), BF16),
                   jax.ShapeDtypeStruct((e, m), F32)],
        compiler_params=_cparams(("arbitrary",)),
        name="post_mixer_even" if even else "post_mixer_odd",
    )(*(args + common_args))


def _rope_halves(x, cos, sin_signed):
    outs = []
    for p in range(2):
        sl = slice(p * LANES, (p + 1) * LANES)
        xp = x[:, sl]
        outs.append(xp * cos[:, sl] + pltpu.roll(xp, LANES // 2, axis=1) * sin_signed[:, sl])
    return jnp.concatenate(outs, axis=-1)


def _retention_kernel(lg_ref, q_ref, k_ref, v_ref, cos_ref, sin_ref, s0_ref, o_ref, sfin_ref, s_scr, *, rope):
    d = pl.program_id(1)
    t = pl.program_id(2)
    c = q_ref.shape[0]
    dk = C_HEAD_DIM

    @pl.when(t == 0)
    def _():
        s_scr[...] = s0_ref[0, 0]

    ti = lax.broadcasted_iota(jnp.int32, (c, c), 0)
    si = lax.broadcasted_iota(jnp.int32, (c, c), 1)
    diff = jnp.where(d == 0, ti - si, si - ti).astype(F32)
    jt = lax.broadcasted_iota(jnp.int32, (c, dk), 0)
    eq = jnp.where(d == 0, jt + 1, c - jt).astype(F32)
    ek = jnp.where(d == 0, c - 1 - jt, jt).astype(F32)
    for hd in range(C_HEADS):
        lg = lg_ref[d, hd]
        sl = slice(hd * dk, (hd + 1) * dk)
        q = q_ref[:, sl]
        k = k_ref[:, sl] * (dk ** -0.5)
        if rope:
            q = _rope_halves(q, cos_ref[...], sin_ref[...])
            k = _rope_halves(k, cos_ref[...], sin_ref[...])
        v = v_ref[:, sl].astype(BF16)
        rel = jnp.where(diff >= 0.0, jnp.exp(lg * jnp.maximum(diff, 0.0)), 0.0)
        att = lax.dot_general(q.astype(BF16), k.astype(BF16), (((1,), (1,)), ((), ())),
                              preferred_element_type=F32) * rel
        intra = jnp.dot(att.astype(BF16), v, preferred_element_type=F32)
        s_old = s_scr[hd]
        inter = jnp.dot((q * jnp.exp(lg * eq)).astype(BF16), s_old.astype(BF16), preferred_element_type=F32)
        o_ref[:, sl] = inter + intra
        kd = (k * jnp.exp(lg * ek)).astype(BF16)
        s_new = s_old * jnp.exp(lg * c) + lax.dot_general(kd, v, (((0,), (0,)), ((), ())),
                                                          preferred_element_type=F32)
        s_scr[hd] = s_new

    @pl.when(t == pl.num_programs(2) - 1)
    def _():
        sfin_ref[0, 0] = s_scr[...]


def retention_scan(proj, row0, n_seq, seq_len, log_gamma, s0, rope_tabs):
    d = D_MODEL
    c = SCAN_TILE
    nt = seq_len // c
    t0 = row0 // c
    rope = rope_tabs is not None
    if rope:
        cos, sin = rope_tabs
    else:
        cos = sin = jnp.zeros((c, C_HEAD_DIM), F32)
    tile = lambda dd, t: t + dd * (nt - 1 - 2 * t)
    tok = lambda col: pl.BlockSpec((c, d), lambda b, dd, t: (t0 + b * nt + tile(dd, t), col))
    rope_spec = pl.BlockSpec((c, C_HEAD_DIM), (lambda b, dd, t: (tile(dd, t), 0)) if rope else (lambda b, dd, t: (0, 0)))
    shared_s0 = s0.shape[0] == 1
    st_shape = (1, 1, C_HEADS, C_HEAD_DIM, C_HEAD_DIM)
    return pl.pallas_call(
        functools.partial(_retention_kernel, rope=rope),
        grid=(n_seq, 2, nt),
        in_specs=[pl.BlockSpec(memory_space=pltpu.SMEM), tok(0), tok(1), tok(2), rope_spec, rope_spec,
                  pl.BlockSpec(st_shape, lambda b, dd, t: (0 if shared_s0 else b, dd, 0, 0, 0))],
        out_specs=[pl.BlockSpec((None, c, d), lambda b, dd, t: (dd, b * nt + tile(dd, t), 0)),
                   pl.BlockSpec(st_shape, lambda b, dd, t: (b, dd, 0, 0, 0))],
        out_shape=[jax.ShapeDtypeStruct((2, n_seq * seq_len, d), F32),
                   jax.ShapeDtypeStruct((n_seq, 2, C_HEADS, C_HEAD_DIM, C_HEAD_DIM), F32)],
        scratch_shapes=[pltpu.VMEM((C_HEADS, C_HEAD_DIM, C_HEAD_DIM), F32)],
        compiler_params=_cparams(("parallel", "parallel", "arbitrary")),
        name="retention_scan",
    )(log_gamma, proj, proj, proj, cos, sin, s0)


def _split3(x):
    hi = x.astype(BF16)
    r1 = x - hi.astype(F32)
    mid = r1.astype(BF16)
    lo = (r1 - mid.astype(F32)).astype(BF16)
    return hi, mid, lo


def _hgrn_kernel(q_ref, v_ref, z_ref, lb_ref, s0_ref, o_ref, sfin_ref, st_scr, qe_scr, k_scr, b_scr, *, sub):
    d = pl.program_id(1)
    t = pl.program_id(2)
    tl, w = q_ref.shape
    dh = A_HEAD_DIM
    nsub = tl // sub

    @pl.when(t == 0)
    def _():
        for hd in range(A_HEADS):
            st_scr[hd] = s0_ref[0, 0, hd].T

    lb = lb_ref[0]
    z = z_ref[...]
    sg = jax.nn.sigmoid(z)
    lf = jnp.log(jnp.maximum(lb + (1.0 - lb) * sg, F_MIN))
    k_scr[...] = (1.0 - lb) * jax.nn.sigmoid(-z)
    ti = lax.broadcasted_iota(jnp.int32, (tl, tl), 0)
    si = lax.broadcasted_iota(jnp.int32, (tl, tl), 1)
    sign = jnp.where(d == 0, 1, -1)
    seen = ((ti - si) * sign >= 0) & ((ti // sub) == (si // sub))
    tri = jnp.where(seen, 1.0, 0.0).astype(BF16)
    b = sum(jnp.dot(tri, part, preferred_element_type=F32) for part in _split3(lf))
    b_scr[...] = b
    q = q_ref[...]
    q = q * jax.nn.sigmoid(q)
    qe_scr[...] = q * jnp.exp(b)

    ones = jnp.ones((dh, dh), BF16)
    rowi = lax.broadcasted_iota(jnp.int32, (sub, dh), 0)

    def step(i, carry):
        blk = jnp.where(d == 0, i, nsub - 1 - i)
        r0 = pl.multiple_of(blk * sub, sub)
        rows = pl.ds(r0, sub)
        for hd in range(A_HEADS):
            sl = slice(hd * dh, (hd + 1) * dh)
            qs = q_ref[rows, sl]
            qs = qs * jax.nn.sigmoid(qs)
            ks = k_scr[rows, sl]
            vs = v_ref[rows, sl]
            bs = b_scr[rows, sl]
            st = st_scr[hd]
            inter = lax.dot_general(qe_scr[rows, sl].astype(BF16), st.astype(BF16), (((1,), (1,)), ((), ())),
                                    preferred_element_type=F32)
            es = []
            for s in range(sub):
                vis = (rowi - s) * sign >= 0
                e = qs * ks[s:s + 1, :] * jnp.exp(jnp.minimum(bs - bs[s:s + 1, :], 0.0))
                es.append(jnp.where(vis, e, 0.0).astype(BF16))
            att = jnp.dot(jnp.concatenate(es, axis=0), ones, preferred_element_type=F32)
            intra = att[0:sub] * vs[0:1, :]
            for s in range(1, sub):
                intra = intra + att[s * sub:(s + 1) * sub] * vs[s:s + 1, :]
            o_ref[rows, sl] = inter + intra
            b_end = jnp.where(d == 0, bs[sub - 1:sub, :], bs[0:1, :])
            kd = (ks * jnp.exp(b_end - bs)).astype(BF16)
            st_scr[hd] = st * jnp.exp(b_end) + lax.dot_general(vs.astype(BF16), kd, (((0,), (0,)), ((), ())),
                                                               preferred_element_type=F32)
        return carry

    lax.fori_loop(0, nsub, step, 0)

    @pl.when(t == pl.num_programs(2) - 1)
    def _():
        for hd in range(A_HEADS):
            sfin_ref[0, 0, hd] = st_scr[hd].T


def hgrn_scan(proj, row0, n_seq, seq_len, lb, s0):
    w = A_WIDTH
    tl = SCAN_TILE
    nt = seq_len // tl
    t0 = row0 // tl
    tile = lambda dd, t: t + dd * (nt - 1 - 2 * t)
    tok = lambda colfn: pl.BlockSpec((tl, w), lambda b, dd, t: (t0 + b * nt + tile(dd, t), colfn(dd)))
    shared_s0 = s0.shape[0] == 1
    st_shape = (1, 1, A_HEADS, A_HEAD_DIM, A_HEAD_DIM)
    return pl.pallas_call(
        functools.partial(_hgrn_kernel, sub=HGRN_SUB),
        grid=(n_seq, 2, nt),
        in_specs=[tok(lambda dd: 0), tok(lambda dd: 1), tok(lambda dd: 2 + dd),
                  pl.BlockSpec((1, 1, w), lambda b, dd, t: (dd, 0, 0)),
                  pl.BlockSpec(st_shape, lambda b, dd, t: (0 if shared_s0 else b, dd, 0, 0, 0))],
        out_specs=[pl.BlockSpec((None, tl, w), lambda b, dd, t: (dd, b * nt + tile(dd, t), 0)),
                   pl.BlockSpec(st_shape, lambda b, dd, t: (b, dd, 0, 0, 0))],
        out_shape=[jax.ShapeDtypeStruct((2, n_seq * seq_len, w), F32),
                   jax.ShapeDtypeStruct((n_seq, 2, A_HEADS, A_HEAD_DIM, A_HEAD_DIM), F32)],
        scratch_shapes=[pltpu.VMEM((A_HEADS, A_HEAD_DIM, A_HEAD_DIM), F32),
                        pltpu.VMEM((tl, w), F32), pltpu.VMEM((tl, w), F32), pltpu.VMEM((tl, w), F32)],
        compiler_params=_cparams(("parallel", "parallel", "arbitrary")),
        name="hgrn_scan",
    )(proj, proj, proj, lb.reshape(2, 1, w), s0)


def _exclusive_count(x):
    r, n = x.shape
    ji = lax.broadcasted_iota(jnp.int32, (LANES, LANES), 0)
    ii = lax.broadcasted_iota(jnp.int32, (LANES, LANES), 1)
    upper = jnp.where(ji < ii, 1.0, 0.0).astype(BF16)
    carry = jnp.zeros((r, 1), F32)
    outs = []
    for blk in range(n // LANES):
        xb = x[:, blk * LANES:(blk + 1) * LANES]
        outs.append(jnp.dot(xb.astype(BF16), upper, preferred_element_type=F32) + carry)
        carry = carry + jnp.sum(xb, axis=1, keepdims=True)
    return jnp.concatenate(outs, axis=1)


def _select_kernel(a_ref, pos_ref, *, cap):
    a = a_ref[...]

    def count(mask):
        return jnp.sum(jnp.where(mask, 1.0, 0.0), axis=1, keepdims=True)

    def body(i, t):
        cand = t | lax.shift_left(jnp.int32(1), 30 - i)
        return jnp.where(count(a >= pltpu.bitcast(cand, F32)) >= cap, cand, t)

    t = lax.fori_loop(0, 31, body, jnp.zeros((a.shape[0], 1), jnp.int32))
    v = jnp.min(jnp.where(a >= pltpu.bitcast(t, F32), a, jnp.inf), axis=1, keepdims=True)

    def too_low(v):
        return jnp.max(count(a > v)) >= cap

    def step_up(v):
        nxt = jnp.min(jnp.where(a > v, a, jnp.inf), axis=1, keepdims=True)
        return jnp.where(count(a > v) >= cap, nxt, v)

    v = lax.while_loop(too_low, step_up, v)
    gt = jnp.where(a > v, 1.0, 0.0)
    eq = jnp.where(a == v, 1.0, 0.0)
    need = cap - jnp.sum(gt, axis=1, keepdims=True)
    sel = gt + eq * jnp.where(_exclusive_count(eq) < need, 1.0, 0.0)
    pos = _exclusive_count(sel)
    pos_ref[...] = jnp.where(sel > 0.0, pos, -1.0).astype(jnp.int32)


def route_select(aff_rows, cap):
    r, n = aff_rows.shape
    rb = min(r, LANES)
    return pl.pallas_call(
        functools.partial(_select_kernel, cap=cap),
        grid=(r // rb,),
        in_specs=[pl.BlockSpec((rb, n), lambda i: (i, 0))],
        out_specs=pl.BlockSpec((rb, n), lambda i: (i, 0)),
        out_shape=jax.ShapeDtypeStruct((r, n), jnp.int32),
        compiler_params=_cparams(("parallel",)),
        name="route_select",
    )(aff_rows)


def _gather_ctx_kernel(pos_ref, u_ref, o_ref):
    e, n = pos_ref.shape
    cap = o_ref.shape[1]
    slot = lax.broadcasted_iota(jnp.int32, (cap, n), 0)
    onehot = jnp.concatenate([jnp.where(pos_ref[ei:ei + 1, :] == slot, 1.0, 0.0).astype(BF16) for ei in range(e)],
                             axis=0)
    x = jnp.dot(onehot, u_ref[...], preferred_element_type=F32)
    for ei in range(e):
        o_ref[ei] = x[ei * cap:(ei + 1) * cap].astype(BF16)


def _gather_lat_kernel(pos_ref, u_ref, o_ref):
    cap = o_ref.shape[1]
    n = u_ref.shape[0]
    chunk = min(n, 2 * ROW_TILE)
    slot = lax.broadcasted_iota(jnp.int32, (cap, chunk), 0)
    acc = jnp.zeros(o_ref.shape[1:], F32)
    for c0 in range(0, n, chunk):
        onehot = jnp.where(pos_ref[0, :, c0:c0 + chunk] == slot, 1.0, 0.0).astype(BF16)
        acc = acc + jnp.dot(onehot, u_ref[c0:c0 + chunk, :], preferred_element_type=F32)
    o_ref[0] = acc.astype(BF16)


def route_gather(pos_ctx, pos_lat, u2, nb, seq, nl, lseq, n_exp):
    d = u2.shape[1]
    cap_c = CAP_FACTOR * seq // n_exp
    cap_l = CAP_FACTOR * lseq // n_exp
    xs_c = pl.pallas_call(
        _gather_ctx_kernel,
        grid=(nb,),
        in_specs=[pl.BlockSpec((n_exp, seq), lambda b: (b, 0)), pl.BlockSpec((seq, d), lambda b: (b, 0))],
        out_specs=pl.BlockSpec((n_exp, cap_c, d), lambda b: (0, b, 0)),
        out_shape=jax.ShapeDtypeStruct((n_exp, nb * cap_c, d), BF16),
        compiler_params=_cparams(("parallel",)),
        name="route_gather_ctx",
    )(pos_ctx, u2)
    lat_blk0 = nb * seq // lseq
    xs_l = pl.pallas_call(
        _gather_lat_kernel,
        grid=(nl, n_exp),
        in_specs=[pl.BlockSpec((1, 1, lseq), lambda b, ei: (b * n_exp + ei, 0, 0)),
                  pl.BlockSpec((lseq, d), lambda b, ei: (lat_blk0 + b, 0))],
        out_specs=pl.BlockSpec((1, cap_l, d), lambda b, ei: (ei, b, 0)),
        out_shape=jax.ShapeDtypeStruct((n_exp, nl * cap_l, d), BF16),
        compiler_params=_cparams(("parallel", "arbitrary")),
        name="route_gather_lat",
    )(pos_lat.reshape(nl * n_exp, 1, lseq), u2)
    return xs_c, xs_l


def _combine_kernel(pos_ref, gate_ref, yc_ref, yl_ref, h_ref, mod_ref, lng_ref, lnb_ref, o_ref, *, ctx_tiles):
    tn, e = pos_ref.shape
    d = h_ref.shape[1]

    def finish(moe):
        gate2 = mod_ref[0, 5:6, :]
        o_ref[...] = _layer_norm_rows(DN_ALPHA * h_ref[...] + gate2 * moe, lng_ref[...], lnb_ref[...])

    @pl.when(pl.program_id(0) < ctx_tiles)
    def _():
        cap = yc_ref.shape[1]
        lane = lax.broadcasted_iota(jnp.int32, (e, e * cap), 1)
        owner = lax.broadcasted_iota(jnp.int32, (e, e * cap), 0)
        expand = jnp.where(lane // cap == owner, 1.0, 0.0).astype(BF16)
        posx = jnp.dot(pos_ref[...].astype(F32).astype(BF16), expand, preferred_element_type=F32)
        gatex = sum(jnp.dot(part, expand, preferred_element_type=F32) for part in _split3(gate_ref[...]))
        slot = (lax.broadcasted_iota(jnp.int32, (tn, e * cap), 1) % cap).astype(F32)
        w = jnp.where(posx == slot, gatex, 0.0)
        w_hi = w.astype(BF16)
        w_lo = (w - w_hi.astype(F32)).astype(BF16)
        y = yc_ref[...].reshape(e * cap, d)
        finish(jnp.dot(w_hi, y, preferred_element_type=F32) + jnp.dot(w_lo, y, preferred_element_type=F32))

    @pl.when(pl.program_id(0) >= ctx_tiles)
    def _():
        cap = yl_ref.shape[1]
        slot = lax.broadcasted_iota(jnp.int32, (tn, cap), 1)
        acc = jnp.zeros((tn, d), F32)
        for ei in range(e):
            onehot = jnp.where(pos_ref[:, ei:ei + 1] == slot, 1.0, 0.0).astype(BF16)
            acc = acc + gate_ref[:, ei:ei + 1] * jnp.dot(onehot, yl_ref[ei], preferred_element_type=F32)
        finish(acc)


def route_combine(pos_t, gate_t, ys, h, mod, ln_g, ln_b, nb, seq, nl, lseq):
    m, d = h.shape
    e = pos_t.shape[1]
    tn = seq
    cap_c = CAP_FACTOR * seq // e
    cap_l = CAP_FACTOR * lseq // e
    lat_blk0 = nb * cap_c // cap_l
    per_seq = lseq // tn
    cidx = functools.partial(_cond_index, tile_rows=tn, n_ctx_rows=nb * seq, lat_seq=lseq)
    row = lambda i: (i, 0)
    fixed = lambda i: (0, 0)
    return pl.pallas_call(
        functools.partial(_combine_kernel, ctx_tiles=nb),
        grid=(m // tn,),
        in_specs=[pl.BlockSpec((tn, e), row), pl.BlockSpec((tn, e), row),
                  pl.BlockSpec((e, cap_c, d), lambda i: (0, jnp.minimum(i, nb - 1), 0)),
                  pl.BlockSpec((e, cap_l, d), lambda i: (0, lat_blk0 + jnp.maximum(i - nb, 0) // per_seq, 0)),
                  pl.BlockSpec((tn, d), row),
                  pl.BlockSpec((1, 6, d), lambda i: (cidx(i), 0, 0)),
                  pl.BlockSpec((1, d), fixed), pl.BlockSpec((1, d), fixed)],
        out_specs=pl.BlockSpec((tn, d), row),
        out_shape=jax.ShapeDtypeStruct((m, d), F32),
        compiler_params=_cparams(("arbitrary",)),
        name="route_combine",
    )(pos_t, gate_t, ys, ys, h, mod, ln_g.reshape(1, d), ln_b.reshape(1, d))


def _expert_kernel(xc_ref, xl_ref, w1_ref, w3_ref, w2_ref, o_ref, *, ctx_tiles):
    x = jnp.where(pl.program_id(1) < ctx_tiles, xc_ref[0], xl_ref[0])
    a = jnp.dot(x, w1_ref[0].astype(BF16), preferred_element_type=F32)
    g = jnp.dot(x, w3_ref[0].astype(BF16), preferred_element_type=F32)
    hid = (a * jax.nn.sigmoid(a) * g).astype(BF16)
    o_ref[0] = jnp.dot(hid, w2_ref[0].astype(BF16), preferred_element_type=F32).astype(BF16)


def expert_ffn(xs_c, xs_l, w1, w3, w2):
    e, mc, d = xs_c.shape
    ml = xs_l.shape[1]
    f = w1.shape[2]
    tm = ROW_TILE
    ct, lt = mc // tm, ml // tm
    return pl.pallas_call(
        functools.partial(_expert_kernel, ctx_tiles=ct),
        grid=(e, ct + lt),
        in_specs=[pl.BlockSpec((1, tm, d), lambda ei, i: (ei, jnp.minimum(i, ct - 1), 0)),
                  pl.BlockSpec((1, tm, d), lambda ei, i: (ei, jnp.maximum(i - ct, 0), 0)),
                  pl.BlockSpec((1, d, f), lambda ei, i: (ei, 0, 0)),
                  pl.BlockSpec((1, d, f), lambda ei, i: (ei, 0, 0)),
                  pl.BlockSpec((1, f, d), lambda ei, i: (ei, 0, 0))],
        out_specs=pl.BlockSpec((1, tm, d), lambda ei, i: (ei, i, 0)),
        out_shape=jax.ShapeDtypeStruct((e, mc + ml, d), BF16),
        compiler_params=_cparams(("parallel", "arbitrary")),
        name="expert_ffn",
    )(xs_c, xs_l, w1, w3, w2)


def moe_layer(aff_t, u2, h, mod, w1, w3, w2, ln_g, ln_b, nb, seq, nl, lseq):
    e = aff_t.shape[0]
    n_ctx = nb * seq
    seq_rows = lambda a, n, length: a.reshape(e, n, length).transpose(1, 0, 2).reshape(n * e, length)
    pos_c = route_select(seq_rows(aff_t[:, :n_ctx], nb, seq), CAP_FACTOR * seq // e)
    pos_l = route_select(seq_rows(aff_t[:, n_ctx:], nl, lseq), CAP_FACTOR * lseq // e)
    xs_c, xs_l = route_gather(pos_c, pos_l, u2, nb, seq, nl, lseq, e)
    ys = expert_ffn(xs_c, xs_l, w1, w3, w2)
    tok_rows = lambda p, n, length: p.reshape(n, e, length).transpose(0, 2, 1).reshape(n * length, e)
    pos_t = jnp.concatenate([tok_rows(pos_c, nb, seq), tok_rows(pos_l, nl, lseq)], axis=0)
    return route_combine(pos_t, aff_t.T, ys, h, mod, ln_g, ln_b, nb, seq, nl, lseq)


def _dot_split(a, b, passes=3):
    ah, bh = a.astype(BF16), b.astype(BF16)
    out = jnp.dot(ah, bh, preferred_element_type=F32)
    if passes >= 3:
        al = (a - ah.astype(F32)).astype(BF16)
        bl = (b - bh.astype(F32)).astype(BF16)
        out = out + jnp.dot(ah, bl, preferred_element_type=F32) + jnp.dot(al, bh, preferred_element_type=F32)
    return out


def _short_conv_kernel(x_ref, before_ref, after_ref, w_ref, b_ref, o_ref, *, ctx_tiles, ctx_seq, lat_seq):
    rows = x_ref.shape[0]
    x = x_ref[...]
    seq = jnp.where(pl.program_id(0) < ctx_tiles, ctx_seq, lat_seq)
    row = lax.broadcasted_iota(jnp.int32, x.shape, 0)
    pos = (pl.program_id(0) * rows + row) % seq
    prev = jnp.where(row == 0, before_ref[SUBLANES - 1:SUBLANES, :], pltpu.roll(x, 1, axis=0))
    nxt = jnp.where(row == rows - 1, after_ref[0:1, :], pltpu.roll(x, rows - 1, axis=0))
    prev = jnp.where(pos == 0, 0.0, prev)
    nxt = jnp.where(pos == seq - 1, 0.0, nxt)
    o_ref[...] = prev * w_ref[0:1, :] + x * w_ref[1:2, :] + nxt * w_ref[2:3, :] + b_ref[...]


def hyena_short_conv(proj, conv_w, conv_b, n_ctx_rows, ctx_seq, lat_seq):
    m = proj.shape[0]
    rb = 2 * ROW_TILE
    col0 = 5 * A_WIDTH // B_WIDTH
    halo = rb // SUBLANES
    last = m // SUBLANES - 1
    return pl.pallas_call(
        functools.partial(_short_conv_kernel, ctx_tiles=n_ctx_rows // rb, ctx_seq=ctx_seq, lat_seq=lat_seq),
        grid=(m // rb, 3),
        in_specs=[pl.BlockSpec((rb, B_WIDTH), lambda i, j: (i, col0 + j)),
                  pl.BlockSpec((SUBLANES, B_WIDTH), lambda i, j: (jnp.maximum(i * halo - 1, 0), col0 + j)),
                  pl.BlockSpec((SUBLANES, B_WIDTH), lambda i, j: (jnp.minimum((i + 1) * halo, last), col0 + j)),
                  pl.BlockSpec((SHORT_CONV, B_WIDTH), lambda i, j: (0, j)),
                  pl.BlockSpec((1, B_WIDTH), lambda i, j: (0, j))],
        out_specs=pl.BlockSpec((None, rb, B_WIDTH), lambda i, j: (j, i, 0)),
        out_shape=jax.ShapeDtypeStruct((3, m, B_WIDTH), F32),
        compiler_params=_cparams(("parallel", "parallel")),
        name="hyena_short_conv",
    )(proj, proj, proj, conv_w, conv_b.reshape(1, -1))


def _filter_kernel(z_ref, meta_ref, w1_ref, b1_ref, w2_ref, b2_ref, w3_ref, f_ref, dl_ref, o_ref):
    hdn = jnp.sin(f_ref[0:1, :] * (_dot_split(z_ref[...], w1_ref[...]) + b1_ref[...]))
    hdn = jnp.sin(f_ref[1:2, :] * (_dot_split(hdn, w2_ref[...]) + b2_ref[...]))
    filt = _dot_split(hdn, w3_ref[...])
    t = meta_ref[:, 0:1]
    fwd = meta_ref[:, 1:2]
    bwd = meta_ref[:, 2:3]
    win = jnp.exp(-t * dl_ref[...])
    for o in range(HYENA_ORDER):
        hf = filt[:, (2 * o) * B_WIDTH:(2 * o + 1) * B_WIDTH]
        hb = filt[:, (2 * o + 1) * B_WIDTH:(2 * o + 2) * B_WIDTH]
        o_ref[o] = (fwd * hf + bwd * hb) * win


def hyena_filter_taps(L, w1, b1, w2, b2, w3, freq):
    n = 2 * L
    t = jnp.linspace(0.0, 1.0, L, dtype=F32)[:, None]
    bands = (FILTER_EMB - 1) // 2
    ang = (2.0 * math.pi / L) * jnp.arange(L, dtype=F32)[:, None] * jnp.linspace(1e-4, bands - 1, bands, dtype=F32)[None, :]
    z = jnp.concatenate([t, jnp.cos(ang), -jnp.sin(ang)], axis=-1)
    deltas = jnp.abs(jnp.linspace(math.log(DECAY_TARGET) / SLOW_DECAY, math.log(DECAY_TARGET) / FAST_DECAY, B_WIDTH,
                                  dtype=F32)).reshape(1, B_WIDTH)
    lag = np.concatenate([np.arange(L), [0], np.arange(L - 1, 0, -1)])
    side = np.zeros((n, 7), np.float32)
    side[:L, 0] = 1.0
    side[L + 1:, 1] = 1.0
    meta = jnp.concatenate([t[lag], jnp.asarray(side)], axis=1)
    tr = min(n, 512)
    hid = w1.shape[1]
    fixed = lambda i: (0, 0)
    emb = LANES
    z = jnp.pad(z[lag], ((0, 0), (0, emb - FILTER_EMB)))
    w1 = jnp.pad(w1, ((0, emb - FILTER_EMB), (0, 0)))
    return pl.pallas_call(
        _filter_kernel,
        grid=(n // tr,),
        in_specs=[pl.BlockSpec((tr, emb), lambda i: (i, 0)), pl.BlockSpec((tr, 8), lambda i: (i, 0)),
                  pl.BlockSpec((emb, hid), fixed), pl.BlockSpec((1, hid), fixed),
                  pl.BlockSpec((hid, hid), fixed), pl.BlockSpec((1, hid), fixed),
                  pl.BlockSpec((hid, HYENA_ORDER * 2 * B_WIDTH), fixed), pl.BlockSpec((2, hid), fixed),
                  pl.BlockSpec((1, B_WIDTH), fixed)],
        out_specs=pl.BlockSpec((HYENA_ORDER, tr, B_WIDTH), lambda i: (0, i, 0)),
        out_shape=jax.ShapeDtypeStruct((HYENA_ORDER, n, B_WIDTH), F32),
        compiler_params=_cparams(("parallel",)),
        name="hyena_filter_taps",
    )(z, meta, w1, b1.reshape(1, hid), w2, b2.reshape(1, hid), w3, freq, deltas)


class _FftPlan:
    def __init__(self, L, minor):
        n = 2 * L
        n1 = n // minor
        h1, k1 = n1 // 2, n1 // 2 + 1
        self.L, self.minor, self.n1, self.h1, self.k1 = L, minor, n1, h1, k1
        th = 2.0 * np.pi / n1
        ph = th * ((np.arange(k1)[:, None] * np.arange(n1)[None, :]) % n1)
        fwd = np.zeros((2 * k1, n1))
        fwd[0::2], fwd[1::2] = np.cos(ph), -np.sin(ph)
        self.first_full = self._two_f32(fwd)
        self.first_half = self._two_f32(fwd[:, :h1])
        wgt = np.where((np.arange(k1) == 0) | (np.arange(k1) == n1 // 2), 1.0, 2.0) / n
        ph = th * ((np.arange(h1)[:, None] * np.arange(k1)[None, :]) % n1)
        inv = np.zeros((h1, 2 * k1))
        inv[:, 0::2], inv[:, 1::2] = wgt * np.cos(ph), -wgt * np.sin(ph)
        self.last = self._two_f32(inv)
        k = np.arange(k1)[:, None, None] + n1 * np.arange(minor)[None, :, None]
        ph = 2.0 * np.pi * ((k * np.arange(minor)[None, None, :]) % n) / n
        c, s = np.cos(ph), np.sin(ph)
        mf = np.concatenate([np.concatenate([c, s], axis=2), np.concatenate([-s, c], axis=2)], axis=1)
        self.mid_fwd = self._hi_lo(mf)
        self.mid_inv = self._hi_lo(np.swapaxes(mf, 1, 2))

    @staticmethod
    def _two_f32(table):
        hi = table.astype(np.float32)
        lo = (table - hi.astype(np.float64)).astype(np.float32)
        return jnp.asarray(np.concatenate([hi.reshape(-1), lo.reshape(-1)]))

    @staticmethod
    def _hi_lo(mat):
        bits = mat.astype(np.float32).view(np.uint32)
        bits = (bits + 0x7FFF + ((bits >> 16) & 1)) & np.uint32(0xFFFF0000)
        hi = bits.view(np.float32)
        lo = (mat - hi.astype(np.float64)).astype(np.float32)
        return jnp.asarray(hi).astype(BF16), jnp.asarray(lo)


def _dot_hi_lo(m_hi, m_rest, x):
    m_lo = m_rest.astype(BF16)
    x_hi = x.astype(BF16)
    x_lo = (x - x_hi.astype(F32)).astype(BF16)
    return (jnp.dot(m_hi, x_hi, preferred_element_type=F32) + jnp.dot(m_hi, x_lo, preferred_element_type=F32)
            + jnp.dot(m_lo, x_hi, preferred_element_type=F32))


def _coef(tab_ref, i, size):
    return tab_ref[i] + tab_ref[size + i]


def _outer_forward(cf_ref, k, n_k, n_slabs, slab):
    size = 2 * n_k * n_slabs
    ar = ai = None
    for n1 in range(n_slabs):
        xs = slab(n1)
        tr = _coef(cf_ref, (2 * k) * n_slabs + n1, size) * xs
        ti = _coef(cf_ref, (2 * k + 1) * n_slabs + n1, size) * xs
        ar, ai = (tr, ti) if ar is None else (ar + tr, ai + ti)
    return jnp.concatenate([ar, ai], axis=0)


def _hyena_conv_kernel(cf_ref, ci_ref, u_ref, g_ref, mfh_ref, mfl_ref, mih_ref, mil_ref, k_ref, d_ref, o_ref, *,
                       minor, h1, n_k, n_seq, seq_len):
    k = pl.program_id(2)
    ct = u_ref.shape[1]

    def slab(n1):
        parts = [u_ref[g * seq_len + n1 * minor:g * seq_len + (n1 + 1) * minor, :] for g in range(n_seq)]
        return parts[0] if n_seq == 1 else jnp.concatenate(parts, axis=1)

    @pl.when(k == 0)
    def _():
        o_ref[...] = jnp.zeros(o_ref.shape, F32)

    x = _dot_hi_lo(mfh_ref[0], mfl_ref[0], _outer_forward(cf_ref, k, n_k, h1, slab))
    xr, xi = x[:minor], x[minor:]
    kr = k_ref[0, :minor, :]
    ki = k_ref[0, minor:, :]
    if n_seq > 1:
        kr = jnp.concatenate([kr] * n_seq, axis=1)
        ki = jnp.concatenate([ki] * n_seq, axis=1)
    y = jnp.concatenate([xr * kr - xi * ki, xr * ki + xi * kr], axis=0)
    gm = _dot_hi_lo(mih_ref[0], mil_ref[0], y)
    gr, gi = gm[:minor], gm[minor:]
    for n1 in range(h1):
        base = n1 * 2 * n_k + 2 * k
        part = _coef(ci_ref, base, 2 * n_k * h1) * gr + _coef(ci_ref, base + 1, 2 * n_k * h1) * gi
        for g in range(n_seq):
            rows = slice(g * seq_len + n1 * minor, g * seq_len + (n1 + 1) * minor)
            o_ref[rows, :] += part[:, g * ct:(g + 1) * ct]

    @pl.when(k == n_k - 1)
    def _():
        o_ref[...] = g_ref[...] * (o_ref[...] + d_ref[...] * u_ref[...])


def _hyena_spectrum_kernel(cf_ref, t_ref, mfh_ref, mfl_ref, o_ref, *, minor, n1, n_k):
    k = pl.program_id(1)
    a = _outer_forward(cf_ref, k, n_k, n1, lambda j: t_ref[j * minor:(j + 1) * minor, :])
    o_ref[0] = _dot_hi_lo(mfh_ref[0], mfl_ref[0], a)


def _hyena_spectrum(plan, taps, ct):
    n, w = taps.shape
    m2 = 2 * plan.minor
    mat = pl.BlockSpec((1, m2, m2), lambda c, k: (k, 0, 0))
    return pl.pallas_call(
        functools.partial(_hyena_spectrum_kernel, minor=plan.minor, n1=plan.n1, n_k=plan.k1),
        grid=(w // ct, plan.k1),
        in_specs=[pl.BlockSpec(memory_space=pltpu.SMEM), pl.BlockSpec((n, ct), lambda c, k: (0, c)), mat, mat],
        out_specs=pl.BlockSpec((1, m2, ct), lambda c, k: (k, 0, c)),
        out_shape=jax.ShapeDtypeStruct((plan.k1, m2, w), F32),
        compiler_params=_cparams(("parallel", "parallel")),
        name="hyena_tap_spectrum",
    )(plan.first_full, taps, *plan.mid_fwd)


def _hyena_conv(plan, n_blocks, n_seq, ct, u, u_plane, u_blk0, gate, gate_plane, g_blk0, spec, bias):
    w = u.shape[-1]
    rows = n_seq * plan.L
    m2 = 2 * plan.minor
    tok = lambda plane, blk0: pl.BlockSpec((None, rows, ct), lambda b, c, k: (plane, blk0 + b, c))
    mat = pl.BlockSpec((1, m2, m2), lambda b, c, k: (k, 0, 0))
    smem = pl.BlockSpec(memory_space=pltpu.SMEM)
    return pl.pallas_call(
        functools.partial(_hyena_conv_kernel, minor=plan.minor, h1=plan.h1, n_k=plan.k1, n_seq=n_seq,
                          seq_len=plan.L),
        grid=(n_blocks, w // ct, plan.k1),
        in_specs=[smem, smem, tok(u_plane, u_blk0), tok(gate_plane, g_blk0), mat, mat, mat, mat,
                  pl.BlockSpec((1, m2, ct), lambda b, c, k: (k, 0, c)),
                  pl.BlockSpec((1, ct), lambda b, c, k: (0, c))],
        out_specs=pl.BlockSpec((None, rows, ct), lambda b, c, k: (0, b, c)),
        out_shape=jax.ShapeDtypeStruct((1, n_blocks * rows, w), F32),
        compiler_params=_cparams(("parallel", "parallel", "arbitrary")),
        name="hyena_long_conv",
    )(plan.first_half, plan.last, u, gate, *plan.mid_fwd, *plan.mid_inv, spec,
      bias.reshape(1, w))


def hyena_mixer(proj, n_ctx_rows, ctx_seq, lat_seq, conv_w, conv_b, fw1, fb1, fw2, fb2, fw3, freq, fbias):
    m = proj.shape[0]
    sc = hyena_short_conv(proj, conv_w, conv_b, n_ctx_rows, ctx_seq, lat_seq)
    outs = []
    for L, row0, n_rows, minor, n_seq, ct in ((ctx_seq, 0, n_ctx_rows, 128, 4, B_WIDTH),
                                              (lat_seq, n_ctx_rows, m - n_ctx_rows, 256, 1, B_WIDTH // 2)):
        plan = _FftPlan(L, minor)
        n_blocks = n_rows // (n_seq * L)
        blk0 = row0 // (n_seq * L)
        taps = hyena_filter_taps(L, fw1, fb1, fw2, fb2, fw3, freq)
        z, z_plane, z_blk0 = sc, 0, blk0
        for o in range(HYENA_ORDER):
            spec = _hyena_spectrum(plan, taps[o], ct)
            z = _hyena_conv(plan, n_blocks, n_seq, ct, z, z_plane, z_blk0, sc, 1 + o, blk0, spec, fbias[o])
            z_plane, z_blk0 = 0, 0
        outs.append(z)
    return outs


def _rope_tables(seq_len):
    n_rows = seq_len // GRID_W
    rows = jnp.repeat(jnp.arange(n_rows), GRID_W)
    cols = jnp.tile(jnp.arange(GRID_W), n_rows)
    quarter = C_HEAD_DIM // 4
    inv = ROPE_BASE ** (-jnp.arange(quarter, dtype=F32) / quarter)
    cos_parts, sin_parts = [], []
    for pos in (rows, cols):
        a = pos.astype(F32)[:, None] * inv
        cos_parts += [jnp.cos(a), jnp.cos(a)]
        sin_parts += [-jnp.sin(a), jnp.sin(a)]
    return jnp.concatenate(cos_parts, axis=-1), jnp.concatenate(sin_parts, axis=-1)


def kernel(x_prompt, x_sample, state_hgrn, state_ret, c, c_ctx, ada_w, ada_b, ln_g, ln_b, even_w_in, even_w_out, hgrn_lb, hgrn_norm_w, hyena_conv_w, hyena_conv_b, hyena_w1, hyena_b1, hyena_w2, hyena_b2, hyena_w3, hyena_freq, hyena_bias, odd_w_in, odd_w_out, ret_decay, moe_router, moe_w1, moe_w3, moe_w2):
    nb, seq, d = x_prompt.shape
    nl, lseq, _ = x_sample.shape
    n_ctx = nb * seq
    n_lat = nl * lseq
    h = jnp.concatenate([x_prompt.reshape(n_ctx, d), x_sample.reshape(n_lat, d)], axis=0)

    cond = jnp.concatenate([c_ctx[None, :], c], axis=0)
    mods = modulation_table(cond, ada_w, ada_b).reshape(DEPTH, 1 + nl, 6, d)

    lb_soft = jax.nn.softmax(hgrn_lb.astype(F32), axis=0)
    lb_all = jnp.cumsum(lb_soft, axis=0) - lb_soft[0]
    rope_tabs = _rope_tables(lseq)
    zero_hgrn = jnp.zeros((1, 2, A_HEADS, A_HEAD_DIM, A_HEAD_DIM), F32)
    zero_ret = jnp.zeros((1, 2, C_HEADS, C_HEAD_DIM, C_HEAD_DIM), F32)

    hgrn_states, ret_states = [], []
    for l in range(DEPTH):
        j = l // 2
        mod = mods[l]
        if l % 2 == 0:
            proj = in_projection(h, mod, even_w_in[j], n_ctx, lseq)
            o_p, s_p = hgrn_scan(proj, 0, nb, seq, lb_all[j], zero_hgrn)
            o_l, _ = hgrn_scan(proj, n_ctx, nl, lseq, lb_all[j], state_hgrn[:, j])
            hgrn_states.append(s_p)
            yb_p, yb_l = hyena_mixer(proj, n_ctx, seq, lseq, hyena_conv_w[j], hyena_conv_b[j], hyena_w1[j],
                                     hyena_b1[j], hyena_w2[j], hyena_b2[j], hyena_w3[j], hyena_freq[j], hyena_bias[j])
            h, u2, aff_t = post_mixer(True, o_p, o_l, proj, (hgrn_norm_w[j], yb_p, yb_l), h, mod, even_w_out[j],
                                      ln_g[l, 0], ln_b[l, 0], moe_router[l], n_ctx, lseq)
        else:
            proj = in_projection(h, mod, odd_w_in[j], n_ctx, lseq)
            log_gamma = jax.nn.log_sigmoid(ret_decay[j].astype(F32))
            o_p, s_p = retention_scan(proj, 0, nb, seq, log_gamma, zero_ret, None)
            o_l, _ = retention_scan(proj, n_ctx, nl, lseq, log_gamma, state_ret[:, j], rope_tabs)
            ret_states.append(s_p)
            h, u2, aff_t = post_mixer(False, o_p, o_l, proj, None, h, mod, odd_w_out[j],
                                      ln_g[l, 0], ln_b[l, 0], moe_router[l], n_ctx, lseq)
        h = moe_layer(aff_t, u2, h, mod, moe_w1[l], moe_w3[l], moe_w2[l], ln_g[l, 1], ln_b[l, 1], nb, seq, nl, lseq)

    y_prompt = h[:n_ctx].reshape(nb, seq, d)
    y_sample = h[n_ctx:].reshape(nl, lseq, d)
    new_state_hgrn = jnp.stack(hgrn_states, axis=1)
    new_state_ret = jnp.stack(ret_states, axis=1)
    return (y_prompt, y_sample, new_state_hgrn, new_state_ret)
```

```python
import functools
import math

import jax
import jax.numpy as jnp
import numpy as np
from jax import lax
from jax.experimental import pallas as pl
from jax.experimental.pallas import tpu as pltpu

F32 = jnp.float32
BF16 = jnp.bfloat16

D_MODEL = 1024
DEPTH = 4
GRID_W = 64
A_WIDTH = D_MODEL // 2
A_HEADS = 4
A_HEAD_DIM = A_WIDTH // A_HEADS
F_MIN = 1e-30
B_WIDTH = D_MODEL - A_WIDTH
HYENA_ORDER = 2
FILTER_EMB = 33
SHORT_CONV = 3
DECAY_TARGET = 1e-2
FAST_DECAY = 0.3
SLOW_DECAY = 1.5
C_HEADS = 4
C_HEAD_DIM = D_MODEL // C_HEADS
ROPE_BASE = 10000.0
N_EXPERTS = 16
CAP_FACTOR = 2
LN_EPS = 1e-5
DN_ALPHA = (2 * DEPTH) ** 0.25

LANES = 128
SUBLANES = 8
VMEM_LIMIT = 56 * 1024 * 1024
ROW_TILE = 512
SCAN_TILE = 256
HGRN_SUB = 16
ROUTE_CHUNK = 256
ROUTE_WINDOW = 128
SLOT_ALIGN = 16


def _cparams(sem):
    return pltpu.CompilerParams(dimension_semantics=sem, vmem_limit_bytes=VMEM_LIMIT)


def _cond_index(tile, tile_rows, n_ctx_rows, lat_seq):
    row = tile * tile_rows
    return jnp.where(row < n_ctx_rows, 0, 1 + (row - n_ctx_rows) // lat_seq)


def _mod_kernel(c_ref, w_ref, b_ref, o_ref):
    c = c_ref[...]
    s = c * jax.nn.sigmoid(c)
    o_ref[0] = jnp.dot(s.astype(BF16), w_ref[0].astype(BF16), preferred_element_type=F32) + b_ref[0]


def modulation_table(cond, ada_w, ada_b):
    r, d = cond.shape
    depth, _, n = ada_w.shape
    tn = 1024
    return pl.pallas_call(
        _mod_kernel,
        grid=(depth, n // tn),
        in_specs=[pl.BlockSpec((r, d), lambda l, j: (0, 0)),
                  pl.BlockSpec((1, d, tn), lambda l, j: (l, 0, j)),
                  pl.BlockSpec((1, 1, tn), lambda l, j: (l, 0, j))],
        out_specs=pl.BlockSpec((1, r, tn), lambda l, j: (l, 0, j)),
        out_shape=jax.ShapeDtypeStruct((depth, r, n), F32),
        compiler_params=_cparams(("parallel", "parallel")),
        name="modulation",
    )(cond, ada_w, ada_b.reshape(depth, 1, n))


def _inproj_kernel(x_ref, mod_ref, w_ref, o_ref):
    shift = mod_ref[0, 0:1, :]
    scale = mod_ref[0, 1:2, :]
    u = (x_ref[...] * (1.0 + scale) + shift).astype(BF16)
    o_ref[...] = jnp.dot(u, w_ref[...].astype(BF16), preferred_element_type=F32)


def in_projection(x, mod, w_stack, layer, n_ctx_rows, lat_seq):
    m, d = x.shape
    n = w_stack.shape[2]
    tm, tn = ROW_TILE, 1024
    cidx = functools.partial(_cond_index, tile_rows=tm, n_ctx_rows=n_ctx_rows, lat_seq=lat_seq)
    return pl.pallas_call(
        _inproj_kernel,
        grid=(n // tn, m // tm),
        in_specs=[pl.BlockSpec((tm, d), lambda j, i: (i, 0)),
                  pl.BlockSpec((1, 6, d), lambda j, i: (cidx(i), 0, 0)),
                  pl.BlockSpec((None, d, tn), lambda j, i: (layer, 0, j))],
        out_specs=pl.BlockSpec((tm, tn), lambda j, i: (i, j)),
        out_shape=jax.ShapeDtypeStruct((m, n), F32),
        compiler_params=_cparams(("parallel", "parallel")),
        name="in_projection",
    )(x, mod, w_stack)


def _layer_norm_rows(z, g, b):
    mu = jnp.mean(z, axis=-1, keepdims=True)
    zc = z - mu
    var = jnp.mean(zc * zc, axis=-1, keepdims=True)
    return zc * lax.rsqrt(var + LN_EPS) * g + b


def _post_mixer(y, h_ref, mod_ref, w_ref, lng_ref, lnb_ref, r_ref, h_out, u_out, aff_out):
    gate1 = mod_ref[0, 2:3, :]
    shift2 = mod_ref[0, 3:4, :]
    scale2 = mod_ref[0, 4:5, :]
    mix = jnp.dot(y.astype(BF16), w_ref[...].astype(BF16), preferred_element_type=F32)
    hn = _layer_norm_rows(DN_ALPHA * h_ref[...] + gate1 * mix, lng_ref[...], lnb_ref[...])
    h_out[...] = hn
    u2 = (hn * (1.0 + scale2) + shift2).astype(BF16)
    u_out[...] = u2
    logits = lax.dot_general(r_ref[...].astype(BF16), u2, (((1,), (1,)), ((), ())), preferred_element_type=F32)
    mx = jnp.max(logits, axis=0, keepdims=True)
    ex = jnp.exp(logits - mx)
    aff_out[...] = ex / jnp.sum(ex, axis=0, keepdims=True)


def _group_pick(ctx_tiles, ctx_refs, lat_refs):
    is_ctx = pl.program_id(0) < ctx_tiles
    return jnp.where(is_ctx, sum(r[...] for r in ctx_refs), sum(r[...] for r in lat_refs))


def _post_even_kernel(ofc_ref, obc_ref, ofl_ref, obl_ref, ybc_ref, ybl_ref, g_ref, nw_ref, h_ref, mod_ref, w_ref,
                      lng_ref, lnb_ref, r_ref, h_out, u_out, aff_out, *, ctx_tiles):
    o = _group_pick(ctx_tiles, (ofc_ref, obc_ref), (ofl_ref, obl_ref))
    g = g_ref[...]
    parts = []
    for hd in range(A_HEADS):
        sl = slice(hd * A_HEAD_DIM, (hd + 1) * A_HEAD_DIM)
        oh = o[:, sl]
        parts.append(oh * lax.rsqrt(jnp.mean(oh * oh, axis=-1, keepdims=True) + LN_EPS))
    ya = jnp.concatenate(parts, axis=-1) * nw_ref[...] * (g * jax.nn.sigmoid(g))
    y = jnp.concatenate([ya, _group_pick(ctx_tiles, (ybc_ref,), (ybl_ref,))], axis=-1)
    _post_mixer(y, h_ref, mod_ref, w_ref, lng_ref, lnb_ref, r_ref, h_out, u_out, aff_out)


def _post_odd_kernel(ofc_ref, obc_ref, ofl_ref, obl_ref, g_ref, h_ref, mod_ref, w_ref, lng_ref, lnb_ref, r_ref,
                     h_out, u_out, aff_out, *, ctx_tiles):
    o = _group_pick(ctx_tiles, (ofc_ref, obc_ref), (ofl_ref, obl_ref))
    g = g_ref[...]
    parts = []
    for hd in range(C_HEADS):
        sl = slice(hd * C_HEAD_DIM, (hd + 1) * C_HEAD_DIM)
        oh = o[:, sl]
        mu = jnp.mean(oh, axis=-1, keepdims=True)
        oc = oh - mu
        parts.append(oc * lax.rsqrt(jnp.mean(oc * oc, axis=-1, keepdims=True) + LN_EPS))
    y = jnp.concatenate(parts, axis=-1) * (g * jax.nn.sigmoid(g))
    _post_mixer(y, h_ref, mod_ref, w_ref, lng_ref, lnb_ref, r_ref, h_out, u_out, aff_out)


def post_mixer(even, scan_ctx, scan_lat, proj, extra, h, mod, w_out_stack, layer, ln_g, ln_b, router, n_ctx_rows,
               lat_seq):
    m, d = h.shape
    e = router.shape[1]
    tm = ROW_TILE
    ct = n_ctx_rows // tm
    cidx = functools.partial(_cond_index, tile_rows=tm, n_ctx_rows=n_ctx_rows, lat_seq=lat_seq)
    width = scan_ctx.shape[-1]
    row = lambda i: (i, 0)
    fixed = lambda i: (0, 0)
    ctx_row = lambda i: jnp.minimum(i, ct - 1)
    lat_row = lambda i: jnp.maximum(i - ct, 0)
    plane = lambda p, rowfn, w: pl.BlockSpec((None, tm, w), lambda i: (p, rowfn(i), 0))
    common_specs = [pl.BlockSpec((tm, d), row),
                    pl.BlockSpec((1, 6, d), lambda i: (cidx(i), 0, 0)),
                    pl.BlockSpec((None, d, d), lambda i: (layer, 0, 0)),
                    pl.BlockSpec((1, d), fixed),
                    pl.BlockSpec((1, d), fixed),
                    pl.BlockSpec((e, d), fixed)]
    common_args = [h, mod, w_out_stack, ln_g.reshape(1, d), ln_b.reshape(1, d), router.T]
    specs = [plane(0, ctx_row, width), plane(1, ctx_row, width), plane(0, lat_row, width), plane(1, lat_row, width)]
    args = [scan_ctx, scan_ctx, scan_lat, scan_lat]
    if even:
        norm_w, yb_ctx, yb_lat = extra
        gate_col = 4 * A_WIDTH // width
        specs += [plane(0, ctx_row, B_WIDTH), plane(0, lat_row, B_WIDTH),
                  pl.BlockSpec((tm, width), lambda i: (i, gate_col)), pl.BlockSpec((1, width), fixed)]
        args += [yb_ctx, yb_lat, proj, norm_w.reshape(1, width)]
        body = _post_even_kernel
    else:
        gate_col = 3
        specs += [pl.BlockSpec((tm, width), lambda i: (i, gate_col))]
        args += [proj]
        body = _post_odd_kernel
    return pl.pallas_call(
        functools.partial(body, ctx_tiles=ct),
        grid=(m // tm,),
        in_specs=specs + common_specs,
        out_specs=[pl.BlockSpec((tm, d), row), pl.BlockSpec((tm, d), row), pl.BlockSpec((e, tm), lambda i: (0, i))],
        out_shape=[jax.ShapeDtypeStruct((m, d), F32), jax.ShapeDtypeStruct((m, d), BF16),
                   jax.ShapeDtypeStruct((e, m), F32)],
        compiler_params=_cparams(("arbitrary",)),
        name="post_mixer_even" if even else "post_mixer_odd",
    )(*(args + common_args))


def _rope_halves(x, cos, sin_signed):
    outs = []
    for p in range(2):
        sl = slice(p * LANES, (p + 1) * LANES)
        xp = x[:, sl]
        outs.append(xp * cos[:, sl] + pltpu.roll(xp, LANES // 2, axis=1) * sin_signed[:, sl])
    return jnp.concatenate(outs, axis=-1)


def _retention_kernel(lg_ref, q_ref, k_ref, v_ref, cos_ref, sin_ref, s0_ref, o_ref, sfin_ref, s_scr, *, rope):
    d = pl.program_id(1)
    t = pl.program_id(2)
    c = q_ref.shape[0]
    dk = C_HEAD_DIM

    @pl.when(t == 0)
    def _():
        s_scr[...] = s0_ref[0, 0]

    ti = lax.broadcasted_iota(jnp.int32, (c, c), 0)
    si = lax.broadcasted_iota(jnp.int32, (c, c), 1)
    diff = jnp.where(d == 0, ti - si, si - ti).astype(F32)
    jt = lax.broadcasted_iota(jnp.int32, (c, dk), 0)
    eq = jnp.where(d == 0, jt + 1, c - jt).astype(F32)
    ek = jnp.where(d == 0, c - 1 - jt, jt).astype(F32)
    for hd in range(C_HEADS):
        lg = lg_ref[d, hd]
        sl = slice(hd * dk, (hd + 1) * dk)
        q = q_ref[:, sl]
        k = k_ref[:, sl] * (dk ** -0.5)
        if rope:
            q = _rope_halves(q, cos_ref[...], sin_ref[...])
            k = _rope_halves(k, cos_ref[...], sin_ref[...])
        v = v_ref[:, sl].astype(BF16)
        rel = jnp.where(diff >= 0.0, jnp.exp(lg * jnp.maximum(diff, 0.0)), 0.0)
        att = lax.dot_general(q.astype(BF16), k.astype(BF16), (((1,), (1,)), ((), ())),
                              preferred_element_type=F32) * rel
        intra = jnp.dot(att.astype(BF16), v, preferred_element_type=F32)
        s_old = s_scr[hd]
        inter = jnp.dot((q * jnp.exp(lg * eq)).astype(BF16), s_old.astype(BF16), preferred_element_type=F32)
        o_ref[:, sl] = inter + intra
        kd = (k * jnp.exp(lg * ek)).astype(BF16)
        s_new = s_old * jnp.exp(lg * c) + lax.dot_general(kd, v, (((0,), (0,)), ((), ())),
                                                          preferred_element_type=F32)
        s_scr[hd] = s_new

    @pl.when(t == pl.num_programs(2) - 1)
    def _():
        sfin_ref[0, 0] = s_scr[...]


def retention_scan(proj, row0, n_seq, seq_len, log_gamma, s0, rope_tabs):
    d = D_MODEL
    c = SCAN_TILE
    nt = seq_len // c
    t0 = row0 // c
    rope = rope_tabs is not None
    if rope:
        cos, sin = rope_tabs
    else:
        cos = sin = jnp.zeros((c, C_HEAD_DIM), F32)
    tile = lambda dd, t: t + dd * (nt - 1 - 2 * t)
    tok = lambda col: pl.BlockSpec((c, d), lambda b, dd, t: (t0 + b * nt + tile(dd, t), col))
    rope_spec = pl.BlockSpec((c, C_HEAD_DIM), (lambda b, dd, t: (tile(dd, t), 0)) if rope else (lambda b, dd, t: (0, 0)))
    shared_s0 = s0.shape[0] == 1
    st_shape = (1, 1, C_HEADS, C_HEAD_DIM, C_HEAD_DIM)
    return pl.pallas_call(
        functools.partial(_retention_kernel, rope=rope),
        grid=(n_seq, 2, nt),
        in_specs=[pl.BlockSpec(memory_space=pltpu.SMEM), tok(0), tok(1), tok(2), rope_spec, rope_spec,
                  pl.BlockSpec(st_shape, lambda b, dd, t: (0 if shared_s0 else b, dd, 0, 0, 0))],
        out_specs=[pl.BlockSpec((None, c, d), lambda b, dd, t: (dd, b * nt + tile(dd, t), 0)),
                   pl.BlockSpec(st_shape, lambda b, dd, t: (b, dd, 0, 0, 0))],
        out_shape=[jax.ShapeDtypeStruct((2, n_seq * seq_len, d), F32),
                   jax.ShapeDtypeStruct((n_seq, 2, C_HEADS, C_HEAD_DIM, C_HEAD_DIM), F32)],
        scratch_shapes=[pltpu.VMEM((C_HEADS, C_HEAD_DIM, C_HEAD_DIM), F32)],
        compiler_params=_cparams(("parallel", "parallel", "arbitrary")),
        name="retention_scan",
    )(log_gamma, proj, proj, proj, cos, sin, s0)


def _split3(x):
    hi = x.astype(BF16)
    r1 = x - hi.astype(F32)
    mid = r1.astype(BF16)
    lo = (r1 - mid.astype(F32)).astype(BF16)
    return hi, mid, lo


def _hgrn_kernel(q_ref, v_ref, z_ref, lb_ref, s0_ref, o_ref, sfin_ref, st_scr, qa_scr, qe_scr, k_scr, b_scr, b2_scr,
                 r_scr, bias_scr, *, sub):
    d = pl.program_id(1)
    t = pl.program_id(2)
    tl, w = q_ref.shape
    dh = A_HEAD_DIM
    nsub = tl // sub
    log2e = 1.0 / math.log(2.0)

    @pl.when(t == 0)
    def _():
        for hd in range(A_HEADS):
            st_scr[hd] = s0_ref[0, 0, hd].T

    lb = lb_ref[0]
    z = z_ref[...]
    sg = jax.nn.sigmoid(z)
    lf = jnp.log(jnp.maximum(lb + (1.0 - lb) * sg, F_MIN))
    k = (1.0 - lb) * jax.nn.sigmoid(-z)
    k_scr[...] = k
    ti = lax.broadcasted_iota(jnp.int32, (tl, tl), 0)
    si = lax.broadcasted_iota(jnp.int32, (tl, tl), 1)
    sign = jnp.where(d == 0, 1, -1)
    seen = ((ti - si) * sign >= 0) & ((ti // sub) == (si // sub))
    tri = jnp.where(seen, 1.0, 0.0).astype(BF16)
    b = sum(jnp.dot(tri, part, preferred_element_type=F32) for part in _split3(lf))
    b_scr[...] = b
    b2_scr[...] = b * log2e
    r_scr[...] = b * log2e - jnp.log2(k)
    q = q_ref[...]
    q = q * jax.nn.sigmoid(q)
    qa_scr[...] = q
    qe_scr[...] = q * jnp.exp(b)
    pr = lax.broadcasted_iota(jnp.int32, (sub * sub, dh), 0)
    bias_scr[...] = jnp.where(((pr % sub) - (pr // sub)) * sign >= 0, 0.0, -1e30)

    ones = jnp.ones((dh, dh), BF16)

    def step(i, carry):
        blk = jnp.where(d == 0, i, nsub - 1 - i)
        r0 = pl.multiple_of(blk * sub, sub)
        rows = pl.ds(r0, sub)
        for hd in range(A_HEADS):
            sl = slice(hd * dh, (hd + 1) * dh)
            qs = qa_scr[rows, sl]
            ks = k_scr[rows, sl]
            vs = v_ref[rows, sl]
            bs = b_scr[rows, sl]
            b2 = b2_scr[rows, sl]
            rs = r_scr[rows, sl]
            st = st_scr[hd]
            inter = lax.dot_general(qe_scr[rows, sl].astype(BF16), st.astype(BF16), (((1,), (1,)), ((), ())),
                                    preferred_element_type=F32)
            es = [(qs * jnp.exp2(b2 - rs[s:s + 1, :] + bias_scr[s * sub:(s + 1) * sub, :])).astype(BF16)
                  for s in range(sub)]
            att =jnp.dot(jnp.concatenate(es, axis=0), ones, preferred_element_type=F32)
            intra = att[0:sub] * vs[0:1, :]
            for s in range(1, sub):
                intra = intra + att[s * sub:(s + 1) * sub] * vs[s:s + 1, :]
            o_ref[rows, sl] = inter + intra
            b_end = jnp.where(d == 0, bs[sub - 1:sub, :], bs[0:1, :])
            kd = (ks * jnp.exp(b_end - bs)).astype(BF16)
            st_scr[hd] = st * jnp.exp(b_end) + lax.dot_general(vs.astype(BF16), kd, (((0,), (0,)), ((), ())),
                                                               preferred_element_type=F32)
        return carry

    lax.fori_loop(0, nsub, step, 0)

    @pl.when(t == pl.num_programs(2) - 1)
    def _():
        for hd in range(A_HEADS):
            sfin_ref[0, 0, hd] = st_scr[hd].T


def hgrn_scan(proj, row0, n_seq, seq_len, lb, s0):
    w = A_WIDTH
    tl = SCAN_TILE
    nt = seq_len // tl
    t0 = row0 // tl
    tile = lambda dd, t: t + dd * (nt - 1 - 2 * t)
    tok = lambda colfn: pl.BlockSpec((tl, w), lambda b, dd, t: (t0 + b * nt + tile(dd, t), colfn(dd)))
    shared_s0 = s0.shape[0] == 1
    st_shape = (1, 1, A_HEADS, A_HEAD_DIM, A_HEAD_DIM)
    return pl.pallas_call(
        functools.partial(_hgrn_kernel, sub=HGRN_SUB),
        grid=(n_seq, 2, nt),
        in_specs=[tok(lambda dd: 0), tok(lambda dd: 1), tok(lambda dd: 2 + dd),
                  pl.BlockSpec((1, 1, w), lambda b, dd, t: (dd, 0, 0)),
                  pl.BlockSpec(st_shape, lambda b, dd, t: (0 if shared_s0 else b, dd, 0, 0, 0))],
        out_specs=[pl.BlockSpec((None, tl, w), lambda b, dd, t: (dd, b * nt + tile(dd, t), 0)),
                   pl.BlockSpec(st_shape, lambda b, dd, t: (b, dd, 0, 0, 0))],
        out_shape=[jax.ShapeDtypeStruct((2, n_seq * seq_len, w), F32),
                   jax.ShapeDtypeStruct((n_seq, 2, A_HEADS, A_HEAD_DIM, A_HEAD_DIM), F32)],
        scratch_shapes=[pltpu.VMEM((A_HEADS, A_HEAD_DIM, A_HEAD_DIM), F32)] + [pltpu.VMEM((tl, w), F32)] * 6
                       + [pltpu.VMEM((HGRN_SUB * HGRN_SUB, A_HEAD_DIM), F32)],
        compiler_params=_cparams(("parallel", "parallel", "arbitrary")),
        name="hgrn_scan",
    )(proj, proj, proj, lb.reshape(2, 1, w), s0)


def _exclusive_count(x):
    r, n = x.shape
    ji = lax.broadcasted_iota(jnp.int32, (LANES, LANES), 0)
    ii = lax.broadcasted_iota(jnp.int32, (LANES, LANES), 1)
    upper = jnp.where(ji < ii, 1.0, 0.0).astype(BF16)
    carry = jnp.zeros((r, 1), F32)
    outs = []
    for blk in range(n // LANES):
        xb = x[:, blk * LANES:(blk + 1) * LANES]
        outs.append(jnp.dot(xb.astype(BF16), upper, preferred_element_type=F32) + carry)
        carry = carry + jnp.sum(xb, axis=1, keepdims=True)
    return jnp.concatenate(outs, axis=1)


def _select_kernel(a_ref, pos_ref, start_ref, *, cap):
    a = a_ref[...]

    def count(mask):
        return jnp.sum(jnp.where(mask, 1.0, 0.0), axis=1, keepdims=True)

    def body(i, t):
        cand = t | lax.shift_left(jnp.int32(1), 30 - i)
        return jnp.where(count(a >= pltpu.bitcast(cand, F32)) >= cap, cand, t)

    t = lax.fori_loop(0, 31, body, jnp.zeros((a.shape[0], 1), jnp.int32))
    v = jnp.min(jnp.where(a >= pltpu.bitcast(t, F32), a, jnp.inf), axis=1, keepdims=True)

    def too_low(v):
        return jnp.max(count(a > v)) >= cap

    def step_up(v):
        nxt = jnp.min(jnp.where(a > v, a, jnp.inf), axis=1, keepdims=True)
        return jnp.where(count(a > v) >= cap, nxt, v)

    v = lax.while_loop(too_low, step_up, v)
    gt = jnp.where(a > v, 1.0, 0.0)
    eq = jnp.where(a == v, 1.0, 0.0)
    need = cap - jnp.sum(gt, axis=1, keepdims=True)
    sel = gt + eq * jnp.where(_exclusive_count(eq) < need, 1.0, 0.0)
    pos = _exclusive_count(sel)
    pos_ref[...] = jnp.where(sel > 0.0, pos, -1.0).astype(jnp.int32)
    n_chunks = a.shape[1] // ROUTE_CHUNK
    lane = lax.broadcasted_iota(jnp.int32, start_ref.shape, 1)
    marks = jnp.where(lane == n_chunks, float(cap), 0.0)
    for c in range(n_chunks):
        marks = jnp.where(lane == c, pos[:, c * ROUTE_CHUNK:c * ROUTE_CHUNK + 1], marks)
    start_ref[...] = marks.astype(jnp.int32)


def route_select(aff_rows, cap):
    r, n = aff_rows.shape
    rb = min(r, LANES)
    marks = LANES
    assert n // ROUTE_CHUNK < marks
    return pl.pallas_call(
        functools.partial(_select_kernel, cap=cap),
        grid=(r // rb,),
        in_specs=[pl.BlockSpec((rb, n), lambda i: (i, 0))],
        out_specs=[pl.BlockSpec((rb, n), lambda i: (i, 0)), pl.BlockSpec((rb, marks), lambda i: (i, 0))],
        out_shape=[jax.ShapeDtypeStruct((r, n), jnp.int32), jax.ShapeDtypeStruct((r, marks), jnp.int32)],
        compiler_params=_cparams(("parallel",)),
        name="route_select",
    )(aff_rows)


def _gather_ctx_kernel(pos_ref, u_ref, o_ref):
    e, n = pos_ref.shape
    cap = o_ref.shape[1]
    slot = lax.broadcasted_iota(jnp.int32, (cap, n), 0)
    onehot = jnp.concatenate([jnp.where(pos_ref[ei:ei + 1, :] == slot, 1.0, 0.0).astype(BF16) for ei in range(e)],
                             axis=0)
    x = jnp.dot(onehot, u_ref[...], preferred_element_type=F32)
    for ei in range(e):
        o_ref[ei] = x[ei * cap:(ei + 1) * cap].astype(BF16)


def _gather_lat_kernel(start_ref, pos_ref, u_ref, o_ref, acc_scr, *, n_exp):
    cap = o_ref.shape[1]
    n = u_ref.shape[0]
    row = pl.program_id(0) * n_exp + pl.program_id(1)
    acc_scr[...] = jnp.zeros(acc_scr.shape, F32)
    slot = lax.broadcasted_iota(jnp.int32, (ROUTE_WINDOW, ROUTE_CHUNK), 0)
    for c in range(n // ROUTE_CHUNK):
        tok = slice(c * ROUTE_CHUNK, (c + 1) * ROUTE_CHUNK)
        lo = start_ref[row, c]
        hi = start_ref[row, c + 1]
        first = (lo // SLOT_ALIGN) * SLOT_ALIGN

        def window(j, carry, tok=tok, first=first):
            w0 = pl.multiple_of(first + j * ROUTE_WINDOW, SLOT_ALIGN)
            onehot = jnp.where(pos_ref[0, :, tok] == slot + w0, 1.0, 0.0).astype(BF16)
            acc_scr[pl.ds(w0, ROUTE_WINDOW), :] += jnp.dot(onehot, u_ref[tok, :], preferred_element_type=F32)
            return carry

        lax.fori_loop(0, (hi - first + ROUTE_WINDOW - 1) // ROUTE_WINDOW, window, 0)
    o_ref[0] = acc_scr[0:cap, :].astype(BF16)


def route_gather(pos_ctx, pos_lat, start_lat, u2, nb, seq, nl, lseq, n_exp):
    d = u2.shape[1]
    cap_c = CAP_FACTOR * seq // n_exp
    cap_l = CAP_FACTOR * lseq // n_exp
    xs_c = pl.pallas_call(
        _gather_ctx_kernel,
        grid=(nb,),
        in_specs=[pl.BlockSpec((n_exp, seq), lambda b: (b, 0)), pl.BlockSpec((seq, d), lambda b: (b, 0))],
        out_specs=pl.BlockSpec((n_exp, cap_c, d), lambda b: (0, b, 0)),
        out_shape=jax.ShapeDtypeStruct((n_exp, nb * cap_c, d), BF16),
        compiler_params=_cparams(("parallel",)),
        name="route_gather_ctx",
    )(pos_ctx, u2)
    lat_blk0 = nb * seq // lseq
    xs_l = pl.pallas_call(
        functools.partial(_gather_lat_kernel, n_exp=n_exp),
        grid=(nl, n_exp),
        in_specs=[pl.BlockSpec(memory_space=pltpu.SMEM),
                  pl.BlockSpec((1, 1, lseq), lambda b, ei: (b * n_exp + ei, 0, 0)),
                  pl.BlockSpec((lseq, d), lambda b, ei: (lat_blk0 + b, 0))],
        out_specs=pl.BlockSpec((1, cap_l, d), lambda b, ei: (ei, b, 0)),
        out_shape=jax.ShapeDtypeStruct((n_exp, nl * cap_l, d), BF16),
        scratch_shapes=[pltpu.VMEM((cap_l + ROUTE_WINDOW, d), F32)],
        compiler_params=_cparams(("parallel", "arbitrary")),
        name="route_gather_lat",
    )(start_lat, pos_lat.reshape(nl * n_exp, 1, lseq), u2)
    return xs_c, xs_l


def _combine_kernel(start_ref, pos_ref, gate_ref, yc_ref, yl_ref, h_ref, mod_ref, lng_ref, lnb_ref, o_ref, acc_scr, *,
                    ctx_tiles, chunks_per_seq):
    tn, e = pos_ref.shape
    d = h_ref.shape[1]

    def finish(moe):
        gate2 = mod_ref[0, 5:6, :]
        o_ref[...] = _layer_norm_rows(DN_ALPHA * h_ref[...] + gate2 * moe, lng_ref[...], lnb_ref[...])

    @pl.when(pl.program_id(0) < ctx_tiles)
    def _():
        cap = yc_ref.shape[1]
        lane = lax.broadcasted_iota(jnp.int32, (e, e * cap), 1)
        owner = lax.broadcasted_iota(jnp.int32, (e, e * cap), 0)
        expand = jnp.where(lane // cap == owner, 1.0, 0.0).astype(BF16)
        posx = jnp.dot(pos_ref[...].astype(F32).astype(BF16), expand, preferred_element_type=F32)
        gatex = sum(jnp.dot(part, expand, preferred_element_type=F32) for part in _split3(gate_ref[...]))
        slot = (lax.broadcasted_iota(jnp.int32, (tn, e * cap), 1) % cap).astype(F32)
        w = jnp.where(posx == slot, gatex, 0.0)
        w_hi = w.astype(BF16)
        w_lo = (w - w_hi.astype(F32)).astype(BF16)
        y = yc_ref[...].reshape(e * cap, d)
        finish(jnp.dot(w_hi, y, preferred_element_type=F32) + jnp.dot(w_lo, y, preferred_element_type=F32))

    @pl.when(pl.program_id(0) >= ctx_tiles)
    def _():
        cap = yl_ref.shape[1]
        j = pl.program_id(0) - ctx_tiles
        seq_i = j // chunks_per_seq
        c = j % chunks_per_seq
        lane = lax.broadcasted_iota(jnp.int32, (tn, ROUTE_WINDOW), 1)
        acc_scr[...] = jnp.zeros(acc_scr.shape, F32)
        for ei in range(e):
            row = seq_i * e + ei
            lo = start_ref[row, c]
            hi = start_ref[row, c + 1]
            first = (lo // SLOT_ALIGN) * SLOT_ALIGN

            def window(jw, carry, ei=ei, first=first):
                new_from = first + jw * ROUTE_WINDOW
                w0 = pl.multiple_of(jnp.minimum(new_from, cap - ROUTE_WINDOW), SLOT_ALIGN)
                p = pos_ref[:, ei:ei + 1]
                p = jnp.where(p >= new_from, p, -1)
                onehot = jnp.where(p == lane + w0, 1.0, 0.0).astype(BF16)
                acc_scr[...] += gate_ref[:, ei:ei + 1] * jnp.dot(onehot, yl_ref[ei, pl.ds(w0, ROUTE_WINDOW), :],
                                                                 preferred_element_type=F32)
                return carry

            lax.fori_loop(0, (hi - first + ROUTE_WINDOW - 1) // ROUTE_WINDOW, window, 0)
        finish(acc_scr[...])


def route_combine(pos_t, gate_t, start_lat, ys, h, mod, ln_g, ln_b, nb, seq, nl, lseq):
    m, d = h.shape
    e = pos_t.shape[1]
    tn = seq
    assert tn == ROUTE_CHUNK
    cap_c = CAP_FACTOR * seq // e
    cap_l = CAP_FACTOR * lseq // e
    lat_blk0 = nb * cap_c // cap_l
    per_seq = lseq // tn
    cidx = functools.partial(_cond_index, tile_rows=tn, n_ctx_rows=nb * seq, lat_seq=lseq)
    row = lambda i: (i, 0)
    fixed = lambda i: (0, 0)
    return pl.pallas_call(
        functools.partial(_combine_kernel, ctx_tiles=nb, chunks_per_seq=per_seq),
        grid=(m // tn,),
        in_specs=[pl.BlockSpec(memory_space=pltpu.SMEM), pl.BlockSpec((tn, e), row), pl.BlockSpec((tn, e), row),
                  pl.BlockSpec((e, cap_c, d), lambda i: (0, jnp.minimum(i, nb - 1), 0)),
                  pl.BlockSpec((e, cap_l, d), lambda i: (0, lat_blk0 + jnp.maximum(i - nb, 0) // per_seq, 0)),
                  pl.BlockSpec((tn, d), row),
                  pl.BlockSpec((1, 6, d), lambda i: (cidx(i), 0, 0)),
                  pl.BlockSpec((1, d), fixed), pl.BlockSpec((1, d), fixed)],
        out_specs=pl.BlockSpec((tn, d), row),
        out_shape=jax.ShapeDtypeStruct((m, d), F32),
        scratch_shapes=[pltpu.VMEM((tn, d), F32)],
        compiler_params=_cparams(("arbitrary",)),
        name="route_combine",
    )(start_lat, pos_t, gate_t, ys, ys, h, mod, ln_g.reshape(1, d), ln_b.reshape(1, d))


def _expert_kernel(xc_ref, xl_ref, w1_ref, w3_ref, w2_ref, o_ref, *, ctx_tiles):
    x = jnp.where(pl.program_id(1) < ctx_tiles, xc_ref[0], xl_ref[0])
    a = jnp.dot(x, w1_ref[0].astype(BF16), preferred_element_type=F32)
    g = jnp.dot(x, w3_ref[0].astype(BF16), preferred_element_type=F32)
    hid = (a * jax.nn.sigmoid(a) * g).astype(BF16)
    o_ref[0] = jnp.dot(hid, w2_ref[0].astype(BF16), preferred_element_type=F32).astype(BF16)


def expert_ffn(xs_c, xs_l, w1, w3, w2, layer):
    e, mc, d = xs_c.shape
    ml = xs_l.shape[1]
    f = w1.shape[3]
    tm = ROW_TILE
    ct, lt = mc // tm, ml // tm
    return pl.pallas_call(
        functools.partial(_expert_kernel, ctx_tiles=ct),
        grid=(e, ct + lt),
        in_specs=[pl.BlockSpec((1, tm, d), lambda ei, i: (ei, jnp.minimum(i, ct - 1), 0)),
                  pl.BlockSpec((1, tm, d), lambda ei, i: (ei, jnp.maximum(i - ct, 0), 0)),
                  pl.BlockSpec((None, 1, d, f), lambda ei, i: (layer, ei, 0, 0)),
                  pl.BlockSpec((None, 1, d, f), lambda ei, i: (layer, ei, 0, 0)),
                  pl.BlockSpec((None, 1, f, d), lambda ei, i: (layer, ei, 0, 0))],
        out_specs=pl.BlockSpec((1, tm, d), lambda ei, i: (ei, i, 0)),
        out_shape=jax.ShapeDtypeStruct((e, mc + ml, d), BF16),
        compiler_params=_cparams(("parallel", "arbitrary")),
        name="expert_ffn",
    )(xs_c, xs_l, w1, w3, w2)


def moe_layer(aff_t, u2, h, mod, w1, w3, w2, layer, ln_g, ln_b, nb, seq, nl, lseq):
    e = aff_t.shape[0]
    n_ctx = nb * seq
    seq_rows = lambda a, n, length: a.reshape(e, n, length).transpose(1, 0, 2).reshape(n * e, length)
    pos_c, _ = route_select(seq_rows(aff_t[:, :n_ctx], nb, seq), CAP_FACTOR * seq // e)
    pos_l, start_l = route_select(seq_rows(aff_t[:, n_ctx:], nl, lseq), CAP_FACTOR * lseq // e)
    xs_c, xs_l = route_gather(pos_c, pos_l, start_l, u2, nb, seq, nl, lseq, e)
    ys = expert_ffn(xs_c, xs_l, w1, w3, w2, layer)
    tok_rows = lambda p, n, length: p.reshape(n, e, length).transpose(0, 2, 1).reshape(n * length, e)
    pos_t = jnp.concatenate([tok_rows(pos_c, nb, seq), tok_rows(pos_l, nl, lseq)], axis=0)
    return route_combine(pos_t, aff_t.T, start_l, ys, h, mod, ln_g, ln_b, nb, seq, nl, lseq)


def _dot_split(a, b, passes=3):
    ah, bh = a.astype(BF16), b.astype(BF16)
    out = jnp.dot(ah, bh, preferred_element_type=F32)
    if passes >= 3:
        al = (a - ah.astype(F32)).astype(BF16)
        bl = (b - bh.astype(F32)).astype(BF16)
        out = out + jnp.dot(ah, bl, preferred_element_type=F32) + jnp.dot(al, bh, preferred_element_type=F32)
    return out


def _short_conv_kernel(x_ref, before_ref, after_ref, w_ref, b_ref, o_ref, *, ctx_tiles, ctx_seq, lat_seq):
    rows = x_ref.shape[0]
    x = x_ref[...]
    seq = jnp.where(pl.program_id(0) < ctx_tiles, ctx_seq, lat_seq)
    row = lax.broadcasted_iota(jnp.int32, x.shape, 0)
    pos = (pl.program_id(0) * rows + row) % seq
    prev = jnp.where(row == 0, before_ref[SUBLANES - 1:SUBLANES, :], pltpu.roll(x, 1, axis=0))
    nxt = jnp.where(row == rows - 1, after_ref[0:1, :], pltpu.roll(x, rows - 1, axis=0))
    prev = jnp.where(pos == 0, 0.0, prev)
    nxt = jnp.where(pos == seq - 1, 0.0, nxt)
    o_ref[...] = prev * w_ref[0:1, :] + x * w_ref[1:2, :] + nxt * w_ref[2:3, :] + b_ref[...]


def hyena_short_conv(proj, conv_w, conv_b, n_ctx_rows, ctx_seq, lat_seq):
    m = proj.shape[0]
    rb = 2 * ROW_TILE
    col0 = 5 * A_WIDTH // B_WIDTH
    halo = rb // SUBLANES
    last = m // SUBLANES - 1
    return pl.pallas_call(
        functools.partial(_short_conv_kernel, ctx_tiles=n_ctx_rows // rb, ctx_seq=ctx_seq, lat_seq=lat_seq),
        grid=(m // rb, 3),
        in_specs=[pl.BlockSpec((rb, B_WIDTH), lambda i, j: (i, col0 + j)),
                  pl.BlockSpec((SUBLANES, B_WIDTH), lambda i, j: (jnp.maximum(i * halo - 1, 0), col0 + j)),
                  pl.BlockSpec((SUBLANES, B_WIDTH), lambda i, j: (jnp.minimum((i + 1) * halo, last), col0 + j)),
                  pl.BlockSpec((SHORT_CONV, B_WIDTH), lambda i, j: (0, j)),
                  pl.BlockSpec((1, B_WIDTH), lambda i, j: (0, j))],
        out_specs=pl.BlockSpec((None, rb, B_WIDTH), lambda i, j: (j, i, 0)),
        out_shape=jax.ShapeDtypeStruct((3, m, B_WIDTH), F32),
        compiler_params=_cparams(("parallel", "parallel")),
        name="hyena_short_conv",
    )(proj, proj, proj, conv_w, conv_b.reshape(1, -1))


def _filter_kernel(z_ref, meta_ref, w1_ref, b1_ref, w2_ref, b2_ref, w3_ref, f_ref, dl_ref, o_ref):
    hdn = jnp.sin(f_ref[0:1, :] * (_dot_split(z_ref[...], w1_ref[...]) + b1_ref[...]))
    hdn = jnp.sin(f_ref[1:2, :] * (_dot_split(hdn, w2_ref[...]) + b2_ref[...]))
    filt = _dot_split(hdn, w3_ref[...])
    t = meta_ref[:, 0:1]
    fwd = meta_ref[:, 1:2]
    bwd = meta_ref[:, 2:3]
    win = jnp.exp(-t * dl_ref[...])
    for o in range(HYENA_ORDER):
        hf = filt[:, (2 * o) * B_WIDTH:(2 * o + 1) * B_WIDTH]
        hb = filt[:, (2 * o + 1) * B_WIDTH:(2 * o + 2) * B_WIDTH]
        o_ref[o] = (fwd * hf + bwd * hb) * win


def hyena_filter_taps(L, w1, b1, w2, b2, w3, freq):
    n = 2 * L
    t = jnp.linspace(0.0, 1.0, L, dtype=F32)[:, None]
    bands = (FILTER_EMB - 1) // 2
    ang = (2.0 * math.pi / L) * jnp.arange(L, dtype=F32)[:, None] * jnp.linspace(1e-4, bands - 1, bands, dtype=F32)[None, :]
    z = jnp.concatenate([t, jnp.cos(ang), -jnp.sin(ang)], axis=-1)
    deltas = jnp.abs(jnp.linspace(math.log(DECAY_TARGET) / SLOW_DECAY, math.log(DECAY_TARGET) / FAST_DECAY, B_WIDTH,
                                  dtype=F32)).reshape(1, B_WIDTH)
    lag = np.concatenate([np.arange(L), [0], np.arange(L - 1, 0, -1)])
    side = np.zeros((n, 7), np.float32)
    side[:L, 0] = 1.0
    side[L + 1:, 1] = 1.0
    meta = jnp.concatenate([t[lag], jnp.asarray(side)], axis=1)
    tr = min(n, 512)
    hid = w1.shape[1]
    fixed = lambda i: (0, 0)
    emb = LANES
    z = jnp.pad(z[lag], ((0, 0), (0, emb - FILTER_EMB)))
    w1 = jnp.pad(w1, ((0, emb - FILTER_EMB), (0, 0)))
    return pl.pallas_call(
        _filter_kernel,
        grid=(n // tr,),
        in_specs=[pl.BlockSpec((tr, emb), lambda i: (i, 0)), pl.BlockSpec((tr, 8), lambda i: (i, 0)),
                  pl.BlockSpec((emb, hid), fixed), pl.BlockSpec((1, hid), fixed),
                  pl.BlockSpec((hid, hid), fixed), pl.BlockSpec((1, hid), fixed),
                  pl.BlockSpec((hid, HYENA_ORDER * 2 * B_WIDTH), fixed), pl.BlockSpec((2, hid), fixed),
                  pl.BlockSpec((1, B_WIDTH), fixed)],
        out_specs=pl.BlockSpec((HYENA_ORDER, tr, B_WIDTH), lambda i: (0, i, 0)),
        out_shape=jax.ShapeDtypeStruct((HYENA_ORDER, n, B_WIDTH), F32),
        compiler_params=_cparams(("parallel",)),
        name="hyena_filter_taps",
    )(z, meta, w1, b1.reshape(1, hid), w2, b2.reshape(1, hid), w3, freq, deltas)


class _FftPlan:
    def __init__(self, L, minor):
        n = 2 * L
        n1 = n // minor
        h1, k1 = n1 // 2, n1 // 2 + 1
        self.L, self.minor, self.n1, self.h1, self.k1 = L, minor, n1, h1, k1
        th = 2.0 * np.pi / n1
        ph = th * ((np.arange(k1)[:, None] * np.arange(n1)[None, :]) % n1)
        fwd = np.zeros((2 * k1, n1))
        fwd[0::2], fwd[1::2] = np.cos(ph), -np.sin(ph)
        self.first_full = self._two_f32(fwd)
        self.first_half = self._two_f32(fwd[:, :h1])
        wgt = np.where((np.arange(k1) == 0) | (np.arange(k1) == n1 // 2), 1.0, 2.0) / n
        ph = th * ((np.arange(h1)[:, None] * np.arange(k1)[None, :]) % n1)
        inv = np.zeros((h1, 2 * k1))
        inv[:, 0::2], inv[:, 1::2] = wgt * np.cos(ph), -wgt * np.sin(ph)
        self.last = self._two_f32(inv)
        k = np.arange(k1)[:, None, None] + n1 * np.arange(minor)[None, :, None]
        ph = 2.0 * np.pi * ((k * np.arange(minor)[None, None, :]) % n) / n
        c, s = np.cos(ph), np.sin(ph)
        mf = np.concatenate([np.concatenate([c, s], axis=2), np.concatenate([-s, c], axis=2)], axis=1)
        self.mid_fwd = self._hi_lo(mf)
        self.mid_inv = self._hi_lo(np.swapaxes(mf, 1, 2))

    @staticmethod
    def _two_f32(table):
        hi = table.astype(np.float32)
        lo = (table - hi.astype(np.float64)).astype(np.float32)
        return jnp.asarray(np.concatenate([hi.reshape(-1), lo.reshape(-1)]))

    @staticmethod
    def _hi_lo(mat):
        bits = mat.astype(np.float32).view(np.uint32)
        bits = (bits + 0x7FFF + ((bits >> 16) & 1)) & np.uint32(0xFFFF0000)
        hi = bits.view(np.float32)
        lo = (mat - hi.astype(np.float64)).astype(np.float32)
        return jnp.asarray(hi).astype(BF16), jnp.asarray(lo)


def _dot_hi_lo(m_hi, m_rest, x):
    m_lo = m_rest.astype(BF16)
    x_hi = x.astype(BF16)
    x_lo = (x - x_hi.astype(F32)).astype(BF16)
    return (jnp.dot(m_hi, x_hi, preferred_element_type=F32) + jnp.dot(m_hi, x_lo, preferred_element_type=F32)
            + jnp.dot(m_lo, x_hi, preferred_element_type=F32))


def _coef(tab_ref, i, size):
    return tab_ref[i] + tab_ref[size + i]


def _outer_forward(cf_ref, k, n_k, n_slabs, slab):
    size = 2 * n_k * n_slabs
    ar = ai = None
    for n1 in range(n_slabs):
        xs = slab(n1)
        tr = _coef(cf_ref, (2 * k) * n_slabs + n1, size) * xs
        ti = _coef(cf_ref, (2 * k + 1) * n_slabs + n1, size) * xs
        ar, ai = (tr, ti) if ar is None else (ar + tr, ai + ti)
    return jnp.concatenate([ar, ai], axis=0)


def _hyena_conv_kernel(cf_ref, ci_ref, u_ref, g_ref, mfh_ref, mfl_ref, mih_ref, mil_ref, k_ref, d_ref, o_ref, *,
                       minor, h1, n_k, n_seq, seq_len):
    k = pl.program_id(2)
    ct = u_ref.shape[1]

    def slab(n1):
        parts = [u_ref[g * seq_len + n1 * minor:g * seq_len + (n1 + 1) * minor, :] for g in range(n_seq)]
        return parts[0] if n_seq == 1 else jnp.concatenate(parts, axis=1)

    @pl.when(k == 0)
    def _():
        o_ref[...] = jnp.zeros(o_ref.shape, F32)

    x = _dot_hi_lo(mfh_ref[0], mfl_ref[0], _outer_forward(cf_ref, k, n_k, h1, slab))
    xr, xi = x[:minor], x[minor:]
    kr = k_ref[0, :minor, :]
    ki = k_ref[0, minor:, :]
    if n_seq > 1:
        kr = jnp.concatenate([kr] * n_seq, axis=1)
        ki = jnp.concatenate([ki] * n_seq, axis=1)
    y = jnp.concatenate([xr * kr - xi * ki, xr * ki + xi * kr], axis=0)
    gm = _dot_hi_lo(mih_ref[0], mil_ref[0], y)
    gr, gi = gm[:minor], gm[minor:]
    for n1 in range(h1):
        base = n1 * 2 * n_k + 2 * k
        part = _coef(ci_ref, base, 2 * n_k * h1) * gr + _coef(ci_ref, base + 1, 2 * n_k * h1) * gi
        for g in range(n_seq):
            rows = slice(g * seq_len + n1 * minor, g * seq_len + (n1 + 1) * minor)
            o_ref[rows, :] += part[:, g * ct:(g + 1) * ct]

    @pl.when(k == n_k - 1)
    def _():
        o_ref[...] = g_ref[...] * (o_ref[...] + d_ref[...] * u_ref[...])


def _hyena_spectrum_kernel(cf_ref, t_ref, mfh_ref, mfl_ref, o_ref, *, minor, n1, n_k):
    k = pl.program_id(1)
    a = _outer_forward(cf_ref, k, n_k, n1, lambda j: t_ref[j * minor:(j + 1) * minor, :])
    o_ref[0] = _dot_hi_lo(mfh_ref[0], mfl_ref[0], a)


def _hyena_spectrum(plan, taps, ct):
    n, w = taps.shape
    m2 = 2 * plan.minor
    mat = pl.BlockSpec((1, m2, m2), lambda c, k: (k, 0, 0))
    return pl.pallas_call(
        functools.partial(_hyena_spectrum_kernel, minor=plan.minor, n1=plan.n1, n_k=plan.k1),
        grid=(w // ct, plan.k1),
        in_specs=[pl.BlockSpec(memory_space=pltpu.SMEM), pl.BlockSpec((n, ct), lambda c, k: (0, c)), mat, mat],
        out_specs=pl.BlockSpec((1, m2, ct), lambda c, k: (k, 0, c)),
        out_shape=jax.ShapeDtypeStruct((plan.k1, m2, w), F32),
        compiler_params=_cparams(("parallel", "parallel")),
        name="hyena_tap_spectrum",
    )(plan.first_full, taps, *plan.mid_fwd)


def _hyena_conv(plan, n_blocks, n_seq, ct, u, u_plane, u_blk0, gate, gate_plane, g_blk0, spec, bias):
    w = u.shape[-1]
    rows = n_seq * plan.L
    m2 = 2 * plan.minor
    tok = lambda plane, blk0: pl.BlockSpec((None, rows, ct), lambda b, c, k: (plane, blk0 + b, c))
    mat = pl.BlockSpec((1, m2, m2), lambda b, c, k: (k, 0, 0))
    smem = pl.BlockSpec(memory_space=pltpu.SMEM)
    return pl.pallas_call(
        functools.partial(_hyena_conv_kernel, minor=plan.minor, h1=plan.h1, n_k=plan.k1, n_seq=n_seq,
                          seq_len=plan.L),
        grid=(n_blocks, w // ct, plan.k1),
        in_specs=[smem, smem, tok(u_plane, u_blk0), tok(gate_plane, g_blk0), mat, mat, mat, mat,
                  pl.BlockSpec((1, m2, ct), lambda b, c, k: (k, 0, c)),
                  pl.BlockSpec((1, ct), lambda b, c, k: (0, c))],
        out_specs=pl.BlockSpec((None, rows, ct), lambda b, c, k: (0, b, c)),
        out_shape=jax.ShapeDtypeStruct((1, n_blocks * rows, w), F32),
        compiler_params=_cparams(("parallel", "parallel", "arbitrary")),
        name="hyena_long_conv",
    )(plan.first_half, plan.last, u, gate, *plan.mid_fwd, *plan.mid_inv, spec,
      bias.reshape(1, w))


def hyena_mixer(proj, n_ctx_rows, ctx_seq, lat_seq, conv_w, conv_b, fw1, fb1, fw2, fb2, fw3, freq, fbias):
    m = proj.shape[0]
    sc = hyena_short_conv(proj, conv_w, conv_b, n_ctx_rows, ctx_seq, lat_seq)
    outs = []
    for L, row0, n_rows, minor, n_seq, ct in ((ctx_seq, 0, n_ctx_rows, 128, 4, B_WIDTH),
                                              (lat_seq, n_ctx_rows, m - n_ctx_rows, 256, 1, B_WIDTH // 2)):
        plan = _FftPlan(L, minor)
        n_blocks = n_rows // (n_seq * L)
        blk0 = row0 // (n_seq * L)
        taps = hyena_filter_taps(L, fw1, fb1, fw2, fb2, fw3, freq)
        z, z_plane, z_blk0 = sc, 0, blk0
        for o in range(HYENA_ORDER):
            spec = _hyena_spectrum(plan, taps[o], ct)
            z = _hyena_conv(plan, n_blocks, n_seq, ct, z, z_plane, z_blk0, sc, 1 + o, blk0, spec, fbias[o])
            z_plane, z_blk0 = 0, 0
        outs.append(z)
    return outs


def _rope_tables(seq_len):
    n_rows = seq_len // GRID_W
    rows = jnp.repeat(jnp.arange(n_rows), GRID_W)
    cols = jnp.tile(jnp.arange(GRID_W), n_rows)
    quarter = C_HEAD_DIM // 4
    inv = ROPE_BASE ** (-jnp.arange(quarter, dtype=F32) / quarter)
    cos_parts, sin_parts = [], []
    for pos in (rows, cols):
        a = pos.astype(F32)[:, None] * inv
        cos_parts += [jnp.cos(a), jnp.cos(a)]
        sin_parts += [-jnp.sin(a), jnp.sin(a)]
    return jnp.concatenate(cos_parts, axis=-1), jnp.concatenate(sin_parts, axis=-1)


def kernel(x_prompt, x_sample, state_hgrn, state_ret, c, c_ctx, ada_w, ada_b, ln_g, ln_b, even_w_in, even_w_out, hgrn_lb, hgrn_norm_w, hyena_conv_w, hyena_conv_b, hyena_w1, hyena_b1, hyena_w2, hyena_b2, hyena_w3, hyena_freq, hyena_bias, odd_w_in, odd_w_out, ret_decay, moe_router, moe_w1, moe_w3, moe_w2):
    nb, seq, d = x_prompt.shape
    nl, lseq, _ = x_sample.shape
    n_ctx = nb * seq
    n_lat = nl * lseq
    h = jnp.concatenate([x_prompt.reshape(n_ctx, d), x_sample.reshape(n_lat, d)], axis=0)

    cond = jnp.concatenate([c_ctx[None, :], c], axis=0)
    mods = modulation_table(cond, ada_w, ada_b).reshape(DEPTH, 1 + nl, 6, d)

    lb_soft = jax.nn.softmax(hgrn_lb.astype(F32), axis=0)
    lb_all = jnp.cumsum(lb_soft, axis=0) - lb_soft[0]
    rope_tabs = _rope_tables(lseq)
    zero_hgrn = jnp.zeros((1, 2, A_HEADS, A_HEAD_DIM, A_HEAD_DIM), F32)
    zero_ret = jnp.zeros((1, 2, C_HEADS, C_HEAD_DIM, C_HEAD_DIM), F32)

    hgrn_states, ret_states = [], []
    for l in range(DEPTH):
        j = l // 2
        mod = mods[l]
        if l % 2 == 0:
            proj = in_projection(h, mod, even_w_in, j, n_ctx, lseq)
            o_p, s_p = hgrn_scan(proj, 0, nb, seq, lb_all[j], zero_hgrn)
            o_l, _ = hgrn_scan(proj, n_ctx, nl, lseq, lb_all[j], state_hgrn[:, j])
            hgrn_states.append(s_p)
            yb_p, yb_l = hyena_mixer(proj, n_ctx, seq, lseq, hyena_conv_w[j], hyena_conv_b[j], hyena_w1[j],
                                     hyena_b1[j], hyena_w2[j], hyena_b2[j], hyena_w3[j], hyena_freq[j], hyena_bias[j])
            h, u2, aff_t = post_mixer(True, o_p, o_l, proj, (hgrn_norm_w[j], yb_p, yb_l), h, mod, even_w_out, j,
                                      ln_g[l, 0], ln_b[l, 0], moe_router[l], n_ctx, lseq)
        else:
            proj = in_projection(h, mod, odd_w_in, j, n_ctx, lseq)
            log_gamma = jax.nn.log_sigmoid(ret_decay[j].astype(F32))
            o_p, s_p = retention_scan(proj, 0, nb, seq, log_gamma, zero_ret, None)
            o_l, _ = retention_scan(proj, n_ctx, nl, lseq, log_gamma, state_ret[:, j], rope_tabs)
            ret_states.append(s_p)
            h, u2, aff_t = post_mixer(False, o_p, o_l, proj, None, h, mod, odd_w_out, j,
                                      ln_g[l, 0], ln_b[l, 0], moe_router[l], n_ctx, lseq)
        h = moe_layer(aff_t, u2, h, mod, moe_w1, moe_w3, moe_w2, l, ln_g[l, 1], ln_b[l, 1], nb, seq, nl, lseq)

    y_prompt = h[:n_ctx].reshape(nb, seq, d)
    y_sample = h[n_ctx:].reshape(nl, lseq, d)
    new_state_hgrn = jnp.stack(hgrn_states, axis=1)
    new_state_ret = jnp.stack(ret_states, axis=1)
    return (y_prompt, y_sample, new_state_hgrn, new_state_ret)
```

```python
import functools
import math

import jax
import jax.numpy as jnp
import numpy as np
from jax import lax
from jax.experimental import pallas as pl
from jax.experimental.pallas import tpu as pltpu

F32 = jnp.float32
BF16 = jnp.bfloat16

D_MODEL = 1024
DEPTH = 4
GRID_W = 64
A_WIDTH = D_MODEL // 2
A_HEADS = 4
A_HEAD_DIM = A_WIDTH // A_HEADS
F_MIN = 1e-30
B_WIDTH = D_MODEL - A_WIDTH
HYENA_ORDER = 2
FILTER_EMB = 33
SHORT_CONV = 3
DECAY_TARGET = 1e-2
FAST_DECAY = 0.3
SLOW_DECAY = 1.5
C_HEADS = 4
C_HEAD_DIM = D_MODEL // C_HEADS
ROPE_BASE = 10000.0
N_EXPERTS = 16
CAP_FACTOR = 2
LN_EPS = 1e-5
DN_ALPHA = (2 * DEPTH) ** 0.25

LANES = 128
SUBLANES = 8
VMEM_LIMIT = 56 * 1024 * 1024
ROW_TILE = 512
SCAN_TILE = 256
HGRN_SUB = 16


def _cparams(sem):
    return pltpu.CompilerParams(dimension_semantics=sem, vmem_limit_bytes=VMEM_LIMIT)


def _cond_index(tile, tile_rows, n_ctx_rows, lat_seq):
    row = tile * tile_rows
    return jnp.where(row < n_ctx_rows, 0, 1 + (row - n_ctx_rows) // lat_seq)


def _mod_kernel(c_ref, w_ref, b_ref, o_ref):
    c = c_ref[...]
    s = c * jax.nn.sigmoid(c)
    o_ref[0] = jnp.dot(s.astype(BF16), w_ref[0].astype(BF16), preferred_element_type=F32) + b_ref[0]


def modulation_table(cond, ada_w, ada_b):
    r, d = cond.shape
    depth, _, n = ada_w.shape
    tn = 1024
    return pl.pallas_call(
        _mod_kernel,
        grid=(depth, n // tn),
        in_specs=[pl.BlockSpec((r, d), lambda l, j: (0, 0)),
                  pl.BlockSpec((1, d, tn), lambda l, j: (l, 0, j)),
                  pl.BlockSpec((1, 1, tn), lambda l, j: (l, 0, j))],
        out_specs=pl.BlockSpec((1, r, tn), lambda l, j: (l, 0, j)),
        out_shape=jax.ShapeDtypeStruct((depth, r, n), F32),
        compiler_params=_cparams(("parallel", "parallel")),
        name="modulation",
    )(cond, ada_w, ada_b.reshape(depth, 1, n))


def _inproj_kernel(x_ref, mod_ref, w_ref, o_ref):
    shift = mod_ref[0, 0:1, :]
    scale = mod_ref[0, 1:2, :]
    u = (x_ref[...] * (1.0 + scale) + shift).astype(BF16)
    o_ref[...] = jnp.dot(u, w_ref[...].astype(BF16), preferred_element_type=F32)


def in_projection(x, mod, w_stack, layer, n_ctx_rows, lat_seq):
    m, d = x.shape
    n = w_stack.shape[2]
    tm, tn = ROW_TILE, 2048
    cidx = functools.partial(_cond_index, tile_rows=tm, n_ctx_rows=n_ctx_rows, lat_seq=lat_seq)
    return pl.pallas_call(
        _inproj_kernel,
        grid=(n // tn, m // tm),
        in_specs=[pl.BlockSpec((tm, d), lambda j, i: (i, 0)),
                  pl.BlockSpec((1, 6, d), lambda j, i: (cidx(i), 0, 0)),
                  pl.BlockSpec((None, d, tn), lambda j, i: (layer, 0, j))],
        out_specs=pl.BlockSpec((tm, tn), lambda j, i: (i, j)),
        out_shape=jax.ShapeDtypeStruct((m, n), F32),
        compiler_params=_cparams(("parallel", "parallel")),
        name="in_projection",
    )(x, mod, w_stack)


def _layer_norm_rows(z, g, b):
    mu = jnp.mean(z, axis=-1, keepdims=True)
    zc = z - mu
    var = jnp.mean(zc * zc, axis=-1, keepdims=True)
    return zc * lax.rsqrt(var + LN_EPS) * g + b


def _post_mixer(y, h_ref, mod_ref, w_ref, lng_ref, lnb_ref, r_ref, h_out, u_out, aff_out):
    gate1 = mod_ref[0, 2:3, :]
    shift2 = mod_ref[0, 3:4, :]
    scale2 = mod_ref[0, 4:5, :]
    mix = jnp.dot(y.astype(BF16), w_ref[...].astype(BF16), preferred_element_type=F32)
    hn = _layer_norm_rows(DN_ALPHA * h_ref[...] + gate1 * mix, lng_ref[...], lnb_ref[...])
    h_out[...] = hn
    u2 = (hn * (1.0 + scale2) + shift2).astype(BF16)
    u_out[...] = u2
    logits = lax.dot_general(r_ref[...].astype(BF16), u2, (((1,), (1,)), ((), ())), preferred_element_type=F32)
    mx = jnp.max(logits, axis=0, keepdims=True)
    ex = jnp.exp(logits - mx)
    aff_out[...] = ex / jnp.sum(ex, axis=0, keepdims=True)


def _group_pick(ctx_tiles, ctx_refs, lat_refs):
    is_ctx = pl.program_id(0) < ctx_tiles
    return jnp.where(is_ctx, sum(r[...] for r in ctx_refs), sum(r[...] for r in lat_refs))


def _post_even_kernel(ofc_ref, obc_ref, ofl_ref, obl_ref, ybc_ref, ybl_ref, g_ref, nw_ref, h_ref, mod_ref, w_ref,
                      lng_ref, lnb_ref, r_ref, h_out, u_out, aff_out, *, ctx_tiles):
    o = _group_pick(ctx_tiles, (ofc_ref, obc_ref), (ofl_ref, obl_ref))
    g = g_ref[...]
    parts = []
    for hd in range(A_HEADS):
        sl = slice(hd * A_HEAD_DIM, (hd + 1) * A_HEAD_DIM)
        oh = o[:, sl]
        parts.append(oh * lax.rsqrt(jnp.mean(oh * oh, axis=-1, keepdims=True) + LN_EPS))
    ya = jnp.concatenate(parts, axis=-1) * nw_ref[...] * (g * jax.nn.sigmoid(g))
    y = jnp.concatenate([ya, _group_pick(ctx_tiles, (ybc_ref,), (ybl_ref,))], axis=-1)
    _post_mixer(y, h_ref, mod_ref, w_ref, lng_ref, lnb_ref, r_ref, h_out, u_out, aff_out)


def _post_odd_kernel(ofc_ref, obc_ref, ofl_ref, obl_ref, g_ref, h_ref, mod_ref, w_ref, lng_ref, lnb_ref, r_ref,
                     h_out, u_out, aff_out, *, ctx_tiles):
    o = _group_pick(ctx_tiles, (ofc_ref, obc_ref), (ofl_ref, obl_ref))
    g = g_ref[...]
    parts = []
    for hd in range(C_HEADS):
        sl = slice(hd * C_HEAD_DIM, (hd + 1) * C_HEAD_DIM)
        oh = o[:, sl]
        mu = jnp.mean(oh, axis=-1, keepdims=True)
        oc = oh - mu
        parts.append(oc * lax.rsqrt(jnp.mean(oc * oc, axis=-1, keepdims=True) + LN_EPS))
    y = jnp.concatenate(parts, axis=-1) * (g * jax.nn.sigmoid(g))
    _post_mixer(y, h_ref, mod_ref, w_ref, lng_ref, lnb_ref, r_ref, h_out, u_out, aff_out)


def post_mixer(even, scan_ctx, scan_lat, proj, extra, h, mod, w_out_stack, layer, ln_g, ln_b, router, n_ctx_rows,
               lat_seq):
    m, d = h.shape
    e = router.shape[1]
    tm = ROW_TILE
    ct = n_ctx_rows // tm
    cidx = functools.partial(_cond_index, tile_rows=tm, n_ctx_rows=n_ctx_rows, lat_seq=lat_seq)
    width = (scan_ctx[0] if isinstance(scan_ctx, tuple) else scan_ctx).shape[-1]
    row = lambda i: (i, 0)
    fixed = lambda i: (0, 0)
    ctx_row = lambda i: jnp.minimum(i, ct - 1)
    lat_row = lambda i: jnp.maximum(i - ct, 0)
    plane = lambda p, rowfn, w: pl.BlockSpec((None, tm, w), lambda i: (p, rowfn(i), 0))
    common_specs = [pl.BlockSpec((tm, d), row),
                    pl.BlockSpec((1, 6, d), lambda i: (cidx(i), 0, 0)),
                    pl.BlockSpec((None, d, d), lambda i: (layer, 0, 0)),
                    pl.BlockSpec((1, d), fixed),
                    pl.BlockSpec((1, d), fixed),
                    pl.BlockSpec((e, d), fixed)]
    common_args = [h, mod, w_out_stack, ln_g.reshape(1, d), ln_b.reshape(1, d), router.T]
    def directions(scan, rowfn):
        if isinstance(scan, tuple):
            return [pl.BlockSpec((tm, width), lambda i: (rowfn(i), 0))] * 2, list(scan)
        return [plane(0, rowfn, width), plane(1, rowfn, width)], [scan, scan]

    ctx_specs, ctx_args = directions(scan_ctx, ctx_row)
    lat_specs, lat_args = directions(scan_lat, lat_row)
    specs = ctx_specs + lat_specs
    args = ctx_args + lat_args
    if even:
        norm_w, yb_ctx, yb_lat = extra
        gate_col = 4 * A_WIDTH // width
        specs += [plane(0, ctx_row, B_WIDTH), plane(0, lat_row, B_WIDTH),
                  pl.BlockSpec((tm, width), lambda i: (i, gate_col)), pl.BlockSpec((1, width), fixed)]
        args += [yb_ctx, yb_lat, proj, norm_w.reshape(1, width)]
        body = _post_even_kernel
    else:
        gate_col = 3
        specs += [pl.BlockSpec((tm, width), lambda i: (i, gate_col))]
        args += [proj]
        body = _post_odd_kernel
    return pl.pallas_call(
        functools.partial(body, ctx_tiles=ct),
        grid=(m // tm,),
        in_specs=specs + common_specs,
        out_specs=[pl.BlockSpec((tm, d), row), pl.BlockSpec((tm, d), row), pl.BlockSpec((e, tm), lambda i: (0, i))],
        out_shape=[jax.ShapeDtypeStruct((m, d), F32), jax.ShapeDtypeStruct((m, d), BF16),
                   jax.ShapeDtypeStruct((e, m), F32)],
        compiler_params=_cparams(("arbitrary",)),
        name="post_mixer_even" if even else "post_mixer_odd",
    )(*(args + common_args))


def _rope_halves(x, cos, sin_signed):
    outs = []
    for p in range(2):
        sl = slice(p * LANES, (p + 1) * LANES)
        xp = x[:, sl]
        outs.append(xp * cos[:, sl] + pltpu.roll(xp, LANES // 2, axis=1) * sin_signed[:, sl])
    return jnp.concatenate(outs, axis=-1)


def _retention_kernel(lg_ref, q_ref, k_ref, v_ref, cos_ref, sin_ref, s0_ref, o_ref, sfin_ref, s_scr, *, rope):
    d = pl.program_id(1)
    t = pl.program_id(2)
    c = q_ref.shape[0]
    dk = C_HEAD_DIM

    @pl.when(t == 0)
    def _():
        s_scr[...] = s0_ref[0, 0]

    ti = lax.broadcasted_iota(jnp.int32, (c, c), 0)
    si = lax.broadcasted_iota(jnp.int32, (c, c), 1)
    diff = jnp.where(d == 0, ti - si, si - ti).astype(F32)
    jt = lax.broadcasted_iota(jnp.int32, (c, dk), 0)
    eq = jnp.where(d == 0, jt + 1, c - jt).astype(F32)
    ek = jnp.where(d == 0, c - 1 - jt, jt).astype(F32)
    for hd in range(C_HEADS):
        lg = lg_ref[d, hd]
        sl = slice(hd * dk, (hd + 1) * dk)
        q = q_ref[:, sl]
        k = k_ref[:, sl] * (dk ** -0.5)
        if rope:
            q = _rope_halves(q, cos_ref[...], sin_ref[...])
            k = _rope_halves(k, cos_ref[...], sin_ref[...])
        v = v_ref[:, sl].astype(BF16)
        rel = jnp.where(diff >= 0.0, jnp.exp(lg * jnp.maximum(diff, 0.0)), 0.0)
        att = lax.dot_general(q.astype(BF16), k.astype(BF16), (((1,), (1,)), ((), ())),
                              preferred_element_type=F32) * rel
        intra = jnp.dot(att.astype(BF16), v, preferred_element_type=F32)
        s_old = s_scr[hd]
        inter = jnp.dot((q * jnp.exp(lg * eq)).astype(BF16), s_old.astype(BF16), preferred_element_type=F32)
        o_ref[:, sl] = inter + intra
        kd = (k * jnp.exp(lg * ek)).astype(BF16)
        s_new = s_old * jnp.exp(lg * c) + lax.dot_general(kd, v, (((0,), (0,)), ((), ())),
                                                          preferred_element_type=F32)
        s_scr[hd] = s_new

    @pl.when(t == pl.num_programs(2) - 1)
    def _():
        sfin_ref[0, 0] = s_scr[...]


def retention_scan(proj, row0, n_seq, seq_len, log_gamma, s0, rope_tabs):
    d = D_MODEL
    c = SCAN_TILE
    nt = seq_len // c
    t0 = row0 // c
    rope = rope_tabs is not None
    if rope:
        cos, sin = rope_tabs
    else:
        cos = sin = jnp.zeros((c, C_HEAD_DIM), F32)
    tile = lambda dd, t: t + dd * (nt - 1 - 2 * t)
    tok = lambda col: pl.BlockSpec((c, d), lambda b, dd, t: (t0 + b * nt + tile(dd, t), col))
    rope_spec = pl.BlockSpec((c, C_HEAD_DIM), (lambda b, dd, t: (tile(dd, t), 0)) if rope else (lambda b, dd, t: (0, 0)))
    shared_s0 = s0.shape[0] == 1
    st_shape = (1, 1, C_HEADS, C_HEAD_DIM, C_HEAD_DIM)
    return pl.pallas_call(
        functools.partial(_retention_kernel, rope=rope),
        grid=(n_seq, 2, nt),
        in_specs=[pl.BlockSpec(memory_space=pltpu.SMEM), tok(0), tok(1), tok(2), rope_spec, rope_spec,
                  pl.BlockSpec(st_shape, lambda b, dd, t: (0 if shared_s0 else b, dd, 0, 0, 0))],
        out_specs=[pl.BlockSpec((None, c, d), lambda b, dd, t: (dd, b * nt + tile(dd, t), 0)),
                   pl.BlockSpec(st_shape, lambda b, dd, t: (b, dd, 0, 0, 0))],
        out_shape=[jax.ShapeDtypeStruct((2, n_seq * seq_len, d), F32),
                   jax.ShapeDtypeStruct((n_seq, 2, C_HEADS, C_HEAD_DIM, C_HEAD_DIM), F32)],
        scratch_shapes=[pltpu.VMEM((C_HEADS, C_HEAD_DIM, C_HEAD_DIM), F32)],
        compiler_params=_cparams(("parallel", "parallel", "arbitrary")),
        name="retention_scan",
    )(log_gamma, proj, proj, proj, cos, sin, s0)


def _split3(x):
    hi = x.astype(BF16)
    r1 = x - hi.astype(F32)
    mid = r1.astype(BF16)
    lo = (r1 - mid.astype(F32)).astype(BF16)
    return hi, mid, lo


def _hgrn_kernel(qf_ref, vf_ref, zf_ref, qb_ref, vb_ref, zb_ref, lb_ref, s0_ref, of_ref, ob_ref, sfin_ref,
                 st_scr, qa_scr, qe_scr, k_scr, b_scr, b2_scr, r_scr, bias_scr, *, sub):
    t = pl.program_id(1)
    streams = ((qf_ref, vf_ref, zf_ref, of_ref), (qb_ref, vb_ref, zb_ref, ob_ref))
    tl, w = qf_ref.shape
    dh = A_HEAD_DIM
    nsub = tl // sub
    log2e = 1.0 / math.log(2.0)

    @pl.when(t == 0)
    def _():
        for dirn in range(2):
            for hd in range(A_HEADS):
                st_scr[dirn, hd] = s0_ref[0, dirn, hd].T

    ti = lax.broadcasted_iota(jnp.int32, (tl, tl), 0)
    si = lax.broadcasted_iota(jnp.int32, (tl, tl), 1)
    pr = lax.broadcasted_iota(jnp.int32, (sub * sub, dh), 0)
    for dirn, (q_ref, _, z_ref, _) in enumerate(streams):
        sign = 1 if dirn == 0 else -1
        lb = lb_ref[dirn]
        z = z_ref[...]
        sg = jax.nn.sigmoid(z)
        lf = jnp.log(jnp.maximum(lb + (1.0 - lb) * sg, F_MIN))
        k = (1.0 - lb) * jax.nn.sigmoid(-z)
        k_scr[dirn] = k
        seen = ((ti - si) * sign >= 0) & ((ti // sub) == (si // sub))
        tri = jnp.where(seen, 1.0, 0.0).astype(BF16)
        b = sum(jnp.dot(tri, part, preferred_element_type=F32) for part in _split3(lf))
        b_scr[dirn] = b
        b2_scr[dirn] = b * log2e
        r_scr[dirn] = b * log2e - jnp.log2(k)
        q = q_ref[...]
        q = q * jax.nn.sigmoid(q)
        qa_scr[dirn] = q
        qe_scr[dirn] = q * jnp.exp(b)
        bias_scr[dirn] = jnp.where(((pr % sub) - (pr // sub)) * sign >= 0, 0.0, -1e30)

    ones = jnp.ones((dh, dh), BF16)

    def step(i, carry):
        for dirn, (_, v_ref, _, o_ref) in enumerate(streams):
            blk = i if dirn == 0 else nsub - 1 - i
            rows = pl.ds(pl.multiple_of(blk * sub, sub), sub)
            last = sub - 1 if dirn == 0 else 0
            for hd in range(A_HEADS):
                sl = slice(hd * dh, (hd + 1) * dh)
                qs = qa_scr[dirn, rows, sl]
                ks = k_scr[dirn, rows, sl]
                vs = v_ref[rows, sl]
                bs = b_scr[dirn, rows, sl]
                b2 = b2_scr[dirn, rows, sl]
                rs = r_scr[dirn, rows, sl]
                st = st_scr[dirn, hd]
                inter = lax.dot_general(qe_scr[dirn, rows, sl].astype(BF16), st.astype(BF16),
                                        (((1,), (1,)), ((), ())), preferred_element_type=F32)
                es = [(qs * jnp.exp2(b2 - rs[s:s + 1, :] + bias_scr[dirn, s * sub:(s + 1) * sub, :])).astype(BF16)
                      for s in range(sub)]
                att = jnp.dot(jnp.concatenate(es, axis=0), ones, preferred_element_type=F32)
                intra = att[0:sub] * vs[0:1, :]
                for s in range(1, sub):
                    intra = intra + att[s * sub:(s + 1) * sub] * vs[s:s + 1, :]
                o_ref[rows, sl] = inter + intra
                b_end = bs[last:last + 1, :]
                kd = (ks * jnp.exp(b_end - bs)).astype(BF16)
                st_scr[dirn, hd] = st * jnp.exp(b_end) + lax.dot_general(
                    vs.astype(BF16), kd, (((0,), (0,)), ((), ())), preferred_element_type=F32)
        return carry

    lax.fori_loop(0, nsub, step, 0)

    @pl.when(t == pl.num_programs(1) - 1)
    def _():
        for dirn in range(2):
            for hd in range(A_HEADS):
                sfin_ref[0, dirn, hd] = st_scr[dirn, hd].T


def hgrn_scan(proj, row0, n_seq, seq_len, lb, s0):
    w = A_WIDTH
    tl = SCAN_TILE
    nt = seq_len // tl
    t0 = row0 // tl
    fwd = lambda col: pl.BlockSpec((tl, w), lambda b, t: (t0 + b * nt + t, col))
    bwd = lambda col: pl.BlockSpec((tl, w), lambda b, t: (t0 + b * nt + nt - 1 - t, col))
    shared_s0 = s0.shape[0] == 1
    st_shape = (1, 2, A_HEADS, A_HEAD_DIM, A_HEAD_DIM)
    o_f, o_b, s_fin = pl.pallas_call(
        functools.partial(_hgrn_kernel, sub=HGRN_SUB),
        grid=(n_seq, nt),
        in_specs=[fwd(0), fwd(1), fwd(2), bwd(0), bwd(1), bwd(3),
                  pl.BlockSpec((2, 1, w), lambda b, t: (0, 0, 0)),
                  pl.BlockSpec(st_shape, lambda b, t: (0 if shared_s0 else b, 0, 0, 0, 0))],
        out_specs=[pl.BlockSpec((tl, w), lambda b, t: (b * nt + t, 0)),
                   pl.BlockSpec((tl, w), lambda b, t: (b * nt + nt - 1 - t, 0)),
                   pl.BlockSpec(st_shape, lambda b, t: (b, 0, 0, 0, 0))],
        out_shape=[jax.ShapeDtypeStruct((n_seq * seq_len, w), F32), jax.ShapeDtypeStruct((n_seq * seq_len, w), F32),
                   jax.ShapeDtypeStruct((n_seq, 2, A_HEADS, A_HEAD_DIM, A_HEAD_DIM), F32)],
        scratch_shapes=[pltpu.VMEM((2, A_HEADS, A_HEAD_DIM, A_HEAD_DIM), F32)] + [pltpu.VMEM((2, tl, w), F32)] * 6
                       + [pltpu.VMEM((2, HGRN_SUB * HGRN_SUB, A_HEAD_DIM), F32)],
        compiler_params=_cparams(("parallel", "arbitrary")),
        name="hgrn_scan",
    )(proj, proj, proj, proj, proj, proj, lb.reshape(2, 1, w), s0)
    return (o_f, o_b), s_fin


def _exclusive_count(x):
    r, n = x.shape
    ji = lax.broadcasted_iota(jnp.int32, (LANES, LANES), 0)
    ii = lax.broadcasted_iota(jnp.int32, (LANES, LANES), 1)
    upper = jnp.where(ji < ii, 1.0, 0.0).astype(BF16)
    carry = jnp.zeros((r, 1), F32)
    outs = []
    for blk in range(n // LANES):
        xb = x[:, blk * LANES:(blk + 1) * LANES]
        outs.append(jnp.dot(xb.astype(BF16), upper, preferred_element_type=F32) + carry)
        carry = carry + jnp.sum(xb, axis=1, keepdims=True)
    return jnp.concatenate(outs, axis=1)


def _select_kernel(a_ref, pos_ref, *, cap):
    a = a_ref[...]

    def count(mask):
        return jnp.sum(jnp.where(mask, 1.0, 0.0), axis=1, keepdims=True)

    def body(i, t):
        cand = t | lax.shift_left(jnp.int32(1), 30 - i)
        return jnp.where(count(a >= pltpu.bitcast(cand, F32)) >= cap, cand, t)

    t = lax.fori_loop(0, 31, body, jnp.zeros((a.shape[0], 1), jnp.int32))
    v = jnp.min(jnp.where(a >= pltpu.bitcast(t, F32), a, jnp.inf), axis=1, keepdims=True)

    def too_low(v):
        return jnp.max(count(a > v)) >= cap

    def step_up(v):
        nxt = jnp.min(jnp.where(a > v, a, jnp.inf), axis=1, keepdims=True)
        return jnp.where(count(a > v) >= cap, nxt, v)

    v = lax.while_loop(too_low, step_up, v)
    gt = jnp.where(a > v, 1.0, 0.0)
    eq = jnp.where(a == v, 1.0, 0.0)
    need = cap - jnp.sum(gt, axis=1, keepdims=True)
    sel = gt + eq * jnp.where(_exclusive_count(eq) < need, 1.0, 0.0)
    pos = _exclusive_count(sel)
    pos_ref[...] = jnp.where(sel > 0.0, pos, -1.0).astype(jnp.int32)


def route_select(aff_rows, cap):
    r, n = aff_rows.shape
    rb = min(r, LANES)
    return pl.pallas_call(
        functools.partial(_select_kernel, cap=cap),
        grid=(r // rb,),
        in_specs=[pl.BlockSpec((rb, n), lambda i: (i, 0))],
        out_specs=pl.BlockSpec((rb, n), lambda i: (i, 0)),
        out_shape=jax.ShapeDtypeStruct((r, n), jnp.int32),
        compiler_params=_cparams(("parallel",)),
        name="route_select",
    )(aff_rows)


def _gather_ctx_kernel(pos_ref, u_ref, o_ref):
    e, n = pos_ref.shape
    cap = o_ref.shape[1]
    slot = lax.broadcasted_iota(jnp.int32, (cap, n), 0)
    onehot = jnp.concatenate([jnp.where(pos_ref[ei:ei + 1, :] == slot, 1.0, 0.0).astype(BF16) for ei in range(e)],
                             axis=0)
    x = jnp.dot(onehot, u_ref[...], preferred_element_type=F32)
    for ei in range(e):
        o_ref[ei] = x[ei * cap:(ei + 1) * cap].astype(BF16)


def _gather_lat_kernel(pos_ref, u_ref, o_ref):
    cap = o_ref.shape[1]
    n = u_ref.shape[0]
    chunk = min(n, 2 * ROW_TILE)
    slot = lax.broadcasted_iota(jnp.int32, (cap, chunk), 0)
    acc = jnp.zeros(o_ref.shape[1:], F32)
    for c0 in range(0, n, chunk):
        onehot = jnp.where(pos_ref[0, :, c0:c0 + chunk] == slot, 1.0, 0.0).astype(BF16)
        acc = acc + jnp.dot(onehot, u_ref[c0:c0 + chunk, :], preferred_element_type=F32)
    o_ref[0] = acc.astype(BF16)


def route_gather(pos_ctx, pos_lat, u2, nb, seq, nl, lseq, n_exp):
    d = u2.shape[1]
    cap_c = CAP_FACTOR * seq // n_exp
    cap_l = CAP_FACTOR * lseq // n_exp
    xs_c = pl.pallas_call(
        _gather_ctx_kernel,
        grid=(nb,),
        in_specs=[pl.BlockSpec((n_exp, seq), lambda b: (b, 0)), pl.BlockSpec((seq, d), lambda b: (b, 0))],
        out_specs=pl.BlockSpec((n_exp, cap_c, d), lambda b: (0, b, 0)),
        out_shape=jax.ShapeDtypeStruct((n_exp, nb * cap_c, d), BF16),
        compiler_params=_cparams(("parallel",)),
        name="route_gather_ctx",
    )(pos_ctx, u2)
    lat_blk0 = nb * seq // lseq
    xs_l = pl.pallas_call(
        _gather_lat_kernel,
        grid=(nl, n_exp),
        in_specs=[pl.BlockSpec((1, 1, lseq), lambda b, ei: (b * n_exp + ei, 0, 0)),
                  pl.BlockSpec((lseq, d), lambda b, ei: (lat_blk0 + b, 0))],
        out_specs=pl.BlockSpec((1, cap_l, d), lambda b, ei: (ei, b, 0)),
        out_shape=jax.ShapeDtypeStruct((n_exp, nl * cap_l, d), BF16),
        compiler_params=_cparams(("parallel", "arbitrary")),
        name="route_gather_lat",
    )(pos_lat.reshape(nl * n_exp, 1, lseq), u2)
    return xs_c, xs_l


def _combine_kernel(pos_ref, gate_ref, yc_ref, yl_ref, h_ref, mod_ref, lng_ref, lnb_ref, o_ref, *, ctx_tiles):
    tn, e = pos_ref.shape
    d = h_ref.shape[1]

    def finish(moe):
        gate2 = mod_ref[0, 5:6, :]
        o_ref[...] = _layer_norm_rows(DN_ALPHA * h_ref[...] + gate2 * moe, lng_ref[...], lnb_ref[...])

    @pl.when(pl.program_id(0) < ctx_tiles)
    def _():
        cap = yc_ref.shape[1]
        lane = lax.broadcasted_iota(jnp.int32, (e, e * cap), 1)
        owner = lax.broadcasted_iota(jnp.int32, (e, e * cap), 0)
        expand = jnp.where(lane // cap == owner, 1.0, 0.0).astype(BF16)
        posx = jnp.dot(pos_ref[...].astype(F32).astype(BF16), expand, preferred_element_type=F32)
        gatex = sum(jnp.dot(part, expand, preferred_element_type=F32) for part in _split3(gate_ref[...]))
        slot = (lax.broadcasted_iota(jnp.int32, (tn, e * cap), 1) % cap).astype(F32)
        w = jnp.where(posx == slot, gatex, 0.0)
        w_hi = w.astype(BF16)
        w_lo = (w - w_hi.astype(F32)).astype(BF16)
        y = yc_ref[...].reshape(e * cap, d)
        finish(jnp.dot(w_hi, y, preferred_element_type=F32) + jnp.dot(w_lo, y, preferred_element_type=F32))

    @pl.when(pl.program_id(0) >= ctx_tiles)
    def _():
        cap = yl_ref.shape[1]
        slot = lax.broadcasted_iota(jnp.int32, (tn, cap), 1)
        acc = jnp.zeros((tn, d), F32)
        for ei in range(e):
            onehot = jnp.where(pos_ref[:, ei:ei + 1] == slot, 1.0, 0.0).astype(BF16)
            acc = acc + gate_ref[:, ei:ei + 1] * jnp.dot(onehot, yl_ref[ei], preferred_element_type=F32)
        finish(acc)


def route_combine(pos_t, gate_t, ys, h, mod, ln_g, ln_b, nb, seq, nl, lseq):
    m, d = h.shape
    e = pos_t.shape[1]
    tn = seq
    cap_c = CAP_FACTOR * seq // e
    cap_l = CAP_FACTOR * lseq // e
    lat_blk0 = nb * cap_c // cap_l
    per_seq = lseq // tn
    cidx = functools.partial(_cond_index, tile_rows=tn, n_ctx_rows=nb * seq, lat_seq=lseq)
    row = lambda i: (i, 0)
    fixed = lambda i: (0, 0)
    return pl.pallas_call(
        functools.partial(_combine_kernel, ctx_tiles=nb),
        grid=(m // tn,),
        in_specs=[pl.BlockSpec((tn, e), row), pl.BlockSpec((tn, e), row),
                  pl.BlockSpec((e, cap_c, d), lambda i: (0, jnp.minimum(i, nb - 1), 0)),
                  pl.BlockSpec((e, cap_l, d), lambda i: (0, lat_blk0 + jnp.maximum(i - nb, 0) // per_seq, 0)),
                  pl.BlockSpec((tn, d), row),
                  pl.BlockSpec((1, 6, d), lambda i: (cidx(i), 0, 0)),
                  pl.BlockSpec((1, d), fixed), pl.BlockSpec((1, d), fixed)],
        out_specs=pl.BlockSpec((tn, d), row),
        out_shape=jax.ShapeDtypeStruct((m, d), F32),
        compiler_params=_cparams(("arbitrary",)),
        name="route_combine",
    )(pos_t, gate_t, ys, ys, h, mod, ln_g.reshape(1, d), ln_b.reshape(1, d))


def _expert_kernel(xc_ref, xl_ref, w1_ref, w3_ref, w2_ref, o_ref, *, ctx_tiles):
    x = jnp.where(pl.program_id(1) < ctx_tiles, xc_ref[0], xl_ref[0])
    a = jnp.dot(x, w1_ref[0].astype(BF16), preferred_element_type=F32)
    g = jnp.dot(x, w3_ref[0].astype(BF16), preferred_element_type=F32)
    hid = (a * jax.nn.sigmoid(a) * g).astype(BF16)
    o_ref[0] = jnp.dot(hid, w2_ref[0].astype(BF16), preferred_element_type=F32).astype(BF16)


def expert_ffn(xs_c, xs_l, w1, w3, w2, layer):
    e, mc, d = xs_c.shape
    ml = xs_l.shape[1]
    f = w1.shape[3]
    tm = ROW_TILE
    ct, lt = mc // tm, ml // tm
    return pl.pallas_call(
        functools.partial(_expert_kernel, ctx_tiles=ct),
        grid=(e, ct + lt),
        in_specs=[pl.BlockSpec((1, tm, d), lambda ei, i: (ei, jnp.minimum(i, ct - 1), 0)),
                  pl.BlockSpec((1, tm, d), lambda ei, i: (ei, jnp.maximum(i - ct, 0), 0)),
                  pl.BlockSpec((None, 1, d, f), lambda ei, i: (layer, ei, 0, 0)),
                  pl.BlockSpec((None, 1, d, f), lambda ei, i: (layer, ei, 0, 0)),
                  pl.BlockSpec((None, 1, f, d), lambda ei, i: (layer, ei, 0, 0))],
        out_specs=pl.BlockSpec((1, tm, d), lambda ei, i: (ei, i, 0)),
        out_shape=jax.ShapeDtypeStruct((e, mc + ml, d), BF16),
        compiler_params=_cparams(("parallel", "arbitrary")),
        name="expert_ffn",
    )(xs_c, xs_l, w1, w3, w2)


def moe_layer(aff_t, u2, h, mod, w1, w3, w2, layer, ln_g, ln_b, nb, seq, nl, lseq):
    e = aff_t.shape[0]
    n_ctx = nb * seq
    seq_rows = lambda a, n, length: a.reshape(e, n, length).transpose(1, 0, 2).reshape(n * e, length)
    pos_c = route_select(seq_rows(aff_t[:, :n_ctx], nb, seq), CAP_FACTOR * seq // e)
    pos_l = route_select(seq_rows(aff_t[:, n_ctx:], nl, lseq), CAP_FACTOR * lseq // e)
    xs_c, xs_l = route_gather(pos_c, pos_l, u2, nb, seq, nl, lseq, e)
    ys = expert_ffn(xs_c, xs_l, w1, w3, w2, layer)
    tok_rows = lambda p, n, length: p.reshape(n, e, length).transpose(0, 2, 1).reshape(n * length, e)
    pos_t = jnp.concatenate([tok_rows(pos_c, nb, seq), tok_rows(pos_l, nl, lseq)], axis=0)
    return route_combine(pos_t, aff_t.T, ys, h, mod, ln_g, ln_b, nb, seq, nl, lseq)


def _dot_split(a, b, passes=3):
    ah, bh = a.astype(BF16), b.astype(BF16)
    out = jnp.dot(ah, bh, preferred_element_type=F32)
    if passes >= 3:
        al = (a - ah.astype(F32)).astype(BF16)
        bl = (b - bh.astype(F32)).astype(BF16)
        out = out + jnp.dot(ah, bl, preferred_element_type=F32) + jnp.dot(al, bh, preferred_element_type=F32)
    return out


def _short_conv_kernel(x_ref, before_ref, after_ref, w_ref, b_ref, o_ref, *, ctx_tiles, ctx_seq, lat_seq):
    rows = x_ref.shape[0]
    x = x_ref[...]
    seq = jnp.where(pl.program_id(0) < ctx_tiles, ctx_seq, lat_seq)
    row = lax.broadcasted_iota(jnp.int32, x.shape, 0)
    pos = (pl.program_id(0) * rows + row) & (seq - 1)
    prev = jnp.where(row == 0, before_ref[SUBLANES - 1:SUBLANES, :], pltpu.roll(x, 1, axis=0))
    nxt = jnp.where(row == rows - 1, after_ref[0:1, :], pltpu.roll(x, rows - 1, axis=0))
    prev = jnp.where(pos == 0, 0.0, prev)
    nxt = jnp.where(pos == seq - 1, 0.0, nxt)
    o_ref[...] = prev * w_ref[0:1, :] + x * w_ref[1:2, :] + nxt * w_ref[2:3, :] + b_ref[...]


def hyena_short_conv(proj, conv_w, conv_b, n_ctx_rows, ctx_seq, lat_seq):
    m = proj.shape[0]
    rb = 2 * ROW_TILE
    assert ctx_seq & (ctx_seq - 1) == 0 and lat_seq & (lat_seq - 1) == 0
    col0 = 5 * A_WIDTH // B_WIDTH
    halo = rb // SUBLANES
    last = m // SUBLANES - 1
    return pl.pallas_call(
        functools.partial(_short_conv_kernel, ctx_tiles=n_ctx_rows // rb, ctx_seq=ctx_seq, lat_seq=lat_seq),
        grid=(m // rb, 3),
        in_specs=[pl.BlockSpec((rb, B_WIDTH), lambda i, j: (i, col0 + j)),
                  pl.BlockSpec((SUBLANES, B_WIDTH), lambda i, j: (jnp.maximum(i * halo - 1, 0), col0 + j)),
                  pl.BlockSpec((SUBLANES, B_WIDTH), lambda i, j: (jnp.minimum((i + 1) * halo, last), col0 + j)),
                  pl.BlockSpec((SHORT_CONV, B_WIDTH), lambda i, j: (0, j)),
                  pl.BlockSpec((1, B_WIDTH), lambda i, j: (0, j))],
        out_specs=pl.BlockSpec((None, rb, B_WIDTH), lambda i, j: (j, i, 0)),
        out_shape=jax.ShapeDtypeStruct((3, m, B_WIDTH), F32),
        compiler_params=_cparams(("parallel", "parallel")),
        name="hyena_short_conv",
    )(proj, proj, proj, conv_w, conv_b.reshape(1, -1))


def _filter_kernel(z_ref, meta_ref, w1_ref, b1_ref, w2_ref, b2_ref, w3_ref, f_ref, dl_ref, o_ref):
    hdn = jnp.sin(f_ref[0:1, :] * (_dot_split(z_ref[...], w1_ref[...]) + b1_ref[...]))
    hdn = jnp.sin(f_ref[1:2, :] * (_dot_split(hdn, w2_ref[...]) + b2_ref[...]))
    filt = _dot_split(hdn, w3_ref[...])
    t = meta_ref[:, 0:1]
    fwd = meta_ref[:, 1:2]
    bwd = meta_ref[:, 2:3]
    win = jnp.exp(-t * dl_ref[...])
    for o in range(HYENA_ORDER):
        hf = filt[:, (2 * o) * B_WIDTH:(2 * o + 1) * B_WIDTH]
        hb = filt[:, (2 * o + 1) * B_WIDTH:(2 * o + 2) * B_WIDTH]
        o_ref[o] = (fwd * hf + bwd * hb) * win


def hyena_filter_taps(L, w1, b1, w2, b2, w3, freq):
    n = 2 * L
    t = jnp.linspace(0.0, 1.0, L, dtype=F32)[:, None]
    bands = (FILTER_EMB - 1) // 2
    ang = (2.0 * math.pi / L) * jnp.arange(L, dtype=F32)[:, None] * jnp.linspace(1e-4, bands - 1, bands, dtype=F32)[None, :]
    z = jnp.concatenate([t, jnp.cos(ang), -jnp.sin(ang)], axis=-1)
    deltas = jnp.abs(jnp.linspace(math.log(DECAY_TARGET) / SLOW_DECAY, math.log(DECAY_TARGET) / FAST_DECAY, B_WIDTH,
                                  dtype=F32)).reshape(1, B_WIDTH)
    lag = np.concatenate([np.arange(L), [0], np.arange(L - 1, 0, -1)])
    side = np.zeros((n, 7), np.float32)
    side[:L, 0] = 1.0
    side[L + 1:, 1] = 1.0
    meta = jnp.concatenate([t[lag], jnp.asarray(side)], axis=1)
    tr = min(n, 512)
    hid = w1.shape[1]
    fixed = lambda i: (0, 0)
    emb = LANES
    z = jnp.pad(z[lag], ((0, 0), (0, emb - FILTER_EMB)))
    w1 = jnp.pad(w1, ((0, emb - FILTER_EMB), (0, 0)))
    return pl.pallas_call(
        _filter_kernel,
        grid=(n // tr,),
        in_specs=[pl.BlockSpec((tr, emb), lambda i: (i, 0)), pl.BlockSpec((tr, 8), lambda i: (i, 0)),
                  pl.BlockSpec((emb, hid), fixed), pl.BlockSpec((1, hid), fixed),
                  pl.BlockSpec((hid, hid), fixed), pl.BlockSpec((1, hid), fixed),
                  pl.BlockSpec((hid, HYENA_ORDER * 2 * B_WIDTH), fixed), pl.BlockSpec((2, hid), fixed),
                  pl.BlockSpec((1, B_WIDTH), fixed)],
        out_specs=pl.BlockSpec((HYENA_ORDER, tr, B_WIDTH), lambda i: (0, i, 0)),
        out_shape=jax.ShapeDtypeStruct((HYENA_ORDER, n, B_WIDTH), F32),
        compiler_params=_cparams(("parallel",)),
        name="hyena_filter_taps",
    )(z, meta, w1, b1.reshape(1, hid), w2, b2.reshape(1, hid), w3, freq, deltas)


class _FftPlan:
    def __init__(self, L, minor):
        n = 2 * L
        n1 = n // minor
        h1, k1 = n1 // 2, n1 // 2 + 1
        self.L, self.minor, self.n1, self.h1, self.k1 = L, minor, n1, h1, k1
        th = 2.0 * np.pi / n1
        ph = th * ((np.arange(k1)[:, None] * np.arange(n1)[None, :]) % n1)
        fwd = np.zeros((2 * k1, n1))
        fwd[0::2], fwd[1::2] = np.cos(ph), -np.sin(ph)
        self.first_full = self._two_f32(fwd)
        self.first_half = self._two_f32(fwd[:, :h1])
        wgt = np.where((np.arange(k1) == 0) | (np.arange(k1) == n1 // 2), 1.0, 2.0) / n
        ph = th * ((np.arange(h1)[:, None] * np.arange(k1)[None, :]) % n1)
        inv = np.zeros((h1, 2 * k1))
        inv[:, 0::2], inv[:, 1::2] = wgt * np.cos(ph), -wgt * np.sin(ph)
        self.last = self._two_f32(inv)
        k = np.arange(k1)[:, None, None] + n1 * np.arange(minor)[None, :, None]
        ph = 2.0 * np.pi * ((k * np.arange(minor)[None, None, :]) % n) / n
        c, s = np.cos(ph), np.sin(ph)
        mf = np.concatenate([np.concatenate([c, s], axis=2), np.concatenate([-s, c], axis=2)], axis=1)
        self.mid_fwd = self._hi_lo(mf)
        self.mid_inv = self._hi_lo(np.swapaxes(mf, 1, 2))

    @staticmethod
    def _two_f32(table):
        hi = table.astype(np.float32)
        lo = (table - hi.astype(np.float64)).astype(np.float32)
        return jnp.asarray(np.concatenate([hi.reshape(-1), lo.reshape(-1)]))

    @staticmethod
    def _hi_lo(mat):
        bits = mat.astype(np.float32).view(np.uint32)
        bits = (bits + 0x7FFF + ((bits >> 16) & 1)) & np.uint32(0xFFFF0000)
        hi = bits.view(np.float32)
        lo = (mat - hi.astype(np.float64)).astype(np.float32)
        return jnp.asarray(hi).astype(BF16), jnp.asarray(lo)


def _dot_hi_lo(m_hi, m_rest, x):
    m_lo = m_rest.astype(BF16)
    x_hi = x.astype(BF16)
    x_lo = (x - x_hi.astype(F32)).astype(BF16)
    return (jnp.dot(m_hi, x_hi, preferred_element_type=F32) + jnp.dot(m_hi, x_lo, preferred_element_type=F32)
            + jnp.dot(m_lo, x_hi, preferred_element_type=F32))


def _coef(tab_ref, i, size):
    return tab_ref[i] + tab_ref[size + i]


def _outer_forward(cf_ref, k, n_k, n_slabs, slab):
    size = 2 * n_k * n_slabs
    ar = ai = None
    for n1 in range(n_slabs):
        xs = slab(n1)
        tr = _coef(cf_ref, (2 * k) * n_slabs + n1, size) * xs
        ti = _coef(cf_ref, (2 * k + 1) * n_slabs + n1, size) * xs
        ar, ai = (tr, ti) if ar is None else (ar + tr, ai + ti)
    return jnp.concatenate([ar, ai], axis=0)


def _hyena_conv_kernel(cf_ref, ci_ref, u_ref, g_ref, mfh_ref, mfl_ref, mih_ref, mil_ref, k_ref, d_ref, o_ref, *,
                       minor, h1, n_k, n_seq, seq_len):
    k = pl.program_id(2)
    ct = u_ref.shape[1]

    def slab(n1):
        parts = [u_ref[g * seq_len + n1 * minor:g * seq_len + (n1 + 1) * minor, :] for g in range(n_seq)]
        return parts[0] if n_seq == 1 else jnp.concatenate(parts, axis=1)

    @pl.when(k == 0)
    def _():
        o_ref[...] = jnp.zeros(o_ref.shape, F32)

    x = _dot_hi_lo(mfh_ref[0], mfl_ref[0], _outer_forward(cf_ref, k, n_k, h1, slab))
    xr, xi = x[:minor], x[minor:]
    kr = k_ref[0, :minor, :]
    ki = k_ref[0, minor:, :]
    if n_seq > 1:
        kr = jnp.concatenate([kr] * n_seq, axis=1)
        ki = jnp.concatenate([ki] * n_seq, axis=1)
    y = jnp.concatenate([xr * kr - xi * ki, xr * ki + xi * kr], axis=0)
    gm = _dot_hi_lo(mih_ref[0], mil_ref[0], y)
    gr, gi = gm[:minor], gm[minor:]
    for n1 in range(h1):
        base = n1 * 2 * n_k + 2 * k
        part = _coef(ci_ref, base, 2 * n_k * h1) * gr + _coef(ci_ref, base + 1, 2 * n_k * h1) * gi
        for g in range(n_seq):
            rows = slice(g * seq_len + n1 * minor, g * seq_len + (n1 + 1) * minor)
            o_ref[rows, :] += part[:, g * ct:(g + 1) * ct]

    @pl.when(k == n_k - 1)
    def _():
        o_ref[...] = g_ref[...] * (o_ref[...] + d_ref[...] * u_ref[...])


def _hyena_spectrum_kernel(cf_ref, t_ref, mfh_ref, mfl_ref, o_ref, *, minor, n1, n_k):
    k = pl.program_id(1)
    a = _outer_forward(cf_ref, k, n_k, n1, lambda j: t_ref[j * minor:(j + 1) * minor, :])
    o_ref[0] = _dot_hi_lo(mfh_ref[0], mfl_ref[0], a)


def _hyena_spectrum(plan, taps, ct):
    n, w = taps.shape
    m2 = 2 * plan.minor
    mat = pl.BlockSpec((1, m2, m2), lambda c, k: (k, 0, 0))
    return pl.pallas_call(
        functools.partial(_hyena_spectrum_kernel, minor=plan.minor, n1=plan.n1, n_k=plan.k1),
        grid=(w // ct, plan.k1),
        in_specs=[pl.BlockSpec(memory_space=pltpu.SMEM), pl.BlockSpec((n, ct), lambda c, k: (0, c)), mat, mat],
        out_specs=pl.BlockSpec((1, m2, ct), lambda c, k: (k, 0, c)),
        out_shape=jax.ShapeDtypeStruct((plan.k1, m2, w), F32),
        compiler_params=_cparams(("parallel", "parallel")),
        name="hyena_tap_spectrum",
    )(plan.first_full, taps, *plan.mid_fwd)


def _hyena_conv(plan, n_blocks, n_seq, ct, u, u_plane, u_blk0, gate, gate_plane, g_blk0, spec, bias):
    w = u.shape[-1]
    rows = n_seq * plan.L
    m2 = 2 * plan.minor
    tok = lambda plane, blk0: pl.BlockSpec((None, rows, ct), lambda b, c, k: (plane, blk0 + b, c))
    mat = pl.BlockSpec((1, m2, m2), lambda b, c, k: (k, 0, 0))
    smem = pl.BlockSpec(memory_space=pltpu.SMEM)
    return pl.pallas_call(
        functools.partial(_hyena_conv_kernel, minor=plan.minor, h1=plan.h1, n_k=plan.k1, n_seq=n_seq,
                          seq_len=plan.L),
        grid=(n_blocks, w // ct, plan.k1),
        in_specs=[smem, smem, tok(u_plane, u_blk0), tok(gate_plane, g_blk0), mat, mat, mat, mat,
                  pl.BlockSpec((1, m2, ct), lambda b, c, k: (k, 0, c)),
                  pl.BlockSpec((1, ct), lambda b, c, k: (0, c))],
        out_specs=pl.BlockSpec((None, rows, ct), lambda b, c, k: (0, b, c)),
        out_shape=jax.ShapeDtypeStruct((1, n_blocks * rows, w), F32),
        compiler_params=_cparams(("parallel", "parallel", "arbitrary")),
        name="hyena_long_conv",
    )(plan.first_half, plan.last, u, gate, *plan.mid_fwd, *plan.mid_inv, spec,
      bias.reshape(1, w))


def hyena_mixer(proj, n_ctx_rows, ctx_seq, lat_seq, conv_w, conv_b, fw1, fb1, fw2, fb2, fw3, freq, fbias):
    m = proj.shape[0]
    sc = hyena_short_conv(proj, conv_w, conv_b, n_ctx_rows, ctx_seq, lat_seq)
    outs = []
    for L, row0, n_rows, minor, n_seq, ct in ((ctx_seq, 0, n_ctx_rows, 128, 4, B_WIDTH),
                                              (lat_seq, n_ctx_rows, m - n_ctx_rows, 256, 1, B_WIDTH // 2)):
        plan = _FftPlan(L, minor)
        n_blocks = n_rows // (n_seq * L)
        blk0 = row0 // (n_seq * L)
        taps = hyena_filter_taps(L, fw1, fb1, fw2, fb2, fw3, freq)
        z, z_plane, z_blk0 = sc, 0, blk0
        for o in range(HYENA_ORDER):
            spec = _hyena_spectrum(plan, taps[o], ct)
            z = _hyena_conv(plan, n_blocks, n_seq, ct, z, z_plane, z_blk0, sc, 1 + o, blk0, spec, fbias[o])
            z_plane, z_blk0 = 0, 0
        outs.append(z)
    return outs


def _rope_tables(seq_len):
    n_rows = seq_len // GRID_W
    rows = jnp.repeat(jnp.arange(n_rows), GRID_W)
    cols = jnp.tile(jnp.arange(GRID_W), n_rows)
    quarter = C_HEAD_DIM // 4
    inv = ROPE_BASE ** (-jnp.arange(quarter, dtype=F32) / quarter)
    cos_parts, sin_parts = [], []
    for pos in (rows, cols):
        a = pos.astype(F32)[:, None] * inv
        cos_parts += [jnp.cos(a), jnp.cos(a)]
        sin_parts += [-jnp.sin(a), jnp.sin(a)]
    return jnp.concatenate(cos_parts, axis=-1), jnp.concatenate(sin_parts, axis=-1)


def kernel(x_prompt, x_sample, state_hgrn, state_ret, c, c_ctx, ada_w, ada_b, ln_g, ln_b, even_w_in, even_w_out, hgrn_lb, hgrn_norm_w, hyena_conv_w, hyena_conv_b, hyena_w1, hyena_b1, hyena_w2, hyena_b2, hyena_w3, hyena_freq, hyena_bias, odd_w_in, odd_w_out, ret_decay, moe_router, moe_w1, moe_w3, moe_w2):
    nb, seq, d = x_prompt.shape
    nl, lseq, _ = x_sample.shape
    n_ctx = nb * seq
    n_lat = nl * lseq
    h = jnp.concatenate([x_prompt.reshape(n_ctx, d), x_sample.reshape(n_lat, d)], axis=0)

    cond = jnp.concatenate([c_ctx[None, :], c], axis=0)
    mods = modulation_table(cond, ada_w, ada_b).reshape(DEPTH, 1 + nl, 6, d)

    lb_soft = jax.nn.softmax(hgrn_lb.astype(F32), axis=0)
    lb_all = jnp.cumsum(lb_soft, axis=0) - lb_soft[0]
    rope_tabs = _rope_tables(lseq)
    zero_hgrn = jnp.zeros((1, 2, A_HEADS, A_HEAD_DIM, A_HEAD_DIM), F32)
    zero_ret = jnp.zeros((1, 2, C_HEADS, C_HEAD_DIM, C_HEAD_DIM), F32)

    hgrn_states, ret_states = [], []
    for l in range(DEPTH):
        j = l // 2
        mod = mods[l]
        if l % 2 == 0:
            proj = in_projection(h, mod, even_w_in, j, n_ctx, lseq)
            o_p, s_p = hgrn_scan(proj, 0, nb, seq, lb_all[j], zero_hgrn)
            o_l, _ = hgrn_scan(proj, n_ctx, nl, lseq, lb_all[j], state_hgrn[:, j])
            hgrn_states.append(s_p)
            yb_p, yb_l = hyena_mixer(proj, n_ctx, seq, lseq, hyena_conv_w[j], hyena_conv_b[j], hyena_w1[j],
                                     hyena_b1[j], hyena_w2[j], hyena_b2[j], hyena_w3[j], hyena_freq[j], hyena_bias[j])
            h, u2, aff_t = post_mixer(True, o_p, o_l, proj, (hgrn_norm_w[j], yb_p, yb_l), h, mod, even_w_out, j,
                                      ln_g[l, 0], ln_b[l, 0], moe_router[l], n_ctx, lseq)
        else:
            proj = in_projection(h, mod, odd_w_in, j, n_ctx, lseq)
            log_gamma = jax.nn.log_sigmoid(ret_decay[j].astype(F32))
            o_p, s_p = retention_scan(proj, 0, nb, seq, log_gamma, zero_ret, None)
            o_l, _ = retention_scan(proj, n_ctx, nl, lseq, log_gamma, state_ret[:, j], rope_tabs)
            ret_states.append(s_p)
            h, u2, aff_t = post_mixer(False, o_p, o_l, proj, None, h, mod, odd_w_out, j,
                                      ln_g[l, 0], ln_b[l, 0], moe_router[l], n_ctx, lseq)
        h = moe_layer(aff_t, u2, h, mod, moe_w1, moe_w3, moe_w2, l, ln_g[l, 1], ln_b[l, 1], nb, seq, nl, lseq)

    y_prompt = h[:n_ctx].reshape(nb, seq, d)
    y_sample = h[n_ctx:].reshape(nl, lseq, d)
    new_state_hgrn = jnp.stack(hgrn_states, axis=1)
    new_state_ret = jnp.stack(ret_states, axis=1)
    return (y_prompt, y_sample, new_state_hgrn, new_state_ret)
```

```python
import functools
import math

import jax
import jax.numpy as jnp
import numpy as np
from jax import lax
from jax.experimental import pallas as pl
from jax.experimental.pallas import tpu as pltpu

F32 = jnp.float32
BF16 = jnp.bfloat16

D_MODEL = 1024
DEPTH = 4
GRID_W = 64
A_WIDTH = D_MODEL // 2
A_HEADS = 4
A_HEAD_DIM = A_WIDTH // A_HEADS
F_MIN = 1e-30
B_WIDTH = D_MODEL - A_WIDTH
HYENA_ORDER = 2
FILTER_EMB = 33
SHORT_CONV = 3
DECAY_TARGET = 1e-2
FAST_DECAY = 0.3
SLOW_DECAY = 1.5
C_HEADS = 4
C_HEAD_DIM = D_MODEL // C_HEADS
ROPE_BASE = 10000.0
N_EXPERTS = 16
CAP_FACTOR = 2
LN_EPS = 1e-5
DN_ALPHA = (2 * DEPTH) ** 0.25

LANES = 128
SUBLANES = 8
VMEM_LIMIT = 56 * 1024 * 1024
ROW_TILE = 512
COL_TILE = 2048
MOD_COL_TILE = 1024
SCAN_TILE = 256
HGRN_SUB = 16


def _cparams(sem):
    return pltpu.CompilerParams(dimension_semantics=sem, vmem_limit_bytes=VMEM_LIMIT)


def _cond_index(tile, tile_rows, n_ctx_rows, lat_seq):
    row = tile * tile_rows
    return jnp.where(row < n_ctx_rows, 0, 1 + (row - n_ctx_rows) // lat_seq)


def _mod_kernel(c_ref, w_ref, b_ref, o_ref):
    c = c_ref[...]
    s = c * jax.nn.sigmoid(c)
    o_ref[0] = jnp.dot(s.astype(BF16), w_ref[0].astype(BF16), preferred_element_type=F32) + b_ref[0]


def modulation_table(cond, ada_w, ada_b):
    r, d = cond.shape
    depth, _, n = ada_w.shape
    tn = MOD_COL_TILE
    return pl.pallas_call(
        _mod_kernel,
        grid=(depth, n // tn),
        in_specs=[pl.BlockSpec((r, d), lambda l, j: (0, 0)),
                  pl.BlockSpec((1, d, tn), lambda l, j: (l, 0, j)),
                  pl.BlockSpec((1, 1, tn), lambda l, j: (l, 0, j))],
        out_specs=pl.BlockSpec((1, r, tn), lambda l, j: (l, 0, j)),
        out_shape=jax.ShapeDtypeStruct((depth, r, n), F32),
        compiler_params=_cparams(("parallel", "parallel")),
        name="modulation",
    )(cond, ada_w, ada_b.reshape(depth, 1, n))


def _inproj_kernel(x_ref, mod_ref, w_ref, o_ref):
    shift = mod_ref[0, 0:1, :]
    scale = mod_ref[0, 1:2, :]
    u = (x_ref[...] * (1.0 + scale) + shift).astype(BF16)
    o_ref[...] = jnp.dot(u, w_ref[...].astype(BF16), preferred_element_type=F32)


def in_projection(x, mod, w_stack, layer, n_ctx_rows, lat_seq):
    m, d = x.shape
    n = w_stack.shape[2]
    tm, tn = ROW_TILE, COL_TILE
    cidx = functools.partial(_cond_index, tile_rows=tm, n_ctx_rows=n_ctx_rows, lat_seq=lat_seq)
    return pl.pallas_call(
        _inproj_kernel,
        grid=(n // tn, m // tm),
        in_specs=[pl.BlockSpec((tm, d), lambda j, i: (i, 0)),
                  pl.BlockSpec((1, 6, d), lambda j, i: (cidx(i), 0, 0)),
                  pl.BlockSpec((None, d, tn), lambda j, i: (layer, 0, j))],
        out_specs=pl.BlockSpec((tm, tn), lambda j, i: (i, j)),
        out_shape=jax.ShapeDtypeStruct((m, n), F32),
        compiler_params=_cparams(("parallel", "parallel")),
        name="in_projection",
    )(x, mod, w_stack)


def _layer_norm_rows(z, g, b):
    mu = jnp.mean(z, axis=-1, keepdims=True)
    zc = z - mu
    var = jnp.mean(zc * zc, axis=-1, keepdims=True)
    return zc * lax.rsqrt(var + LN_EPS) * g + b


def _post_mixer(y, h_ref, mod_ref, w_ref, lng_ref, lnb_ref, r_ref, h_out, u_out, aff_out):
    gate1 = mod_ref[0, 2:3, :]
    shift2 = mod_ref[0, 3:4, :]
    scale2 = mod_ref[0, 4:5, :]
    mix = jnp.dot(y.astype(BF16), w_ref[...].astype(BF16), preferred_element_type=F32)
    hn = _layer_norm_rows(DN_ALPHA * h_ref[...] + gate1 * mix, lng_ref[...], lnb_ref[...])
    h_out[...] = hn
    u2 = (hn * (1.0 + scale2) + shift2).astype(BF16)
    u_out[...] = u2
    logits = lax.dot_general(r_ref[...].astype(BF16), u2, (((1,), (1,)), ((), ())), preferred_element_type=F32)
    mx = jnp.max(logits, axis=0, keepdims=True)
    ex = jnp.exp(logits - mx)
    aff_out[...] = ex / jnp.sum(ex, axis=0, keepdims=True)


def _group_pick(ctx_tiles, ctx_refs, lat_refs):
    is_ctx = pl.program_id(0) < ctx_tiles
    return jnp.where(is_ctx, sum(r[...] for r in ctx_refs), sum(r[...] for r in lat_refs))


def _post_even_kernel(ofc_ref, obc_ref, ofl_ref, obl_ref, ybc_ref, ybl_ref, g_ref, nw_ref, h_ref, mod_ref, w_ref,
                      lng_ref, lnb_ref, r_ref, h_out, u_out, aff_out, *, ctx_tiles):
    o = _group_pick(ctx_tiles, (ofc_ref, obc_ref), (ofl_ref, obl_ref))
    g = g_ref[...]
    parts = []
    for hd in range(A_HEADS):
        sl = slice(hd * A_HEAD_DIM, (hd + 1) * A_HEAD_DIM)
        oh = o[:, sl]
        parts.append(oh * lax.rsqrt(jnp.mean(oh * oh, axis=-1, keepdims=True) + LN_EPS))
    ya = jnp.concatenate(parts, axis=-1) * nw_ref[...] * (g * jax.nn.sigmoid(g))
    y = jnp.concatenate([ya, _group_pick(ctx_tiles, (ybc_ref,), (ybl_ref,))], axis=-1)
    _post_mixer(y, h_ref, mod_ref, w_ref, lng_ref, lnb_ref, r_ref, h_out, u_out, aff_out)


def _post_odd_kernel(ofc_ref, obc_ref, ofl_ref, obl_ref, g_ref, h_ref, mod_ref, w_ref, lng_ref, lnb_ref, r_ref,
                     h_out, u_out, aff_out, *, ctx_tiles):
    o = _group_pick(ctx_tiles, (ofc_ref, obc_ref), (ofl_ref, obl_ref))
    g = g_ref[...]
    parts = []
    for hd in range(C_HEADS):
        sl = slice(hd * C_HEAD_DIM, (hd + 1) * C_HEAD_DIM)
        oh = o[:, sl]
        mu = jnp.mean(oh, axis=-1, keepdims=True)
        oc = oh - mu
        parts.append(oc * lax.rsqrt(jnp.mean(oc * oc, axis=-1, keepdims=True) + LN_EPS))
    y = jnp.concatenate(parts, axis=-1) * (g * jax.nn.sigmoid(g))
    _post_mixer(y, h_ref, mod_ref, w_ref, lng_ref, lnb_ref, r_ref, h_out, u_out, aff_out)


def post_mixer(even, scan_ctx, scan_lat, proj, extra, h, mod, w_out_stack, layer, ln_g, ln_b, router, n_ctx_rows,
               lat_seq):
    m, d = h.shape
    e = router.shape[1]
    tm = ROW_TILE
    ct = n_ctx_rows // tm
    cidx = functools.partial(_cond_index, tile_rows=tm, n_ctx_rows=n_ctx_rows, lat_seq=lat_seq)
    width = (scan_ctx[0] if isinstance(scan_ctx, tuple) else scan_ctx).shape[-1]
    row = lambda i: (i, 0)
    fixed = lambda i: (0, 0)
    ctx_row = lambda i: jnp.minimum(i, ct - 1)
    lat_row = lambda i: jnp.maximum(i - ct, 0)
    plane = lambda p, rowfn, w: pl.BlockSpec((None, tm, w), lambda i: (p, rowfn(i), 0))
    common_specs = [pl.BlockSpec((tm, d), row),
                    pl.BlockSpec((1, 6, d), lambda i: (cidx(i), 0, 0)),
                    pl.BlockSpec((None, d, d), lambda i: (layer, 0, 0)),
                    pl.BlockSpec((1, d), fixed),
                    pl.BlockSpec((1, d), fixed),
                    pl.BlockSpec((e, d), fixed)]
    common_args = [h, mod, w_out_stack, ln_g.reshape(1, d), ln_b.reshape(1, d), router.T]
    def directions(scan, rowfn):
        if isinstance(scan, tuple):
            return [pl.BlockSpec((tm, width), lambda i: (rowfn(i), 0))] * 2, list(scan)
        return [plane(0, rowfn, width), plane(1, rowfn, width)], [scan, scan]

    ctx_specs, ctx_args = directions(scan_ctx, ctx_row)
    lat_specs, lat_args = directions(scan_lat, lat_row)
    specs = ctx_specs + lat_specs
    args = ctx_args + lat_args
    if even:
        norm_w, yb_ctx, yb_lat = extra
        gate_col = 4 * A_WIDTH // width
        specs += [plane(0, ctx_row, B_WIDTH), plane(0, lat_row, B_WIDTH),
                  pl.BlockSpec((tm, width), lambda i: (i, gate_col)), pl.BlockSpec((1, width), fixed)]
        args += [yb_ctx, yb_lat, proj, norm_w.reshape(1, width)]
        body = _post_even_kernel
    else:
        gate_col = 3
        specs += [pl.BlockSpec((tm, width), lambda i: (i, gate_col))]
        args += [proj]
        body = _post_odd_kernel
    return pl.pallas_call(
        functools.partial(body, ctx_tiles=ct),
        grid=(m // tm,),
        in_specs=specs + common_specs,
        out_specs=[pl.BlockSpec((tm, d), row), pl.BlockSpec((tm, d), row), pl.BlockSpec((e, tm), lambda i: (0, i))],
        out_shape=[jax.ShapeDtypeStruct((m, d), F32), jax.ShapeDtypeStruct((m, d), BF16),
                   jax.ShapeDtypeStruct((e, m), F32)],
        compiler_params=_cparams(("arbitrary",)),
        name="post_mixer_even" if even else "post_mixer_odd",
    )(*(args + common_args))


def _rope_halves(x, cos, sin_signed):
    outs = []
    for p in range(2):
        sl = slice(p * LANES, (p + 1) * LANES)
        xp = x[:, sl]
        outs.append(xp * cos[:, sl] + pltpu.roll(xp, LANES // 2, axis=1) * sin_signed[:, sl])
    return jnp.concatenate(outs, axis=-1)


def _retention_kernel(lg_ref, q_ref, k_ref, v_ref, cos_ref, sin_ref, s0_ref, o_ref, sfin_ref, s_scr, *, rope):
    d = pl.program_id(1)
    t = pl.program_id(2)
    c = q_ref.shape[0]
    dk = C_HEAD_DIM

    @pl.when(t == 0)
    def _():
        s_scr[...] = s0_ref[0, 0]

    ti = lax.broadcasted_iota(jnp.int32, (c, c), 0)
    si = lax.broadcasted_iota(jnp.int32, (c, c), 1)
    diff = jnp.where(d == 0, ti - si, si - ti).astype(F32)
    jt = lax.broadcasted_iota(jnp.int32, (c, dk), 0)
    eq = jnp.where(d == 0, jt + 1, c - jt).astype(F32)
    ek = jnp.where(d == 0, c - 1 - jt, jt).astype(F32)
    for hd in range(C_HEADS):
        lg = lg_ref[d, hd]
        sl = slice(hd * dk, (hd + 1) * dk)
        q = q_ref[:, sl]
        k = k_ref[:, sl] * (dk ** -0.5)
        if rope:
            q = _rope_halves(q, cos_ref[...], sin_ref[...])
            k = _rope_halves(k, cos_ref[...], sin_ref[...])
        v = v_ref[:, sl].astype(BF16)
        rel = jnp.where(diff >= 0.0, jnp.exp(lg * jnp.maximum(diff, 0.0)), 0.0)
        att = lax.dot_general(q.astype(BF16), k.astype(BF16), (((1,), (1,)), ((), ())),
                              preferred_element_type=F32) * rel
        intra = jnp.dot(att.astype(BF16), v, preferred_element_type=F32)
        s_old = s_scr[hd]
        inter = jnp.dot((q * jnp.exp(lg * eq)).astype(BF16), s_old.astype(BF16), preferred_element_type=F32)
        o_ref[:, sl] = inter + intra
        kd = (k * jnp.exp(lg * ek)).astype(BF16)
        s_new = s_old * jnp.exp(lg * c) + lax.dot_general(kd, v, (((0,), (0,)), ((), ())),
                                                          preferred_element_type=F32)
        s_scr[hd] = s_new

    @pl.when(t == pl.num_programs(2) - 1)
    def _():
        sfin_ref[0, 0] = s_scr[...]


def retention_scan(proj, row0, n_seq, seq_len, log_gamma, s0, rope_tabs):
    d = D_MODEL
    c = SCAN_TILE
    nt = seq_len // c
    t0 = row0 // c
    rope = rope_tabs is not None
    if rope:
        cos, sin = rope_tabs
    else:
        cos = sin = jnp.zeros((c, C_HEAD_DIM), F32)
    tile = lambda dd, t: t + dd * (nt - 1 - 2 * t)
    tok = lambda col: pl.BlockSpec((c, d), lambda b, dd, t: (t0 + b * nt + tile(dd, t), col))
    rope_spec = pl.BlockSpec((c, C_HEAD_DIM), (lambda b, dd, t: (tile(dd, t), 0)) if rope else (lambda b, dd, t: (0, 0)))
    shared_s0 = s0.shape[0] == 1
    st_shape = (1, 1, C_HEADS, C_HEAD_DIM, C_HEAD_DIM)
    return pl.pallas_call(
        functools.partial(_retention_kernel, rope=rope),
        grid=(n_seq, 2, nt),
        in_specs=[pl.BlockSpec(memory_space=pltpu.SMEM), tok(0), tok(1), tok(2), rope_spec, rope_spec,
                  pl.BlockSpec(st_shape, lambda b, dd, t: (0 if shared_s0 else b, dd, 0, 0, 0))],
        out_specs=[pl.BlockSpec((None, c, d), lambda b, dd, t: (dd, b * nt + tile(dd, t), 0)),
                   pl.BlockSpec(st_shape, lambda b, dd, t: (b, dd, 0, 0, 0))],
        out_shape=[jax.ShapeDtypeStruct((2, n_seq * seq_len, d), F32),
                   jax.ShapeDtypeStruct((n_seq, 2, C_HEADS, C_HEAD_DIM, C_HEAD_DIM), F32)],
        scratch_shapes=[pltpu.VMEM((C_HEADS, C_HEAD_DIM, C_HEAD_DIM), F32)],
        compiler_params=_cparams(("parallel", "parallel", "arbitrary")),
        name="retention_scan",
    )(log_gamma, proj, proj, proj, cos, sin, s0)


def _split3(x):
    hi = x.astype(BF16)
    r1 = x - hi.astype(F32)
    mid = r1.astype(BF16)
    lo = (r1 - mid.astype(F32)).astype(BF16)
    return hi, mid, lo


def _hgrn_kernel(qf_ref, vf_ref, zf_ref, qb_ref, vb_ref, zb_ref, lb_ref, s0_ref, of_ref, ob_ref, sfin_ref,
                 st_scr, qa_scr, qe_scr, k_scr, b_scr, b2_scr, r_scr, bias_scr, *, sub):
    t = pl.program_id(1)
    streams = ((qf_ref, vf_ref, zf_ref, of_ref), (qb_ref, vb_ref, zb_ref, ob_ref))
    tl, w = qf_ref.shape
    dh = A_HEAD_DIM
    nsub = tl // sub
    log2e = 1.0 / math.log(2.0)

    @pl.when(t == 0)
    def _():
        for dirn in range(2):
            for hd in range(A_HEADS):
                st_scr[dirn, hd] = s0_ref[0, dirn, hd].T

    ti = lax.broadcasted_iota(jnp.int32, (tl, tl), 0)
    si = lax.broadcasted_iota(jnp.int32, (tl, tl), 1)
    pr = lax.broadcasted_iota(jnp.int32, (sub * sub, dh), 0)
    for dirn, (q_ref, _, z_ref, _) in enumerate(streams):
        sign = 1 if dirn == 0 else -1
        lb = lb_ref[dirn]
        z = z_ref[...]
        sg = jax.nn.sigmoid(z)
        lf = jnp.log(jnp.maximum(lb + (1.0 - lb) * sg, F_MIN))
        k = (1.0 - lb) * jax.nn.sigmoid(-z)
        k_scr[dirn] = k
        seen = ((ti - si) * sign >= 0) & ((ti // sub) == (si // sub))
        tri = jnp.where(seen, 1.0, 0.0).astype(BF16)
        b = sum(jnp.dot(tri, part, preferred_element_type=F32) for part in _split3(lf))
        b_scr[dirn] = b
        b2_scr[dirn] = b * log2e
        r_scr[dirn] = b * log2e - jnp.log2(k)
        q = q_ref[...]
        q = q * jax.nn.sigmoid(q)
        qa_scr[dirn] = q
        qe_scr[dirn] = q * jnp.exp(b)
        bias_scr[dirn] = jnp.where(((pr % sub) - (pr // sub)) * sign >= 0, 0.0, -1e30)

    ones = jnp.ones((dh, dh), BF16)

    def step(i, carry):
        for dirn, (_, v_ref, _, o_ref) in enumerate(streams):
            blk = i if dirn == 0 else nsub - 1 - i
            rows = pl.ds(pl.multiple_of(blk * sub, sub), sub)
            last = sub - 1 if dirn == 0 else 0
            for hd in range(A_HEADS):
                sl = slice(hd * dh, (hd + 1) * dh)
                qs = qa_scr[dirn, rows, sl]
                ks = k_scr[dirn, rows, sl]
                vs = v_ref[rows, sl]
                bs = b_scr[dirn, rows, sl]
                b2 = b2_scr[dirn, rows, sl]
                rs = r_scr[dirn, rows, sl]
                st = st_scr[dirn, hd]
                inter = lax.dot_general(qe_scr[dirn, rows, sl].astype(BF16), st.astype(BF16),
                                        (((1,), (1,)), ((), ())), preferred_element_type=F32)
                es = [(qs * jnp.exp2(b2 - rs[s:s + 1, :] + bias_scr[dirn, s * sub:(s + 1) * sub, :])).astype(BF16)
                      for s in range(sub)]
                att = jnp.dot(jnp.concatenate(es, axis=0), ones, preferred_element_type=F32)
                intra = att[0:sub] * vs[0:1, :]
                for s in range(1, sub):
                    intra = intra + att[s * sub:(s + 1) * sub] * vs[s:s + 1, :]
                o_ref[rows, sl] = inter + intra
                b_end = bs[last:last + 1, :]
                kd = (ks * jnp.exp(b_end - bs)).astype(BF16)
                st_scr[dirn, hd] = st * jnp.exp(b_end) + lax.dot_general(
                    vs.astype(BF16), kd, (((0,), (0,)), ((), ())), preferred_element_type=F32)
        return carry

    lax.fori_loop(0, nsub, step, 0)

    @pl.when(t == pl.num_programs(1) - 1)
    def _():
        for dirn in range(2):
            for hd in range(A_HEADS):
                sfin_ref[0, dirn, hd] = st_scr[dirn, hd].T


def hgrn_scan(proj, row0, n_seq, seq_len, lb, s0):
    w = A_WIDTH
    tl = SCAN_TILE
    nt = seq_len // tl
    t0 = row0 // tl
    fwd = lambda col: pl.BlockSpec((tl, w), lambda b, t: (t0 + b * nt + t, col))
    bwd = lambda col: pl.BlockSpec((tl, w), lambda b, t: (t0 + b * nt + nt - 1 - t, col))
    shared_s0 = s0.shape[0] == 1
    st_shape = (1, 2, A_HEADS, A_HEAD_DIM, A_HEAD_DIM)
    o_f, o_b, s_fin = pl.pallas_call(
        functools.partial(_hgrn_kernel, sub=HGRN_SUB),
        grid=(n_seq, nt),
        in_specs=[fwd(0), fwd(1), fwd(2), bwd(0), bwd(1), bwd(3),
                  pl.BlockSpec((2, 1, w), lambda b, t: (0, 0, 0)),
                  pl.BlockSpec(st_shape, lambda b, t: (0 if shared_s0 else b, 0, 0, 0, 0))],
        out_specs=[pl.BlockSpec((tl, w), lambda b, t: (b * nt + t, 0)),
                   pl.BlockSpec((tl, w), lambda b, t: (b * nt + nt - 1 - t, 0)),
                   pl.BlockSpec(st_shape, lambda b, t: (b, 0, 0, 0, 0))],
        out_shape=[jax.ShapeDtypeStruct((n_seq * seq_len, w), F32), jax.ShapeDtypeStruct((n_seq * seq_len, w), F32),
                   jax.ShapeDtypeStruct((n_seq, 2, A_HEADS, A_HEAD_DIM, A_HEAD_DIM), F32)],
        scratch_shapes=[pltpu.VMEM((2, A_HEADS, A_HEAD_DIM, A_HEAD_DIM), F32)] + [pltpu.VMEM((2, tl, w), F32)] * 6
                       + [pltpu.VMEM((2, HGRN_SUB * HGRN_SUB, A_HEAD_DIM), F32)],
        compiler_params=_cparams(("parallel", "arbitrary")),
        name="hgrn_scan",
    )(proj, proj, proj, proj, proj, proj, lb.reshape(2, 1, w), s0)
    return (o_f, o_b), s_fin


def _exclusive_count(x):
    r, n = x.shape
    ji = lax.broadcasted_iota(jnp.int32, (LANES, LANES), 0)
    ii = lax.broadcasted_iota(jnp.int32, (LANES, LANES), 1)
    upper = jnp.where(ji < ii, 1.0, 0.0).astype(BF16)
    carry = jnp.zeros((r, 1), F32)
    outs = []
    for blk in range(n // LANES):
        xb = x[:, blk * LANES:(blk + 1) * LANES]
        outs.append(jnp.dot(xb.astype(BF16), upper, preferred_element_type=F32) + carry)
        carry = carry + jnp.sum(xb, axis=1, keepdims=True)
    return jnp.concatenate(outs, axis=1)


def _select_kernel(a_ref, pos_ref, *, cap):
    a = a_ref[...]

    def count(mask):
        return jnp.sum(jnp.where(mask, 1.0, 0.0), axis=1, keepdims=True)

    def body(i, t):
        cand = t | lax.shift_left(jnp.int32(1), 30 - i)
        return jnp.where(count(a >= pltpu.bitcast(cand, F32)) >= cap, cand, t)

    t = lax.fori_loop(0, 31, body, jnp.zeros((a.shape[0], 1), jnp.int32))
    v = jnp.min(jnp.where(a >= pltpu.bitcast(t, F32), a, jnp.inf), axis=1, keepdims=True)

    def too_low(v):
        return jnp.max(count(a > v)) >= cap

    def step_up(v):
        nxt = jnp.min(jnp.where(a > v, a, jnp.inf), axis=1, keepdims=True)
        return jnp.where(count(a > v) >= cap, nxt, v)

    v = lax.while_loop(too_low, step_up, v)
    gt = jnp.where(a > v, 1.0, 0.0)
    eq = jnp.where(a == v, 1.0, 0.0)
    need = cap - jnp.sum(gt, axis=1, keepdims=True)
    sel = gt + eq * jnp.where(_exclusive_count(eq) < need, 1.0, 0.0)
    pos = _exclusive_count(sel)
    pos_ref[...] = jnp.where(sel > 0.0, pos, -1.0).astype(jnp.int32)


def route_select(aff_rows, cap):
    r, n = aff_rows.shape
    rb = min(r, LANES)
    return pl.pallas_call(
        functools.partial(_select_kernel, cap=cap),
        grid=(r // rb,),
        in_specs=[pl.BlockSpec((rb, n), lambda i: (i, 0))],
        out_specs=pl.BlockSpec((rb, n), lambda i: (i, 0)),
        out_shape=jax.ShapeDtypeStruct((r, n), jnp.int32),
        compiler_params=_cparams(("parallel",)),
        name="route_select",
    )(aff_rows)


def _gather_ctx_kernel(pos_ref, u_ref, o_ref):
    e, n = pos_ref.shape
    cap = o_ref.shape[1]
    slot = lax.broadcasted_iota(jnp.int32, (cap, n), 0)
    onehot = jnp.concatenate([jnp.where(pos_ref[ei:ei + 1, :] == slot, 1.0, 0.0).astype(BF16) for ei in range(e)],
                             axis=0)
    x = jnp.dot(onehot, u_ref[...], preferred_element_type=F32)
    for ei in range(e):
        o_ref[ei] = x[ei * cap:(ei + 1) * cap].astype(BF16)


def _gather_lat_kernel(pos_ref, u_ref, o_ref):
    n_grp, cap, d = o_ref.shape
    n = u_ref.shape[0]
    chunk = min(n, 2 * ROW_TILE)
    slot = lax.broadcasted_iota(jnp.int32, (cap, chunk), 0)
    acc = jnp.zeros((n_grp * cap, d), F32)
    for c0 in range(0, n, chunk):
        onehot = jnp.concatenate([jnp.where(pos_ref[g, :, c0:c0 + chunk] == slot, 1.0, 0.0).astype(BF16)
                                  for g in range(n_grp)], axis=0)
        acc = acc + jnp.dot(onehot, u_ref[c0:c0 + chunk, :], preferred_element_type=F32)
    for g in range(n_grp):
        o_ref[g] = acc[g * cap:(g + 1) * cap].astype(BF16)


def route_gather(pos_ctx, pos_lat, u2, nb, seq, nl, lseq, n_exp):
    d = u2.shape[1]
    cap_c = CAP_FACTOR * seq // n_exp
    cap_l = CAP_FACTOR * lseq // n_exp
    xs_c = pl.pallas_call(
        _gather_ctx_kernel,
        grid=(nb,),
        in_specs=[pl.BlockSpec((n_exp, seq), lambda b: (b, 0)), pl.BlockSpec((seq, d), lambda b: (b, 0))],
        out_specs=pl.BlockSpec((n_exp, cap_c, d), lambda b: (0, b, 0)),
        out_shape=jax.ShapeDtypeStruct((n_exp, nb * cap_c, d), BF16),
        compiler_params=_cparams(("parallel",)),
        name="route_gather_ctx",
    )(pos_ctx, u2)
    lat_blk0 = nb * seq // lseq
    grp = 2
    xs_l = pl.pallas_call(
        _gather_lat_kernel,
        grid=(nl, n_exp // grp),
        in_specs=[pl.BlockSpec((grp, 1, lseq), lambda b, ei: (b * (n_exp // grp) + ei, 0, 0)),
                  pl.BlockSpec((lseq, d), lambda b, ei: (lat_blk0 + b, 0))],
        out_specs=pl.BlockSpec((grp, cap_l, d), lambda b, ei: (ei, b, 0)),
        out_shape=jax.ShapeDtypeStruct((n_exp, nl * cap_l, d), BF16),
        compiler_params=_cparams(("parallel", "arbitrary")),
        name="route_gather_lat",
    )(pos_lat.reshape(nl * n_exp, 1, lseq), u2)
    return xs_c, xs_l


def _combine_kernel(pos_ref, gate_ref, yc_ref, yl_ref, h_ref, mod_ref, lng_ref, lnb_ref, o_ref, *, ctx_tiles,
                    ctx_seq):
    tn, e = pos_ref.shape
    d = h_ref.shape[1]

    def finish(moe):
        gate2 = mod_ref[0, 5:6, :]
        o_ref[...] = _layer_norm_rows(DN_ALPHA * h_ref[...] + gate2 * moe, lng_ref[...], lnb_ref[...])

    @pl.when(pl.program_id(0) < ctx_tiles)
    def _():
        n_sub = tn // ctx_seq
        cap = yc_ref.shape[1] // n_sub
        lane = lax.broadcasted_iota(jnp.int32, (e, e * cap), 1)
        owner = lax.broadcasted_iota(jnp.int32, (e, e * cap), 0)
        expand = jnp.where(lane // cap == owner, 1.0, 0.0).astype(BF16)
        slot = (lax.broadcasted_iota(jnp.int32, (ctx_seq, e * cap), 1) % cap).astype(F32)
        parts = []
        for g in range(n_sub):
            rows = slice(g * ctx_seq, (g + 1) * ctx_seq)
            posx = jnp.dot(pos_ref[rows, :].astype(F32).astype(BF16), expand, preferred_element_type=F32)
            gatex = sum(jnp.dot(part, expand, preferred_element_type=F32) for part in _split3(gate_ref[rows, :]))
            w = jnp.where(posx == slot, gatex, 0.0)
            w_hi = w.astype(BF16)
            w_lo = (w - w_hi.astype(F32)).astype(BF16)
            y = yc_ref[:, g * cap:(g + 1) * cap, :].reshape(e * cap, d)
            parts.append(jnp.dot(w_hi, y, preferred_element_type=F32) + jnp.dot(w_lo, y, preferred_element_type=F32))
        finish(jnp.concatenate(parts, axis=0))

    @pl.when(pl.program_id(0) >= ctx_tiles)
    def _():
        cap = yl_ref.shape[1]
        slot = lax.broadcasted_iota(jnp.int32, (tn, cap), 1)
        acc = jnp.zeros((tn, d), F32)
        for ei in range(e):
            onehot = jnp.where(pos_ref[:, ei:ei + 1] == slot, 1.0, 0.0).astype(BF16)
            acc = acc + gate_ref[:, ei:ei + 1] * jnp.dot(onehot, yl_ref[ei], preferred_element_type=F32)
        finish(acc)


def route_combine(pos_t, gate_t, ys, h, mod, ln_g, ln_b, nb, seq, nl, lseq):
    m, d = h.shape
    e = pos_t.shape[1]
    tn = ROW_TILE
    ct = nb * seq // tn
    cap_c = CAP_FACTOR * seq // e * (tn // seq)
    cap_l = CAP_FACTOR * lseq // e
    lat_blk0 = ct * cap_c // cap_l
    per_seq = lseq // tn
    cidx = functools.partial(_cond_index, tile_rows=tn, n_ctx_rows=nb * seq, lat_seq=lseq)
    row = lambda i: (i, 0)
    fixed = lambda i: (0, 0)
    return pl.pallas_call(
        functools.partial(_combine_kernel, ctx_tiles=ct, ctx_seq=seq),
        grid=(m // tn,),
        in_specs=[pl.BlockSpec((tn, e), row), pl.BlockSpec((tn, e), row),
                  pl.BlockSpec((e, cap_c, d), lambda i: (0, jnp.minimum(i, ct - 1), 0)),
                  pl.BlockSpec((e, cap_l, d), lambda i: (0, lat_blk0 + jnp.maximum(i - ct, 0) // per_seq, 0)),
                  pl.BlockSpec((tn, d), row),
                  pl.BlockSpec((1, 6, d), lambda i: (cidx(i), 0, 0)),
                  pl.BlockSpec((1, d), fixed), pl.BlockSpec((1, d), fixed)],
        out_specs=pl.BlockSpec((tn, d), row),
        out_shape=jax.ShapeDtypeStruct((m, d), F32),
        compiler_params=_cparams(("arbitrary",)),
        name="route_combine",
    )(pos_t, gate_t, ys, ys, h, mod, ln_g.reshape(1, d), ln_b.reshape(1, d))


def _expert_kernel(xc_ref, xl_ref, w1_ref, w3_ref, w2_ref, o_ref, *, ctx_tiles):
    x = jnp.where(pl.program_id(1) < ctx_tiles, xc_ref[0], xl_ref[0])
    a = jnp.dot(x, w1_ref[0].astype(BF16), preferred_element_type=F32)
    g = jnp.dot(x, w3_ref[0].astype(BF16), preferred_element_type=F32)
    hid = (a * jax.nn.sigmoid(a) * g).astype(BF16)
    o_ref[0] = jnp.dot(hid, w2_ref[0].astype(BF16), preferred_element_type=F32).astype(BF16)


def expert_ffn(xs_c, xs_l, w1, w3, w2, layer):
    e, mc, d = xs_c.shape
    ml = xs_l.shape[1]
    f = w1.shape[3]
    tm = ROW_TILE
    ct, lt = mc // tm, ml // tm
    return pl.pallas_call(
        functools.partial(_expert_kernel, ctx_tiles=ct),
        grid=(e, ct + lt),
        in_specs=[pl.BlockSpec((1, tm, d), lambda ei, i: (ei, jnp.minimum(i, ct - 1), 0)),
                  pl.BlockSpec((1, tm, d), lambda ei, i: (ei, jnp.maximum(i - ct, 0), 0)),
                  pl.BlockSpec((None, 1, d, f), lambda ei, i: (layer, ei, 0, 0)),
                  pl.BlockSpec((None, 1, d, f), lambda ei, i: (layer, ei, 0, 0)),
                  pl.BlockSpec((None, 1, f, d), lambda ei, i: (layer, ei, 0, 0))],
        out_specs=pl.BlockSpec((1, tm, d), lambda ei, i: (ei, i, 0)),
        out_shape=jax.ShapeDtypeStruct((e, mc + ml, d), BF16),
        compiler_params=_cparams(("parallel", "arbitrary")),
        name="expert_ffn",
    )(xs_c, xs_l, w1, w3, w2)


def moe_layer(aff_t, u2, h, mod, w1, w3, w2, layer, ln_g, ln_b, nb, seq, nl, lseq):
    e = aff_t.shape[0]
    n_ctx = nb * seq
    seq_rows = lambda a, n, length: a.reshape(e, n, length).transpose(1, 0, 2).reshape(n * e, length)
    pos_c = route_select(seq_rows(aff_t[:, :n_ctx], nb, seq), CAP_FACTOR * seq // e)
    pos_l = route_select(seq_rows(aff_t[:, n_ctx:], nl, lseq), CAP_FACTOR * lseq // e)
    xs_c, xs_l = route_gather(pos_c, pos_l, u2, nb, seq, nl, lseq, e)
    ys = expert_ffn(xs_c, xs_l, w1, w3, w2, layer)
    tok_rows = lambda p, n, length: p.reshape(n, e, length).transpose(0, 2, 1).reshape(n * length, e)
    pos_t = jnp.concatenate([tok_rows(pos_c, nb, seq), tok_rows(pos_l, nl, lseq)], axis=0)
    return route_combine(pos_t, aff_t.T, ys, h, mod, ln_g, ln_b, nb, seq, nl, lseq)


def _dot_split(a, b, passes=3):
    ah, bh = a.astype(BF16), b.astype(BF16)
    out = jnp.dot(ah, bh, preferred_element_type=F32)
    if passes >= 3:
        al = (a - ah.astype(F32)).astype(BF16)
        bl = (b - bh.astype(F32)).astype(BF16)
        out = out + jnp.dot(ah, bl, preferred_element_type=F32) + jnp.dot(al, bh, preferred_element_type=F32)
    return out


def _short_conv_kernel(x_ref, before_ref, after_ref, w_ref, b_ref, o_ref, *, ctx_tiles, ctx_seq, lat_seq):
    rows = x_ref.shape[0]
    x = x_ref[...]
    seq = jnp.where(pl.program_id(0) < ctx_tiles, ctx_seq, lat_seq)
    row = lax.broadcasted_iota(jnp.int32, x.shape, 0)
    pos = (pl.program_id(0) * rows + row) & (seq - 1)
    prev = jnp.where(row == 0, before_ref[SUBLANES - 1:SUBLANES, :], pltpu.roll(x, 1, axis=0))
    nxt = jnp.where(row == rows - 1, after_ref[0:1, :], pltpu.roll(x, rows - 1, axis=0))
    prev = jnp.where(pos == 0, 0.0, prev)
    nxt = jnp.where(pos == seq - 1, 0.0, nxt)
    o_ref[...] = prev * w_ref[0:1, :] + x * w_ref[1:2, :] + nxt * w_ref[2:3, :] + b_ref[...]


def hyena_short_conv(proj, conv_w, conv_b, n_ctx_rows, ctx_seq, lat_seq):
    m = proj.shape[0]
    rb = 2 * ROW_TILE
    assert ctx_seq & (ctx_seq - 1) == 0 and lat_seq & (lat_seq - 1) == 0
    col0 = 5 * A_WIDTH // B_WIDTH
    halo = rb // SUBLANES
    last = m // SUBLANES - 1
    return pl.pallas_call(
        functools.partial(_short_conv_kernel, ctx_tiles=n_ctx_rows // rb, ctx_seq=ctx_seq, lat_seq=lat_seq),
        grid=(m // rb, 3),
        in_specs=[pl.BlockSpec((rb, B_WIDTH), lambda i, j: (i, col0 + j)),
                  pl.BlockSpec((SUBLANES, B_WIDTH), lambda i, j: (jnp.maximum(i * halo - 1, 0), col0 + j)),
                  pl.BlockSpec((SUBLANES, B_WIDTH), lambda i, j: (jnp.minimum((i + 1) * halo, last), col0 + j)),
                  pl.BlockSpec((SHORT_CONV, B_WIDTH), lambda i, j: (0, j)),
                  pl.BlockSpec((1, B_WIDTH), lambda i, j: (0, j))],
        out_specs=pl.BlockSpec((None, rb, B_WIDTH), lambda i, j: (j, i, 0)),
        out_shape=jax.ShapeDtypeStruct((3, m, B_WIDTH), F32),
        compiler_params=_cparams(("parallel", "parallel")),
        name="hyena_short_conv",
    )(proj, proj, proj, conv_w, conv_b.reshape(1, -1))


def _filter_kernel(z_ref, meta_ref, w1_ref, b1_ref, w2_ref, b2_ref, w3_ref, f_ref, dl_ref, o_ref):
    hdn = jnp.sin(f_ref[0:1, :] * (_dot_split(z_ref[...], w1_ref[...]) + b1_ref[...]))
    hdn = jnp.sin(f_ref[1:2, :] * (_dot_split(hdn, w2_ref[...]) + b2_ref[...]))
    filt = _dot_split(hdn, w3_ref[...])
    t = meta_ref[:, 0:1]
    fwd = meta_ref[:, 1:2]
    bwd = meta_ref[:, 2:3]
    win = jnp.exp(-t * dl_ref[...])
    for o in range(HYENA_ORDER):
        hf = filt[:, (2 * o) * B_WIDTH:(2 * o + 1) * B_WIDTH]
        hb = filt[:, (2 * o + 1) * B_WIDTH:(2 * o + 2) * B_WIDTH]
        o_ref[o] = (fwd * hf + bwd * hb) * win


def hyena_filter_taps(L, w1, b1, w2, b2, w3, freq):
    n = 2 * L
    t = jnp.linspace(0.0, 1.0, L, dtype=F32)[:, None]
    bands = (FILTER_EMB - 1) // 2
    ang = (2.0 * math.pi / L) * jnp.arange(L, dtype=F32)[:, None] * jnp.linspace(1e-4, bands - 1, bands, dtype=F32)[None, :]
    z = jnp.concatenate([t, jnp.cos(ang), -jnp.sin(ang)], axis=-1)
    deltas = jnp.abs(jnp.linspace(math.log(DECAY_TARGET) / SLOW_DECAY, math.log(DECAY_TARGET) / FAST_DECAY, B_WIDTH,
                                  dtype=F32)).reshape(1, B_WIDTH)
    lag = np.concatenate([np.arange(L), [0], np.arange(L - 1, 0, -1)])
    side = np.zeros((n, 7), np.float32)
    side[:L, 0] = 1.0
    side[L + 1:, 1] = 1.0
    meta = jnp.concatenate([t[lag], jnp.asarray(side)], axis=1)
    tr = min(n, 512)
    hid = w1.shape[1]
    fixed = lambda i: (0, 0)
    emb = LANES
    z = jnp.pad(z[lag], ((0, 0), (0, emb - FILTER_EMB)))
    w1 = jnp.pad(w1, ((0, emb - FILTER_EMB), (0, 0)))
    return pl.pallas_call(
        _filter_kernel,
        grid=(n // tr,),
        in_specs=[pl.BlockSpec((tr, emb), lambda i: (i, 0)), pl.BlockSpec((tr, 8), lambda i: (i, 0)),
                  pl.BlockSpec((emb, hid), fixed), pl.BlockSpec((1, hid), fixed),
                  pl.BlockSpec((hid, hid), fixed), pl.BlockSpec((1, hid), fixed),
                  pl.BlockSpec((hid, HYENA_ORDER * 2 * B_WIDTH), fixed), pl.BlockSpec((2, hid), fixed),
                  pl.BlockSpec((1, B_WIDTH), fixed)],
        out_specs=pl.BlockSpec((HYENA_ORDER, tr, B_WIDTH), lambda i: (0, i, 0)),
        out_shape=jax.ShapeDtypeStruct((HYENA_ORDER, n, B_WIDTH), F32),
        compiler_params=_cparams(("parallel",)),
        name="hyena_filter_taps",
    )(z, meta, w1, b1.reshape(1, hid), w2, b2.reshape(1, hid), w3, freq, deltas)


class _FftPlan:
    def __init__(self, L, minor):
        n = 2 * L
        n1 = n // minor
        h1, k1 = n1 // 2, n1 // 2 + 1
        self.L, self.minor, self.n1, self.h1, self.k1 = L, minor, n1, h1, k1
        th = 2.0 * np.pi / n1
        ph = th * ((np.arange(k1)[:, None] * np.arange(n1)[None, :]) % n1)
        fwd = np.zeros((2 * k1, n1))
        fwd[0::2], fwd[1::2] = np.cos(ph), -np.sin(ph)
        self.first_full = self._two_f32(fwd)
        self.first_half = self._two_f32(fwd[:, :h1])
        wgt = np.where((np.arange(k1) == 0) | (np.arange(k1) == n1 // 2), 1.0, 2.0) / n
        ph = th * ((np.arange(h1)[:, None] * np.arange(k1)[None, :]) % n1)
        inv = np.zeros((h1, 2 * k1))
        inv[:, 0::2], inv[:, 1::2] = wgt * np.cos(ph), -wgt * np.sin(ph)
        self.last = self._two_f32(inv)
        k = np.arange(k1)[:, None, None] + n1 * np.arange(minor)[None, :, None]
        ph = 2.0 * np.pi * ((k * np.arange(minor)[None, None, :]) % n) / n
        c, s = np.cos(ph), np.sin(ph)
        mf = np.concatenate([np.concatenate([c, s], axis=2), np.concatenate([-s, c], axis=2)], axis=1)
        self.mid_fwd = self._hi_lo(mf)
        self.mid_inv = self._hi_lo(np.swapaxes(mf, 1, 2))

    @staticmethod
    def _two_f32(table):
        hi = table.astype(np.float32)
        lo = (table - hi.astype(np.float64)).astype(np.float32)
        return jnp.asarray(np.concatenate([hi.reshape(-1), lo.reshape(-1)]))

    @staticmethod
    def _hi_lo(mat):
        bits = mat.astype(np.float32).view(np.uint32)
        bits = (bits + 0x7FFF + ((bits >> 16) & 1)) & np.uint32(0xFFFF0000)
        hi = bits.view(np.float32)
        lo = (mat - hi.astype(np.float64)).astype(np.float32)
        return jnp.asarray(hi).astype(BF16), jnp.asarray(lo)


def _dot_hi_lo(m_hi, m_rest, x):
    m_lo = m_rest.astype(BF16)
    x_hi = x.astype(BF16)
    x_lo = (x - x_hi.astype(F32)).astype(BF16)
    return (jnp.dot(m_hi, x_hi, preferred_element_type=F32) + jnp.dot(m_hi, x_lo, preferred_element_type=F32)
            + jnp.dot(m_lo, x_hi, preferred_element_type=F32))


def _coef(tab_ref, i, size):
    return tab_ref[i] + tab_ref[size + i]


def _outer_forward(cf_ref, k, n_k, n_slabs, slab):
    size = 2 * n_k * n_slabs
    ar = ai = None
    for n1 in range(n_slabs):
        xs = slab(n1)
        tr = _coef(cf_ref, (2 * k) * n_slabs + n1, size) * xs
        ti = _coef(cf_ref, (2 * k + 1) * n_slabs + n1, size) * xs
        ar, ai = (tr, ti) if ar is None else (ar + tr, ai + ti)
    return jnp.concatenate([ar, ai], axis=0)


def _hyena_conv_kernel(cf_ref, ci_ref, u_ref, g_ref, mfh_ref, mfl_ref, mih_ref, mil_ref, k_ref, d_ref, o_ref, *,
                       minor, h1, n_k, n_seq, seq_len):
    k = pl.program_id(2)
    ct = u_ref.shape[1]

    def slab(n1):
        parts = [u_ref[g * seq_len + n1 * minor:g * seq_len + (n1 + 1) * minor, :] for g in range(n_seq)]
        return parts[0] if n_seq == 1 else jnp.concatenate(parts, axis=1)

    @pl.when(k == 0)
    def _():
        o_ref[...] = jnp.zeros(o_ref.shape, F32)

    x = _dot_hi_lo(mfh_ref[0], mfl_ref[0], _outer_forward(cf_ref, k, n_k, h1, slab))
    xr, xi = x[:minor], x[minor:]
    kr = k_ref[0, :minor, :]
    ki = k_ref[0, minor:, :]
    if n_seq > 1:
        kr = jnp.concatenate([kr] * n_seq, axis=1)
        ki = jnp.concatenate([ki] * n_seq, axis=1)
    y = jnp.concatenate([xr * kr - xi * ki, xr * ki + xi * kr], axis=0)
    gm = _dot_hi_lo(mih_ref[0], mil_ref[0], y)
    gr, gi = gm[:minor], gm[minor:]
    for n1 in range(h1):
        base = n1 * 2 * n_k + 2 * k
        part = _coef(ci_ref, base, 2 * n_k * h1) * gr + _coef(ci_ref, base + 1, 2 * n_k * h1) * gi
        for g in range(n_seq):
            rows = slice(g * seq_len + n1 * minor, g * seq_len + (n1 + 1) * minor)
            o_ref[rows, :] += part[:, g * ct:(g + 1) * ct]

    @pl.when(k == n_k - 1)
    def _():
        o_ref[...] = g_ref[...] * (o_ref[...] + d_ref[...] * u_ref[...])


def _hyena_spectrum_kernel(cf_ref, t_ref, mfh_ref, mfl_ref, o_ref, *, minor, n1, n_k):
    k = pl.program_id(1)
    a = _outer_forward(cf_ref, k, n_k, n1, lambda j: t_ref[j * minor:(j + 1) * minor, :])
    o_ref[0] = _dot_hi_lo(mfh_ref[0], mfl_ref[0], a)


def _hyena_spectrum(plan, taps, ct):
    n, w = taps.shape
    m2 = 2 * plan.minor
    mat = pl.BlockSpec((1, m2, m2), lambda c, k: (k, 0, 0))
    return pl.pallas_call(
        functools.partial(_hyena_spectrum_kernel, minor=plan.minor, n1=plan.n1, n_k=plan.k1),
        grid=(w // ct, plan.k1),
        in_specs=[pl.BlockSpec(memory_space=pltpu.SMEM), pl.BlockSpec((n, ct), lambda c, k: (0, c)), mat, mat],
        out_specs=pl.BlockSpec((1, m2, ct), lambda c, k: (k, 0, c)),
        out_shape=jax.ShapeDtypeStruct((plan.k1, m2, w), F32),
        compiler_params=_cparams(("parallel", "parallel")),
        name="hyena_tap_spectrum",
    )(plan.first_full, taps, *plan.mid_fwd)


def _hyena_conv(plan, n_blocks, n_seq, ct, u, u_plane, u_blk0, gate, gate_plane, g_blk0, spec, bias):
    w = u.shape[-1]
    rows = n_seq * plan.L
    m2 = 2 * plan.minor
    tok = lambda plane, blk0: pl.BlockSpec((None, rows, ct), lambda b, c, k: (plane, blk0 + b, c))
    mat = pl.BlockSpec((1, m2, m2), lambda b, c, k: (k, 0, 0))
    smem = pl.BlockSpec(memory_space=pltpu.SMEM)
    return pl.pallas_call(
        functools.partial(_hyena_conv_kernel, minor=plan.minor, h1=plan.h1, n_k=plan.k1, n_seq=n_seq,
                          seq_len=plan.L),
        grid=(n_blocks, w // ct, plan.k1),
        in_specs=[smem, smem, tok(u_plane, u_blk0), tok(gate_plane, g_blk0), mat, mat, mat, mat,
                  pl.BlockSpec((1, m2, ct), lambda b, c, k: (k, 0, c)),
                  pl.BlockSpec((1, ct), lambda b, c, k: (0, c))],
        out_specs=pl.BlockSpec((None, rows, ct), lambda b, c, k: (0, b, c)),
        out_shape=jax.ShapeDtypeStruct((1, n_blocks * rows, w), F32),
        compiler_params=_cparams(("parallel", "parallel", "arbitrary")),
        name="hyena_long_conv",
    )(plan.first_half, plan.last, u, gate, *plan.mid_fwd, *plan.mid_inv, spec,
      bias.reshape(1, w))


def hyena_mixer(proj, n_ctx_rows, ctx_seq, lat_seq, conv_w, conv_b, fw1, fb1, fw2, fb2, fw3, freq, fbias):
    m = proj.shape[0]
    sc = hyena_short_conv(proj, conv_w, conv_b, n_ctx_rows, ctx_seq, lat_seq)
    outs = []
    for L, row0, n_rows, minor, n_seq, ct in ((ctx_seq, 0, n_ctx_rows, 128, 4, B_WIDTH),
                                              (lat_seq, n_ctx_rows, m - n_ctx_rows, 256, 1, B_WIDTH // 2)):
        plan = _FftPlan(L, minor)
        n_blocks = n_rows // (n_seq * L)
        blk0 = row0 // (n_seq * L)
        taps = hyena_filter_taps(L, fw1, fb1, fw2, fb2, fw3, freq)
        z, z_plane, z_blk0 = sc, 0, blk0
        for o in range(HYENA_ORDER):
            spec = _hyena_spectrum(plan, taps[o], ct)
            z = _hyena_conv(plan, n_blocks, n_seq, ct, z, z_plane, z_blk0, sc, 1 + o, blk0, spec, fbias[o])
            z_plane, z_blk0 = 0, 0
        outs.append(z)
    return outs


def _rope_tables(seq_len):
    n_rows = seq_len // GRID_W
    rows = jnp.repeat(jnp.arange(n_rows), GRID_W)
    cols = jnp.tile(jnp.arange(GRID_W), n_rows)
    quarter = C_HEAD_DIM // 4
    inv = ROPE_BASE ** (-jnp.arange(quarter, dtype=F32) / quarter)
    cos_parts, sin_parts = [], []
    for pos in (rows, cols):
        a = pos.astype(F32)[:, None] * inv
        cos_parts += [jnp.cos(a), jnp.cos(a)]
        sin_parts += [-jnp.sin(a), jnp.sin(a)]
    return jnp.concatenate(cos_parts, axis=-1), jnp.concatenate(sin_parts, axis=-1)


def kernel(x_prompt, x_sample, state_hgrn, state_ret, c, c_ctx, ada_w, ada_b, ln_g, ln_b, even_w_in, even_w_out, hgrn_lb, hgrn_norm_w, hyena_conv_w, hyena_conv_b, hyena_w1, hyena_b1, hyena_w2, hyena_b2, hyena_w3, hyena_freq, hyena_bias, odd_w_in, odd_w_out, ret_decay, moe_router, moe_w1, moe_w3, moe_w2):
    nb, seq, d = x_prompt.shape
    nl, lseq, _ = x_sample.shape
    n_ctx = nb * seq
    n_lat = nl * lseq
    h = jnp.concatenate([x_prompt.reshape(n_ctx, d), x_sample.reshape(n_lat, d)], axis=0)

    cond = jnp.concatenate([c_ctx[None, :], c], axis=0)
    mods = modulation_table(cond, ada_w, ada_b).reshape(DEPTH, 1 + nl, 6, d)

    lb_soft = jax.nn.softmax(hgrn_lb.astype(F32), axis=0)
    lb_all = jnp.cumsum(lb_soft, axis=0) - lb_soft[0]
    rope_tabs = _rope_tables(lseq)
    zero_hgrn = jnp.zeros((1, 2, A_HEADS, A_HEAD_DIM, A_HEAD_DIM), F32)
    zero_ret = jnp.zeros((1, 2, C_HEADS, C_HEAD_DIM, C_HEAD_DIM), F32)

    hgrn_states, ret_states = [], []
    for l in range(DEPTH):
        j = l // 2
        mod = mods[l]
        if l % 2 == 0:
            proj = in_projection(h, mod, even_w_in, j, n_ctx, lseq)
            o_p, s_p = hgrn_scan(proj, 0, nb, seq, lb_all[j], zero_hgrn)
            o_l, _ = hgrn_scan(proj, n_ctx, nl, lseq, lb_all[j], state_hgrn[:, j])
            hgrn_states.append(s_p)
            yb_p, yb_l = hyena_mixer(proj, n_ctx, seq, lseq, hyena_conv_w[j], hyena_conv_b[j], hyena_w1[j],
                                     hyena_b1[j], hyena_w2[j], hyena_b2[j], hyena_w3[j], hyena_freq[j], hyena_bias[j])
            h, u2, aff_t = post_mixer(True, o_p, o_l, proj, (hgrn_norm_w[j], yb_p, yb_l), h, mod, even_w_out, j,
                                      ln_g[l, 0], ln_b[l, 0], moe_router[l], n_ctx, lseq)
        else:
            proj = in_projection(h, mod, odd_w_in, j, n_ctx, lseq)
            log_gamma = jax.nn.log_sigmoid(ret_decay[j].astype(F32))
            o_p, s_p = retention_scan(proj, 0, nb, seq, log_gamma, zero_ret, None)
            o_l, _ = retention_scan(proj, n_ctx, nl, lseq, log_gamma, state_ret[:, j], rope_tabs)
            ret_states.append(s_p)
            h, u2, aff_t = post_mixer(False, o_p, o_l, proj, None, h, mod, odd_w_out, j,
                                      ln_g[l, 0], ln_b[l, 0], moe_router[l], n_ctx, lseq)
        h = moe_layer(aff_t, u2, h, mod, moe_w1, moe_w3, moe_w2, l, ln_g[l, 1], ln_b[l, 1], nb, seq, nl, lseq)

    y_prompt = h[:n_ctx].reshape(nb, seq, d)
    y_sample = h[n_ctx:].reshape(nl, lseq, d)
    new_state_hgrn = jnp.stack(hgrn_states, axis=1)
    new_state_ret = jnp.stack(ret_states, axis=1)
    return (y_prompt, y_sample, new_state_hgrn, new_state_ret)
```

```python
import functools
import math

import jax
import jax.numpy as jnp
import numpy as np
from jax import lax
from jax.experimental import pallas as pl
from jax.experimental.pallas import tpu as pltpu

F32 = jnp.float32
BF16 = jnp.bfloat16

D_MODEL = 1024
DEPTH = 4
GRID_W = 64
A_WIDTH = D_MODEL // 2
A_HEADS = 4
A_HEAD_DIM = A_WIDTH // A_HEADS
F_MIN = 1e-30
B_WIDTH = D_MODEL - A_WIDTH
HYENA_ORDER = 2
FILTER_EMB = 33
SHORT_CONV = 3
DECAY_TARGET = 1e-2
FAST_DECAY = 0.3
SLOW_DECAY = 1.5
C_HEADS = 4
C_HEAD_DIM = D_MODEL // C_HEADS
ROPE_BASE = 10000.0
N_EXPERTS = 16
CAP_FACTOR = 2
LN_EPS = 1e-5
DN_ALPHA = (2 * DEPTH) ** 0.25

LANES = 128
SUBLANES = 8
VMEM_LIMIT = 56 * 1024 * 1024
ROW_TILE = 512
COL_TILE = 2048
MOD_COL_TILE = 1024
SCAN_TILE = 256
HGRN_SUB = 16


def _cparams(sem):
    return pltpu.CompilerParams(dimension_semantics=sem, vmem_limit_bytes=VMEM_LIMIT)


def _cond_index(tile, tile_rows, n_ctx_rows, lat_seq):
    row = tile * tile_rows
    return jnp.where(row < n_ctx_rows, 0, 1 + (row - n_ctx_rows) // lat_seq)


def _mod_kernel(c_ref, w_ref, b_ref, o_ref):
    c = c_ref[...]
    s = c * jax.nn.sigmoid(c)
    o_ref[0] = jnp.dot(s.astype(BF16), w_ref[0].astype(BF16), preferred_element_type=F32) + b_ref[0]


def modulation_table(cond, ada_w, ada_b):
    r, d = cond.shape
    depth, _, n = ada_w.shape
    tn = MOD_COL_TILE
    return pl.pallas_call(
        _mod_kernel,
        grid=(depth, n // tn),
        in_specs=[pl.BlockSpec((r, d), lambda l, j: (0, 0)),
                  pl.BlockSpec((1, d, tn), lambda l, j: (l, 0, j)),
                  pl.BlockSpec((1, 1, tn), lambda l, j: (l, 0, j))],
        out_specs=pl.BlockSpec((1, r, tn), lambda l, j: (l, 0, j)),
        out_shape=jax.ShapeDtypeStruct((depth, r, n), F32),
        compiler_params=_cparams(("parallel", "parallel")),
        name="modulation",
    )(cond, ada_w, ada_b.reshape(depth, 1, n))


def _inproj_kernel(x_ref, mod_ref, w_ref, o_ref):
    shift = mod_ref[0, 0:1, :]
    scale = mod_ref[0, 1:2, :]
    u = (x_ref[...] * (1.0 + scale) + shift).astype(BF16)
    o_ref[...] = jnp.dot(u, w_ref[...].astype(BF16), preferred_element_type=F32)


def in_projection(x, mod, w_stack, layer, n_ctx_rows, lat_seq):
    m, d = x.shape
    n = w_stack.shape[2]
    tm, tn = 2 * ROW_TILE, COL_TILE
    cidx = functools.partial(_cond_index, tile_rows=tm, n_ctx_rows=n_ctx_rows, lat_seq=lat_seq)
    return pl.pallas_call(
        _inproj_kernel,
        grid=(n // tn, m // tm),
        in_specs=[pl.BlockSpec((tm, d), lambda j, i: (i, 0)),
                  pl.BlockSpec((1, 6, d), lambda j, i: (cidx(i), 0, 0)),
                  pl.BlockSpec((None, d, tn), lambda j, i: (layer, 0, j))],
        out_specs=pl.BlockSpec((tm, tn), lambda j, i: (i, j)),
        out_shape=jax.ShapeDtypeStruct((m, n), F32),
        compiler_params=_cparams(("parallel", "parallel")),
        name="in_projection",
    )(x, mod, w_stack)


def _layer_norm_rows(z, g, b):
    mu = jnp.mean(z, axis=-1, keepdims=True)
    zc = z - mu
    var = jnp.mean(zc * zc, axis=-1, keepdims=True)
    return zc * lax.rsqrt(var + LN_EPS) * g + b


def _post_mixer(y, h_ref, mod_ref, w_ref, lng_ref, lnb_ref, r_ref, h_out, u_out, aff_out):
    gate1 = mod_ref[0, 2:3, :]
    shift2 = mod_ref[0, 3:4, :]
    scale2 = mod_ref[0, 4:5, :]
    mix = jnp.dot(y.astype(BF16), w_ref[...].astype(BF16), preferred_element_type=F32)
    hn = _layer_norm_rows(DN_ALPHA * h_ref[...] + gate1 * mix, lng_ref[...], lnb_ref[...])
    h_out[...] = hn
    u2 = (hn * (1.0 + scale2) + shift2).astype(BF16)
    u_out[...] = u2
    logits = lax.dot_general(r_ref[...].astype(BF16), u2, (((1,), (1,)), ((), ())), preferred_element_type=F32)
    mx = jnp.max(logits, axis=0, keepdims=True)
    ex = jnp.exp(logits - mx)
    aff_out[...] = ex / jnp.sum(ex, axis=0, keepdims=True)


def _group_pick(ctx_tiles, ctx_refs, lat_refs):
    is_ctx = pl.program_id(0) < ctx_tiles
    return jnp.where(is_ctx, sum(r[...] for r in ctx_refs), sum(r[...] for r in lat_refs))


def _post_even_kernel(ofc_ref, obc_ref, ofl_ref, obl_ref, ybc_ref, ybl_ref, g_ref, nw_ref, h_ref, mod_ref, w_ref,
                      lng_ref, lnb_ref, r_ref, h_out, u_out, aff_out, *, ctx_tiles):
    o = _group_pick(ctx_tiles, (ofc_ref, obc_ref), (ofl_ref, obl_ref))
    g = g_ref[...]
    parts = []
    for hd in range(A_HEADS):
        sl = slice(hd * A_HEAD_DIM, (hd + 1) * A_HEAD_DIM)
        oh = o[:, sl]
        parts.append(oh * lax.rsqrt(jnp.mean(oh * oh, axis=-1, keepdims=True) + LN_EPS))
    ya = jnp.concatenate(parts, axis=-1) * nw_ref[...] * (g * jax.nn.sigmoid(g))
    y = jnp.concatenate([ya, _group_pick(ctx_tiles, (ybc_ref,), (ybl_ref,))], axis=-1)
    _post_mixer(y, h_ref, mod_ref, w_ref, lng_ref, lnb_ref, r_ref, h_out, u_out, aff_out)


def _post_odd_kernel(ofc_ref, obc_ref, ofl_ref, obl_ref, g_ref, h_ref, mod_ref, w_ref, lng_ref, lnb_ref, r_ref,
                     h_out, u_out, aff_out, *, ctx_tiles):
    o = _group_pick(ctx_tiles, (ofc_ref, obc_ref), (ofl_ref, obl_ref))
    g = g_ref[...]
    parts = []
    for hd in range(C_HEADS):
        sl = slice(hd * C_HEAD_DIM, (hd + 1) * C_HEAD_DIM)
        oh = o[:, sl]
        mu = jnp.mean(oh, axis=-1, keepdims=True)
        oc = oh - mu
        parts.append(oc * lax.rsqrt(jnp.mean(oc * oc, axis=-1, keepdims=True) + LN_EPS))
    y = jnp.concatenate(parts, axis=-1) * (g * jax.nn.sigmoid(g))
    _post_mixer(y, h_ref, mod_ref, w_ref, lng_ref, lnb_ref, r_ref, h_out, u_out, aff_out)


def post_mixer(even, scan_ctx, scan_lat, proj, extra, h, mod, w_out_stack, layer, ln_g, ln_b, router, n_ctx_rows,
               lat_seq):
    m, d = h.shape
    e = router.shape[1]
    tm = ROW_TILE
    ct = n_ctx_rows // tm
    cidx = functools.partial(_cond_index, tile_rows=tm, n_ctx_rows=n_ctx_rows, lat_seq=lat_seq)
    width = (scan_ctx[0] if isinstance(scan_ctx, tuple) else scan_ctx).shape[-1]
    row = lambda i: (i, 0)
    fixed = lambda i: (0, 0)
    ctx_row = lambda i: jnp.minimum(i, ct - 1)
    lat_row = lambda i: jnp.maximum(i - ct, 0)
    plane = lambda p, rowfn, w: pl.BlockSpec((None, tm, w), lambda i: (p, rowfn(i), 0))
    common_specs = [pl.BlockSpec((tm, d), row),
                    pl.BlockSpec((1, 6, d), lambda i: (cidx(i), 0, 0)),
                    pl.BlockSpec((None, d, d), lambda i: (layer, 0, 0)),
                    pl.BlockSpec((1, d), fixed),
                    pl.BlockSpec((1, d), fixed),
                    pl.BlockSpec((e, d), fixed)]
    common_args = [h, mod, w_out_stack, ln_g.reshape(1, d), ln_b.reshape(1, d), router.T]
    def directions(scan, rowfn):
        if isinstance(scan, tuple):
            return [pl.BlockSpec((tm, width), lambda i: (rowfn(i), 0))] * 2, list(scan)
        return [plane(0, rowfn, width), plane(1, rowfn, width)], [scan, scan]

    ctx_specs, ctx_args = directions(scan_ctx, ctx_row)
    lat_specs, lat_args = directions(scan_lat, lat_row)
    specs = ctx_specs + lat_specs
    args = ctx_args + lat_args
    if even:
        norm_w, yb_ctx, yb_lat = extra
        gate_col = 4 * A_WIDTH // width
        specs += [plane(0, ctx_row, B_WIDTH), plane(0, lat_row, B_WIDTH),
                  pl.BlockSpec((tm, width), lambda i: (i, gate_col)), pl.BlockSpec((1, width), fixed)]
        args += [yb_ctx, yb_lat, proj, norm_w.reshape(1, width)]
        body = _post_even_kernel
    else:
        gate_col = 3
        specs += [pl.BlockSpec((tm, width), lambda i: (i, gate_col))]
        args += [proj]
        body = _post_odd_kernel
    return pl.pallas_call(
        functools.partial(body, ctx_tiles=ct),
        grid=(m // tm,),
        in_specs=specs + common_specs,
        out_specs=[pl.BlockSpec((tm, d), row), pl.BlockSpec((tm, d), row), pl.BlockSpec((e, tm), lambda i: (0, i))],
        out_shape=[jax.ShapeDtypeStruct((m, d), F32), jax.ShapeDtypeStruct((m, d), BF16),
                   jax.ShapeDtypeStruct((e, m), F32)],
        compiler_params=_cparams(("arbitrary",)),
        name="post_mixer_even" if even else "post_mixer_odd",
    )(*(args + common_args))


def _rope_halves(x, cos, sin_signed):
    outs = []
    for p in range(2):
        sl = slice(p * LANES, (p + 1) * LANES)
        xp = x[:, sl]
        outs.append(xp * cos[:, sl] + pltpu.roll(xp, LANES // 2, axis=1) * sin_signed[:, sl])
    return jnp.concatenate(outs, axis=-1)


def _retention_kernel(lg_ref, q_ref, k_ref, v_ref, cos_ref, sin_ref, s0_ref, o_ref, sfin_ref, s_scr, *, rope):
    d = pl.program_id(1)
    t = pl.program_id(2)
    c = q_ref.shape[0]
    dk = C_HEAD_DIM

    @pl.when(t == 0)
    def _():
        s_scr[...] = s0_ref[0, 0]

    ti = lax.broadcasted_iota(jnp.int32, (c, c), 0)
    si = lax.broadcasted_iota(jnp.int32, (c, c), 1)
    diff = jnp.where(d == 0, ti - si, si - ti).astype(F32)
    jt = lax.broadcasted_iota(jnp.int32, (c, dk), 0)
    eq = jnp.where(d == 0, jt + 1, c - jt).astype(F32)
    ek = jnp.where(d == 0, c - 1 - jt, jt).astype(F32)
    for hd in range(C_HEADS):
        lg = lg_ref[d, hd]
        sl = slice(hd * dk, (hd + 1) * dk)
        q = q_ref[:, sl]
        k = k_ref[:, sl] * (dk ** -0.5)
        if rope:
            q = _rope_halves(q, cos_ref[...], sin_ref[...])
            k = _rope_halves(k, cos_ref[...], sin_ref[...])
        v = v_ref[:, sl].astype(BF16)
        rel = jnp.where(diff >= 0.0, jnp.exp(lg * jnp.maximum(diff, 0.0)), 0.0)
        att = lax.dot_general(q.astype(BF16), k.astype(BF16), (((1,), (1,)), ((), ())),
                              preferred_element_type=F32) * rel
        intra = jnp.dot(att.astype(BF16), v, preferred_element_type=F32)
        s_old = s_scr[hd]
        inter = jnp.dot((q * jnp.exp(lg * eq)).astype(BF16), s_old.astype(BF16), preferred_element_type=F32)
        o_ref[:, sl] = inter + intra
        kd = (k * jnp.exp(lg * ek)).astype(BF16)
        s_new = s_old * jnp.exp(lg * c) + lax.dot_general(kd, v, (((0,), (0,)), ((), ())),
                                                          preferred_element_type=F32)
        s_scr[hd] = s_new

    @pl.when(t == pl.num_programs(2) - 1)
    def _():
        sfin_ref[0, 0] = s_scr[...]


def retention_scan(proj, row0, n_seq, seq_len, log_gamma, s0, rope_tabs):
    d = D_MODEL
    c = SCAN_TILE
    nt = seq_len // c
    t0 = row0 // c
    rope = rope_tabs is not None
    if rope:
        cos, sin = rope_tabs
    else:
        cos = sin = jnp.zeros((c, C_HEAD_DIM), F32)
    tile = lambda dd, t: t + dd * (nt - 1 - 2 * t)
    tok = lambda col: pl.BlockSpec((c, d), lambda b, dd, t: (t0 + b * nt + tile(dd, t), col))
    rope_spec = pl.BlockSpec((c, C_HEAD_DIM), (lambda b, dd, t: (tile(dd, t), 0)) if rope else (lambda b, dd, t: (0, 0)))
    shared_s0 = s0.shape[0] == 1
    st_shape = (1, 1, C_HEADS, C_HEAD_DIM, C_HEAD_DIM)
    return pl.pallas_call(
        functools.partial(_retention_kernel, rope=rope),
        grid=(n_seq, 2, nt),
        in_specs=[pl.BlockSpec(memory_space=pltpu.SMEM), tok(0), tok(1), tok(2), rope_spec, rope_spec,
                  pl.BlockSpec(st_shape, lambda b, dd, t: (0 if shared_s0 else b, dd, 0, 0, 0))],
        out_specs=[pl.BlockSpec((None, c, d), lambda b, dd, t: (dd, b * nt + tile(dd, t), 0)),
                   pl.BlockSpec(st_shape, lambda b, dd, t: (b, dd, 0, 0, 0))],
        out_shape=[jax.ShapeDtypeStruct((2, n_seq * seq_len, d), F32),
                   jax.ShapeDtypeStruct((n_seq, 2, C_HEADS, C_HEAD_DIM, C_HEAD_DIM), F32)],
        scratch_shapes=[pltpu.VMEM((C_HEADS, C_HEAD_DIM, C_HEAD_DIM), F32)],
        compiler_params=_cparams(("parallel", "parallel", "arbitrary")),
        name="retention_scan",
    )(log_gamma, proj, proj, proj, cos, sin, s0)


def _split3(x):
    hi = x.astype(BF16)
    r1 = x - hi.astype(F32)
    mid = r1.astype(BF16)
    lo = (r1 - mid.astype(F32)).astype(BF16)
    return hi, mid, lo


def _hgrn_kernel(qf_ref, vf_ref, zf_ref, qb_ref, vb_ref, zb_ref, lb_ref, s0_ref, of_ref, ob_ref, sfin_ref,
                 st_scr, qa_scr, qe_scr, k_scr, b_scr, b2_scr, r_scr, bias_scr, *, sub):
    t = pl.program_id(1)
    streams = ((qf_ref, vf_ref, zf_ref, of_ref), (qb_ref, vb_ref, zb_ref, ob_ref))
    tl, w = qf_ref.shape
    dh = A_HEAD_DIM
    nsub = tl // sub
    log2e = 1.0 / math.log(2.0)

    @pl.when(t == 0)
    def _():
        for dirn in range(2):
            for hd in range(A_HEADS):
                st_scr[dirn, hd] = s0_ref[0, dirn, hd].T

    ti = lax.broadcasted_iota(jnp.int32, (tl, tl), 0)
    si = lax.broadcasted_iota(jnp.int32, (tl, tl), 1)
    pr = lax.broadcasted_iota(jnp.int32, (sub * sub, dh), 0)
    for dirn, (q_ref, _, z_ref, _) in enumerate(streams):
        sign = 1 if dirn == 0 else -1
        lb = lb_ref[dirn]
        z = z_ref[...]
        sg = jax.nn.sigmoid(z)
        lf = jnp.log(jnp.maximum(lb + (1.0 - lb) * sg, F_MIN))
        k = (1.0 - lb) * jax.nn.sigmoid(-z)
        k_scr[dirn] = k
        seen = ((ti - si) * sign >= 0) & ((ti // sub) == (si // sub))
        tri = jnp.where(seen, 1.0, 0.0).astype(BF16)
        b = sum(jnp.dot(tri, part, preferred_element_type=F32) for part in _split3(lf))
        b_scr[dirn] = b
        b2_scr[dirn] = b * log2e
        r_scr[dirn] = b * log2e - jnp.log2(k)
        q = q_ref[...]
        q = q * jax.nn.sigmoid(q)
        qa_scr[dirn] = q
        qe_scr[dirn] = q * jnp.exp(b)
        bias_scr[dirn] = jnp.where(((pr % sub) - (pr // sub)) * sign >= 0, 0.0, -1e30)

    ones = jnp.ones((dh, dh), BF16)

    def step(i, carry):
        for dirn, (_, v_ref, _, o_ref) in enumerate(streams):
            blk = i if dirn == 0 else nsub - 1 - i
            rows = pl.ds(pl.multiple_of(blk * sub, sub), sub)
            last = sub - 1 if dirn == 0 else 0
            for hd in range(A_HEADS):
                sl = slice(hd * dh, (hd + 1) * dh)
                qs = qa_scr[dirn, rows, sl]
                ks = k_scr[dirn, rows, sl]
                vs = v_ref[rows, sl]
                bs = b_scr[dirn, rows, sl]
                b2 = b2_scr[dirn, rows, sl]
                rs = r_scr[dirn, rows, sl]
                st = st_scr[dirn, hd]
                inter = lax.dot_general(qe_scr[dirn, rows, sl].astype(BF16), st.astype(BF16),
                                        (((1,), (1,)), ((), ())), preferred_element_type=F32)
                es = [(qs * jnp.exp2(b2 - rs[s:s + 1, :] + bias_scr[dirn, s * sub:(s + 1) * sub, :])).astype(BF16)
                      for s in range(sub)]
                att = jnp.dot(jnp.concatenate(es, axis=0), ones, preferred_element_type=F32)
                intra = att[0:sub] * vs[0:1, :]
                for s in range(1, sub):
                    intra = intra + att[s * sub:(s + 1) * sub] * vs[s:s + 1, :]
                o_ref[rows, sl] = inter + intra
                b_end = bs[last:last + 1, :]
                kd = (ks * jnp.exp(b_end - bs)).astype(BF16)
                st_scr[dirn, hd] = st * jnp.exp(b_end) + lax.dot_general(
                    vs.astype(BF16), kd, (((0,), (0,)), ((), ())), preferred_element_type=F32)
        return carry

    lax.fori_loop(0, nsub, step, 0)

    @pl.when(t == pl.num_programs(1) - 1)
    def _():
        for dirn in range(2):
            for hd in range(A_HEADS):
                sfin_ref[0, dirn, hd] = st_scr[dirn, hd].T


def hgrn_scan(proj, row0, n_seq, seq_len, lb, s0):
    w = A_WIDTH
    tl = SCAN_TILE
    nt = seq_len // tl
    t0 = row0 // tl
    fwd = lambda col: pl.BlockSpec((tl, w), lambda b, t: (t0 + b * nt + t, col))
    bwd = lambda col: pl.BlockSpec((tl, w), lambda b, t: (t0 + b * nt + nt - 1 - t, col))
    shared_s0 = s0.shape[0] == 1
    st_shape = (1, 2, A_HEADS, A_HEAD_DIM, A_HEAD_DIM)
    o_f, o_b, s_fin = pl.pallas_call(
        functools.partial(_hgrn_kernel, sub=HGRN_SUB),
        grid=(n_seq, nt),
        in_specs=[fwd(0), fwd(1), fwd(2), bwd(0), bwd(1), bwd(3),
                  pl.BlockSpec((2, 1, w), lambda b, t: (0, 0, 0)),
                  pl.BlockSpec(st_shape, lambda b, t: (0 if shared_s0 else b, 0, 0, 0, 0))],
        out_specs=[pl.BlockSpec((tl, w), lambda b, t: (b * nt + t, 0)),
                   pl.BlockSpec((tl, w), lambda b, t: (b * nt + nt - 1 - t, 0)),
                   pl.BlockSpec(st_shape, lambda b, t: (b, 0, 0, 0, 0))],
        out_shape=[jax.ShapeDtypeStruct((n_seq * seq_len, w), F32), jax.ShapeDtypeStruct((n_seq * seq_len, w), F32),
                   jax.ShapeDtypeStruct((n_seq, 2, A_HEADS, A_HEAD_DIM, A_HEAD_DIM), F32)],
        scratch_shapes=[pltpu.VMEM((2, A_HEADS, A_HEAD_DIM, A_HEAD_DIM), F32)] + [pltpu.VMEM((2, tl, w), F32)] * 6
                       + [pltpu.VMEM((2, HGRN_SUB * HGRN_SUB, A_HEAD_DIM), F32)],
        compiler_params=_cparams(("parallel", "arbitrary")),
        name="hgrn_scan",
    )(proj, proj, proj, proj, proj, proj, lb.reshape(2, 1, w), s0)
    return (o_f, o_b), s_fin


def _exclusive_count(x):
    r, n = x.shape
    ji = lax.broadcasted_iota(jnp.int32, (LANES, LANES), 0)
    ii = lax.broadcasted_iota(jnp.int32, (LANES, LANES), 1)
    upper = jnp.where(ji < ii, 1.0, 0.0).astype(BF16)
    carry = jnp.zeros((r, 1), F32)
    outs = []
    for blk in range(n // LANES):
        xb = x[:, blk * LANES:(blk + 1) * LANES]
        outs.append(jnp.dot(xb.astype(BF16), upper, preferred_element_type=F32) + carry)
        carry = carry + jnp.sum(xb, axis=1, keepdims=True)
    return jnp.concatenate(outs, axis=1)


def _select_kernel(a_ref, pos_ref, *, cap):
    a = a_ref[...]

    def count(mask):
        return jnp.sum(jnp.where(mask, 1.0, 0.0), axis=1, keepdims=True)

    def body(i, t):
        cand = t | lax.shift_left(jnp.int32(1), 30 - i)
        return jnp.where(count(a >= pltpu.bitcast(cand, F32)) >= cap, cand, t)

    t = lax.fori_loop(0, 31, body, jnp.zeros((a.shape[0], 1), jnp.int32))
    v = jnp.min(jnp.where(a >= pltpu.bitcast(t, F32), a, jnp.inf), axis=1, keepdims=True)

    def too_low(v):
        return jnp.max(count(a > v)) >= cap

    def step_up(v):
        nxt = jnp.min(jnp.where(a > v, a, jnp.inf), axis=1, keepdims=True)
        return jnp.where(count(a > v) >= cap, nxt, v)

    v = lax.while_loop(too_low, step_up, v)
    gt = jnp.where(a > v, 1.0, 0.0)
    eq = jnp.where(a == v, 1.0, 0.0)
    need = cap - jnp.sum(gt, axis=1, keepdims=True)
    sel = gt + eq * jnp.where(_exclusive_count(eq) < need, 1.0, 0.0)
    pos = _exclusive_count(sel)
    pos_ref[...] = jnp.where(sel > 0.0, pos, -1.0).astype(jnp.int32)


def route_select(aff_rows, cap):
    r, n = aff_rows.shape
    rb = min(r, LANES)
    return pl.pallas_call(
        functools.partial(_select_kernel, cap=cap),
        grid=(r // rb,),
        in_specs=[pl.BlockSpec((rb, n), lambda i: (i, 0))],
        out_specs=pl.BlockSpec((rb, n), lambda i: (i, 0)),
        out_shape=jax.ShapeDtypeStruct((r, n), jnp.int32),
        compiler_params=_cparams(("parallel",)),
        name="route_select",
    )(aff_rows)


def _gather_ctx_kernel(pos_ref, u_ref, o_ref):
    e, n = pos_ref.shape
    cap = o_ref.shape[1]
    slot = lax.broadcasted_iota(jnp.int32, (cap, n), 0)
    onehot = jnp.concatenate([jnp.where(pos_ref[ei:ei + 1, :] == slot, 1.0, 0.0).astype(BF16) for ei in range(e)],
                             axis=0)
    x = jnp.dot(onehot, u_ref[...], preferred_element_type=F32)
    for ei in range(e):
        o_ref[ei] = x[ei * cap:(ei + 1) * cap].astype(BF16)


def _gather_lat_kernel(pos_ref, u_ref, o_ref):
    n_grp, cap, d = o_ref.shape
    n = u_ref.shape[0]
    chunk = min(n, 2 * ROW_TILE)
    slot = lax.broadcasted_iota(jnp.int32, (cap, chunk), 0)
    acc = jnp.zeros((n_grp * cap, d), F32)
    for c0 in range(0, n, chunk):
        onehot = jnp.concatenate([jnp.where(pos_ref[g, :, c0:c0 + chunk] == slot, 1.0, 0.0).astype(BF16)
                                  for g in range(n_grp)], axis=0)
        acc = acc + jnp.dot(onehot, u_ref[c0:c0 + chunk, :], preferred_element_type=F32)
    for g in range(n_grp):
        o_ref[g] = acc[g * cap:(g + 1) * cap].astype(BF16)


def route_gather(pos_ctx, pos_lat, u2, nb, seq, nl, lseq, n_exp):
    d = u2.shape[1]
    cap_c = CAP_FACTOR * seq // n_exp
    cap_l = CAP_FACTOR * lseq // n_exp
    xs_c = pl.pallas_call(
        _gather_ctx_kernel,
        grid=(nb,),
        in_specs=[pl.BlockSpec((n_exp, seq), lambda b: (b, 0)), pl.BlockSpec((seq, d), lambda b: (b, 0))],
        out_specs=pl.BlockSpec((n_exp, cap_c, d), lambda b: (0, b, 0)),
        out_shape=jax.ShapeDtypeStruct((n_exp, nb * cap_c, d), BF16),
        compiler_params=_cparams(("parallel",)),
        name="route_gather_ctx",
    )(pos_ctx, u2)
    lat_blk0 = nb * seq // lseq
    grp = 2
    xs_l = pl.pallas_call(
        _gather_lat_kernel,
        grid=(nl, n_exp // grp),
        in_specs=[pl.BlockSpec((grp, 1, lseq), lambda b, ei: (b * (n_exp // grp) + ei, 0, 0)),
                  pl.BlockSpec((lseq, d), lambda b, ei: (lat_blk0 + b, 0))],
        out_specs=pl.BlockSpec((grp, cap_l, d), lambda b, ei: (ei, b, 0)),
        out_shape=jax.ShapeDtypeStruct((n_exp, nl * cap_l, d), BF16),
        compiler_params=_cparams(("parallel", "arbitrary")),
        name="route_gather_lat",
    )(pos_lat.reshape(nl * n_exp, 1, lseq), u2)
    return xs_c, xs_l


def _combine_kernel(pos_ref, gate_ref, yc_ref, yl_ref, h_ref, mod_ref, lng_ref, lnb_ref, o_ref, *, ctx_tiles,
                    ctx_seq):
    tn, e = pos_ref.shape
    d = h_ref.shape[1]

    def finish(moe):
        gate2 = mod_ref[0, 5:6, :]
        o_ref[...] = _layer_norm_rows(DN_ALPHA * h_ref[...] + gate2 * moe, lng_ref[...], lnb_ref[...])

    @pl.when(pl.program_id(0) < ctx_tiles)
    def _():
        n_sub = tn // ctx_seq
        cap = yc_ref.shape[1] // n_sub
        lane = lax.broadcasted_iota(jnp.int32, (e, e * cap), 1)
        owner = lax.broadcasted_iota(jnp.int32, (e, e * cap), 0)
        expand = jnp.where(lane // cap == owner, 1.0, 0.0).astype(BF16)
        slot = (lax.broadcasted_iota(jnp.int32, (ctx_seq, e * cap), 1) % cap).astype(F32)
        parts = []
        for g in range(n_sub):
            rows = slice(g * ctx_seq, (g + 1) * ctx_seq)
            posx = jnp.dot(pos_ref[rows, :].astype(F32).astype(BF16), expand, preferred_element_type=F32)
            gatex = sum(jnp.dot(part, expand, preferred_element_type=F32) for part in _split3(gate_ref[rows, :]))
            w = jnp.where(posx == slot, gatex, 0.0)
            w_hi = w.astype(BF16)
            w_lo = (w - w_hi.astype(F32)).astype(BF16)
            y = yc_ref[:, g * cap:(g + 1) * cap, :].reshape(e * cap, d)
            parts.append(jnp.dot(w_hi, y, preferred_element_type=F32) + jnp.dot(w_lo, y, preferred_element_type=F32))
        finish(jnp.concatenate(parts, axis=0))

    @pl.when(pl.program_id(0) >= ctx_tiles)
    def _():
        cap = yl_ref.shape[1]
        slot = lax.broadcasted_iota(jnp.int32, (tn, cap), 1)
        acc = jnp.zeros((tn, d), F32)
        for ei in range(e):
            onehot = jnp.where(pos_ref[:, ei:ei + 1] == slot, 1.0, 0.0).astype(BF16)
            acc = acc + gate_ref[:, ei:ei + 1] * jnp.dot(onehot, yl_ref[ei], preferred_element_type=F32)
        finish(acc)


def route_combine(pos_t, gate_t, ys, h, mod, ln_g, ln_b, nb, seq, nl, lseq):
    m, d = h.shape
    e = pos_t.shape[1]
    tn = ROW_TILE
    ct = nb * seq // tn
    cap_c = CAP_FACTOR * seq // e * (tn // seq)
    cap_l = CAP_FACTOR * lseq // e
    lat_blk0 = ct * cap_c // cap_l
    per_seq = lseq // tn
    cidx = functools.partial(_cond_index, tile_rows=tn, n_ctx_rows=nb * seq, lat_seq=lseq)
    row = lambda i: (i, 0)
    fixed = lambda i: (0, 0)
    return pl.pallas_call(
        functools.partial(_combine_kernel, ctx_tiles=ct, ctx_seq=seq),
        grid=(m // tn,),
        in_specs=[pl.BlockSpec((tn, e), row), pl.BlockSpec((tn, e), row),
                  pl.BlockSpec((e, cap_c, d), lambda i: (0, jnp.minimum(i, ct - 1), 0)),
                  pl.BlockSpec((e, cap_l, d), lambda i: (0, lat_blk0 + jnp.maximum(i - ct, 0) // per_seq, 0)),
                  pl.BlockSpec((tn, d), row),
                  pl.BlockSpec((1, 6, d), lambda i: (cidx(i), 0, 0)),
                  pl.BlockSpec((1, d), fixed), pl.BlockSpec((1, d), fixed)],
        out_specs=pl.BlockSpec((tn, d), row),
        out_shape=jax.ShapeDtypeStruct((m, d), F32),
        compiler_params=_cparams(("arbitrary",)),
        name="route_combine",
    )(pos_t, gate_t, ys, ys, h, mod, ln_g.reshape(1, d), ln_b.reshape(1, d))


def _expert_kernel(xc_ref, xl_ref, w1_ref, w3_ref, w2_ref, o_ref, *, ctx_tiles):
    x = jnp.where(pl.program_id(1) < ctx_tiles, xc_ref[0], xl_ref[0])
    a = jnp.dot(x, w1_ref[0].astype(BF16), preferred_element_type=F32)
    g = jnp.dot(x, w3_ref[0].astype(BF16), preferred_element_type=F32)
    hid = (a * jax.nn.sigmoid(a) * g).astype(BF16)
    o_ref[0] = jnp.dot(hid, w2_ref[0].astype(BF16), preferred_element_type=F32).astype(BF16)


def expert_ffn(xs_c, xs_l, w1, w3, w2, layer):
    e, mc, d = xs_c.shape
    ml = xs_l.shape[1]
    f = w1.shape[3]
    tm = ROW_TILE
    ct, lt = mc // tm, ml // tm
    return pl.pallas_call(
        functools.partial(_expert_kernel, ctx_tiles=ct),
        grid=(e, ct + lt),
        in_specs=[pl.BlockSpec((1, tm, d), lambda ei, i: (ei, jnp.minimum(i, ct - 1), 0)),
                  pl.BlockSpec((1, tm, d), lambda ei, i: (ei, jnp.maximum(i - ct, 0), 0)),
                  pl.BlockSpec((None, 1, d, f), lambda ei, i: (layer, ei, 0, 0)),
                  pl.BlockSpec((None, 1, d, f), lambda ei, i: (layer, ei, 0, 0)),
                  pl.BlockSpec((None, 1, f, d), lambda ei, i: (layer, ei, 0, 0))],
        out_specs=pl.BlockSpec((1, tm, d), lambda ei, i: (ei, i, 0)),
        out_shape=jax.ShapeDtypeStruct((e, mc + ml, d), BF16),
        compiler_params=_cparams(("parallel", "arbitrary")),
        name="expert_ffn",
    )(xs_c, xs_l, w1, w3, w2)


def moe_layer(aff_t, u2, h, mod, w1, w3, w2, layer, ln_g, ln_b, nb, seq, nl, lseq):
    e = aff_t.shape[0]
    n_ctx = nb * seq
    seq_rows = lambda a, n, length: a.reshape(e, n, length).transpose(1, 0, 2).reshape(n * e, length)
    pos_c = route_select(seq_rows(aff_t[:, :n_ctx], nb, seq), CAP_FACTOR * seq // e)
    pos_l = route_select(seq_rows(aff_t[:, n_ctx:], nl, lseq), CAP_FACTOR * lseq // e)
    xs_c, xs_l = route_gather(pos_c, pos_l, u2, nb, seq, nl, lseq, e)
    ys = expert_ffn(xs_c, xs_l, w1, w3, w2, layer)
    tok_rows = lambda p, n, length: p.reshape(n, e, length).transpose(0, 2, 1).reshape(n * length, e)
    pos_t = jnp.concatenate([tok_rows(pos_c, nb, seq), tok_rows(pos_l, nl, lseq)], axis=0)
    return route_combine(pos_t, aff_t.T, ys, h, mod, ln_g, ln_b, nb, seq, nl, lseq)


def _dot_split(a, b, passes=3):
    ah, bh = a.astype(BF16), b.astype(BF16)
    out = jnp.dot(ah, bh, preferred_element_type=F32)
    if passes >= 3:
        al = (a - ah.astype(F32)).astype(BF16)
        bl = (b - bh.astype(F32)).astype(BF16)
        out = out + jnp.dot(ah, bl, preferred_element_type=F32) + jnp.dot(al, bh, preferred_element_type=F32)
    return out


def _short_conv_kernel(x_ref, before_ref, after_ref, w_ref, b_ref, o_ref, *, ctx_tiles, ctx_seq, lat_seq):
    rows = x_ref.shape[0]
    x = x_ref[...]
    seq = jnp.where(pl.program_id(0) < ctx_tiles, ctx_seq, lat_seq)
    row = lax.broadcasted_iota(jnp.int32, x.shape, 0)
    pos = (pl.program_id(0) * rows + row) & (seq - 1)
    prev = jnp.where(row == 0, before_ref[SUBLANES - 1:SUBLANES, :], pltpu.roll(x, 1, axis=0))
    nxt = jnp.where(row == rows - 1, after_ref[0:1, :], pltpu.roll(x, rows - 1, axis=0))
    prev = jnp.where(pos == 0, 0.0, prev)
    nxt = jnp.where(pos == seq - 1, 0.0, nxt)
    o_ref[...] = prev * w_ref[0:1, :] + x * w_ref[1:2, :] + nxt * w_ref[2:3, :] + b_ref[...]


def hyena_short_conv(proj, conv_w, conv_b, n_ctx_rows, ctx_seq, lat_seq):
    m = proj.shape[0]
    rb = 2 * ROW_TILE
    assert ctx_seq & (ctx_seq - 1) == 0 and lat_seq & (lat_seq - 1) == 0
    col0 = 5 * A_WIDTH // B_WIDTH
    halo = rb // SUBLANES
    last = m // SUBLANES - 1
    return pl.pallas_call(
        functools.partial(_short_conv_kernel, ctx_tiles=n_ctx_rows // rb, ctx_seq=ctx_seq, lat_seq=lat_seq),
        grid=(m // rb, 3),
        in_specs=[pl.BlockSpec((rb, B_WIDTH), lambda i, j: (i, col0 + j)),
                  pl.BlockSpec((SUBLANES, B_WIDTH), lambda i, j: (jnp.maximum(i * halo - 1, 0), col0 + j)),
                  pl.BlockSpec((SUBLANES, B_WIDTH), lambda i, j: (jnp.minimum((i + 1) * halo, last), col0 + j)),
                  pl.BlockSpec((SHORT_CONV, B_WIDTH), lambda i, j: (0, j)),
                  pl.BlockSpec((1, B_WIDTH), lambda i, j: (0, j))],
        out_specs=pl.BlockSpec((None, rb, B_WIDTH), lambda i, j: (j, i, 0)),
        out_shape=jax.ShapeDtypeStruct((3, m, B_WIDTH), F32),
        compiler_params=_cparams(("parallel", "parallel")),
        name="hyena_short_conv",
    )(proj, proj, proj, conv_w, conv_b.reshape(1, -1))


def _filter_kernel(z_ref, meta_ref, w1_ref, b1_ref, w2_ref, b2_ref, w3_ref, f_ref, dl_ref, o_ref):
    hdn = jnp.sin(f_ref[0:1, :] * (_dot_split(z_ref[...], w1_ref[...]) + b1_ref[...]))
    hdn = jnp.sin(f_ref[1:2, :] * (_dot_split(hdn, w2_ref[...]) + b2_ref[...]))
    filt = _dot_split(hdn, w3_ref[...])
    t = meta_ref[:, 0:1]
    fwd = meta_ref[:, 1:2]
    bwd = meta_ref[:, 2:3]
    win = jnp.exp(-t * dl_ref[...])
    for o in range(HYENA_ORDER):
        hf = filt[:, (2 * o) * B_WIDTH:(2 * o + 1) * B_WIDTH]
        hb = filt[:, (2 * o + 1) * B_WIDTH:(2 * o + 2) * B_WIDTH]
        o_ref[o] = (fwd * hf + bwd * hb) * win


def hyena_filter_taps(L, w1, b1, w2, b2, w3, freq):
    n = 2 * L
    t = jnp.linspace(0.0, 1.0, L, dtype=F32)[:, None]
    bands = (FILTER_EMB - 1) // 2
    ang = (2.0 * math.pi / L) * jnp.arange(L, dtype=F32)[:, None] * jnp.linspace(1e-4, bands - 1, bands, dtype=F32)[None, :]
    z = jnp.concatenate([t, jnp.cos(ang), -jnp.sin(ang)], axis=-1)
    deltas = jnp.abs(jnp.linspace(math.log(DECAY_TARGET) / SLOW_DECAY, math.log(DECAY_TARGET) / FAST_DECAY, B_WIDTH,
                                  dtype=F32)).reshape(1, B_WIDTH)
    lag = np.concatenate([np.arange(L), [0], np.arange(L - 1, 0, -1)])
    side = np.zeros((n, 7), np.float32)
    side[:L, 0] = 1.0
    side[L + 1:, 1] = 1.0
    meta = jnp.concatenate([t[lag], jnp.asarray(side)], axis=1)
    tr = min(n, 512)
    hid = w1.shape[1]
    fixed = lambda i: (0, 0)
    emb = LANES
    z = jnp.pad(z[lag], ((0, 0), (0, emb - FILTER_EMB)))
    w1 = jnp.pad(w1, ((0, emb - FILTER_EMB), (0, 0)))
    return pl.pallas_call(
        _filter_kernel,
        grid=(n // tr,),
        in_specs=[pl.BlockSpec((tr, emb), lambda i: (i, 0)), pl.BlockSpec((tr, 8), lambda i: (i, 0)),
                  pl.BlockSpec((emb, hid), fixed), pl.BlockSpec((1, hid), fixed),
                  pl.BlockSpec((hid, hid), fixed), pl.BlockSpec((1, hid), fixed),
                  pl.BlockSpec((hid, HYENA_ORDER * 2 * B_WIDTH), fixed), pl.BlockSpec((2, hid), fixed),
                  pl.BlockSpec((1, B_WIDTH), fixed)],
        out_specs=pl.BlockSpec((HYENA_ORDER, tr, B_WIDTH), lambda i: (0, i, 0)),
        out_shape=jax.ShapeDtypeStruct((HYENA_ORDER, n, B_WIDTH), F32),
        compiler_params=_cparams(("parallel",)),
        name="hyena_filter_taps",
    )(z, meta, w1, b1.reshape(1, hid), w2, b2.reshape(1, hid), w3, freq, deltas)


class _FftPlan:
    def __init__(self, L, minor):
        n = 2 * L
        n1 = n // minor
        h1, k1 = n1 // 2, n1 // 2 + 1
        self.L, self.minor, self.n1, self.h1, self.k1 = L, minor, n1, h1, k1
        th = 2.0 * np.pi / n1
        ph = th * ((np.arange(k1)[:, None] * np.arange(n1)[None, :]) % n1)
        fwd = np.zeros((2 * k1, n1))
        fwd[0::2], fwd[1::2] = np.cos(ph), -np.sin(ph)
        self.first_full = self._two_f32(fwd)
        self.first_half = self._two_f32(fwd[:, :h1])
        wgt = np.where((np.arange(k1) == 0) | (np.arange(k1) == n1 // 2), 1.0, 2.0) / n
        ph = th * ((np.arange(h1)[:, None] * np.arange(k1)[None, :]) % n1)
        inv = np.zeros((h1, 2 * k1))
        inv[:, 0::2], inv[:, 1::2] = wgt * np.cos(ph), -wgt * np.sin(ph)
        self.last = self._two_f32(inv)
        k = np.arange(k1)[:, None, None] + n1 * np.arange(minor)[None, :, None]
        ph = 2.0 * np.pi * ((k * np.arange(minor)[None, None, :]) % n) / n
        c, s = np.cos(ph), np.sin(ph)
        mf = np.concatenate([np.concatenate([c, s], axis=2), np.concatenate([-s, c], axis=2)], axis=1)
        self.mid_fwd = self._hi_lo(mf)
        self.mid_inv = self._hi_lo(np.swapaxes(mf, 1, 2))

    @staticmethod
    def _two_f32(table):
        hi = table.astype(np.float32)
        lo = (table - hi.astype(np.float64)).astype(np.float32)
        return jnp.asarray(np.concatenate([hi.reshape(-1), lo.reshape(-1)]))

    @staticmethod
    def _hi_lo(mat):
        bits = mat.astype(np.float32).view(np.uint32)
        bits = (bits + 0x7FFF + ((bits >> 16) & 1)) & np.uint32(0xFFFF0000)
        hi = bits.view(np.float32)
        lo = (mat - hi.astype(np.float64)).astype(np.float32)
        return jnp.asarray(hi).astype(BF16), jnp.asarray(lo)


def _dot_hi_lo(m_hi, m_rest, x):
    m_lo = m_rest.astype(BF16)
    x_hi = x.astype(BF16)
    x_lo = (x - x_hi.astype(F32)).astype(BF16)
    return (jnp.dot(m_hi, x_hi, preferred_element_type=F32) + jnp.dot(m_hi, x_lo, preferred_element_type=F32)
            + jnp.dot(m_lo, x_hi, preferred_element_type=F32))


def _coef(tab_ref, i, size):
    return tab_ref[i] + tab_ref[size + i]


def _outer_forward(cf_ref, k, n_k, n_slabs, slab):
    size = 2 * n_k * n_slabs
    ar = ai = None
    for n1 in range(n_slabs):
        xs = slab(n1)
        tr = _coef(cf_ref, (2 * k) * n_slabs + n1, size) * xs
        ti = _coef(cf_ref, (2 * k + 1) * n_slabs + n1, size) * xs
        ar, ai = (tr, ti) if ar is None else (ar + tr, ai + ti)
    return jnp.concatenate([ar, ai], axis=0)


def _hyena_conv_kernel(cf_ref, ci_ref, u_ref, g_ref, mfh_ref, mfl_ref, mih_ref, mil_ref, k_ref, d_ref, o_ref, *,
                       minor, h1, n_k, n_seq, seq_len):
    k = pl.program_id(2)
    ct = u_ref.shape[1]

    def slab(n1):
        parts = [u_ref[g * seq_len + n1 * minor:g * seq_len + (n1 + 1) * minor, :] for g in range(n_seq)]
        return parts[0] if n_seq == 1 else jnp.concatenate(parts, axis=1)

    @pl.when(k == 0)
    def _():
        o_ref[...] = jnp.zeros(o_ref.shape, F32)

    x = _dot_hi_lo(mfh_ref[0], mfl_ref[0], _outer_forward(cf_ref, k, n_k, h1, slab))
    xr, xi = x[:minor], x[minor:]
    kr = k_ref[0, :minor, :]
    ki = k_ref[0, minor:, :]
    if n_seq > 1:
        kr = jnp.concatenate([kr] * n_seq, axis=1)
        ki = jnp.concatenate([ki] * n_seq, axis=1)
    y = jnp.concatenate([xr * kr - xi * ki, xr * ki + xi * kr], axis=0)
    gm = _dot_hi_lo(mih_ref[0], mil_ref[0], y)
    gr, gi = gm[:minor], gm[minor:]
    for n1 in range(h1):
        base = n1 * 2 * n_k + 2 * k
        part = _coef(ci_ref, base, 2 * n_k * h1) * gr + _coef(ci_ref, base + 1, 2 * n_k * h1) * gi
        for g in range(n_seq):
            rows = slice(g * seq_len + n1 * minor, g * seq_len + (n1 + 1) * minor)
            o_ref[rows, :] += part[:, g * ct:(g + 1) * ct]

    @pl.when(k == n_k - 1)
    def _():
        o_ref[...] = g_ref[...] * (o_ref[...] + d_ref[...] * u_ref[...])


def _hyena_spectrum_kernel(cf_ref, t_ref, mfh_ref, mfl_ref, o_ref, *, minor, n1, n_k):
    k = pl.program_id(1)
    a = _outer_forward(cf_ref, k, n_k, n1, lambda j: t_ref[j * minor:(j + 1) * minor, :])
    o_ref[0] = _dot_hi_lo(mfh_ref[0], mfl_ref[0], a)


def _hyena_spectrum(plan, taps, ct):
    n, w = taps.shape
    m2 = 2 * plan.minor
    mat = pl.BlockSpec((1, m2, m2), lambda c, k: (k, 0, 0))
    return pl.pallas_call(
        functools.partial(_hyena_spectrum_kernel, minor=plan.minor, n1=plan.n1, n_k=plan.k1),
        grid=(w // ct, plan.k1),
        in_specs=[pl.BlockSpec(memory_space=pltpu.SMEM), pl.BlockSpec((n, ct), lambda c, k: (0, c)), mat, mat],
        out_specs=pl.BlockSpec((1, m2, ct), lambda c, k: (k, 0, c)),
        out_shape=jax.ShapeDtypeStruct((plan.k1, m2, w), F32),
        compiler_params=_cparams(("parallel", "parallel")),
        name="hyena_tap_spectrum",
    )(plan.first_full, taps, *plan.mid_fwd)


def _hyena_conv(plan, n_blocks, n_seq, ct, u, u_plane, u_blk0, gate, gate_plane, g_blk0, spec, bias):
    w = u.shape[-1]
    rows = n_seq * plan.L
    m2 = 2 * plan.minor
    tok = lambda plane, blk0: pl.BlockSpec((None, rows, ct), lambda b, c, k: (plane, blk0 + b, c))
    mat = pl.BlockSpec((1, m2, m2), lambda b, c, k: (k, 0, 0))
    smem = pl.BlockSpec(memory_space=pltpu.SMEM)
    return pl.pallas_call(
        functools.partial(_hyena_conv_kernel, minor=plan.minor, h1=plan.h1, n_k=plan.k1, n_seq=n_seq,
                          seq_len=plan.L),
        grid=(n_blocks, w // ct, plan.k1),
        in_specs=[smem, smem, tok(u_plane, u_blk0), tok(gate_plane, g_blk0), mat, mat, mat, mat,
                  pl.BlockSpec((1, m2, ct), lambda b, c, k: (k, 0, c)),
                  pl.BlockSpec((1, ct), lambda b, c, k: (0, c))],
        out_specs=pl.BlockSpec((None, rows, ct), lambda b, c, k: (0, b, c)),
        out_shape=jax.ShapeDtypeStruct((1, n_blocks * rows, w), F32),
        compiler_params=_cparams(("parallel", "parallel", "arbitrary")),
        name="hyena_long_conv",
    )(plan.first_half, plan.last, u, gate, *plan.mid_fwd, *plan.mid_inv, spec,
      bias.reshape(1, w))


def hyena_mixer(proj, n_ctx_rows, ctx_seq, lat_seq, conv_w, conv_b, fw1, fb1, fw2, fb2, fw3, freq, fbias):
    m = proj.shape[0]
    sc = hyena_short_conv(proj, conv_w, conv_b, n_ctx_rows, ctx_seq, lat_seq)
    outs = []
    for L, row0, n_rows, minor, n_seq, ct in ((ctx_seq, 0, n_ctx_rows, 128, 4, B_WIDTH),
                                              (lat_seq, n_ctx_rows, m - n_ctx_rows, 256, 1, B_WIDTH // 2)):
        plan = _FftPlan(L, minor)
        n_blocks = n_rows // (n_seq * L)
        blk0 = row0 // (n_seq * L)
        taps = hyena_filter_taps(L, fw1, fb1, fw2, fb2, fw3, freq)
        z, z_plane, z_blk0 = sc, 0, blk0
        for o in range(HYENA_ORDER):
            spec = _hyena_spectrum(plan, taps[o], ct)
            z = _hyena_conv(plan, n_blocks, n_seq, ct, z, z_plane, z_blk0, sc, 1 + o, blk0, spec, fbias[o])
            z_plane, z_blk0 = 0, 0
        outs.append(z)
    return outs


def _rope_tables(seq_len):
    n_rows = seq_len // GRID_W
    rows = jnp.repeat(jnp.arange(n_rows), GRID_W)
    cols = jnp.tile(jnp.arange(GRID_W), n_rows)
    quarter = C_HEAD_DIM // 4
    inv = ROPE_BASE ** (-jnp.arange(quarter, dtype=F32) / quarter)
    cos_parts, sin_parts = [], []
    for pos in (rows, cols):
        a = pos.astype(F32)[:, None] * inv
        cos_parts += [jnp.cos(a), jnp.cos(a)]
        sin_parts += [-jnp.sin(a), jnp.sin(a)]
    return jnp.concatenate(cos_parts, axis=-1), jnp.concatenate(sin_parts, axis=-1)


def kernel(x_prompt, x_sample, state_hgrn, state_ret, c, c_ctx, ada_w, ada_b, ln_g, ln_b, even_w_in, even_w_out, hgrn_lb, hgrn_norm_w, hyena_conv_w, hyena_conv_b, hyena_w1, hyena_b1, hyena_w2, hyena_b2, hyena_w3, hyena_freq, hyena_bias, odd_w_in, odd_w_out, ret_decay, moe_router, moe_w1, moe_w3, moe_w2):
    nb, seq, d = x_prompt.shape
    nl, lseq, _ = x_sample.shape
    n_ctx = nb * seq
    n_lat = nl * lseq
    h = jnp.concatenate([x_prompt.reshape(n_ctx, d), x_sample.reshape(n_lat, d)], axis=0)

    cond = jnp.concatenate([c_ctx[None, :], c], axis=0)
    mods = modulation_table(cond, ada_w, ada_b).reshape(DEPTH, 1 + nl, 6, d)

    lb_soft = jax.nn.softmax(hgrn_lb.astype(F32), axis=0)
    lb_all = jnp.cumsum(lb_soft, axis=0) - lb_soft[0]
    rope_tabs = _rope_tables(lseq)
    zero_hgrn = jnp.zeros((1, 2, A_HEADS, A_HEAD_DIM, A_HEAD_DIM), F32)
    zero_ret = jnp.zeros((1, 2, C_HEADS, C_HEAD_DIM, C_HEAD_DIM), F32)

    hgrn_states, ret_states = [], []
    for l in range(DEPTH):
        j = l // 2
        mod = mods[l]
        if l % 2 == 0:
            proj = in_projection(h, mod, even_w_in, j, n_ctx, lseq)
            o_p, s_p = hgrn_scan(proj, 0, nb, seq, lb_all[j], zero_hgrn)
            o_l, _ = hgrn_scan(proj, n_ctx, nl, lseq, lb_all[j], state_hgrn[:, j])
            hgrn_states.append(s_p)
            yb_p, yb_l = hyena_mixer(proj, n_ctx, seq, lseq, hyena_conv_w[j], hyena_conv_b[j], hyena_w1[j],
                                     hyena_b1[j], hyena_w2[j], hyena_b2[j], hyena_w3[j], hyena_freq[j], hyena_bias[j])
            h, u2, aff_t = post_mixer(True, o_p, o_l, proj, (hgrn_norm_w[j], yb_p, yb_l), h, mod, even_w_out, j,
                                      ln_g[l, 0], ln_b[l, 0], moe_router[l], n_ctx, lseq)
        else:
            proj = in_projection(h, mod, odd_w_in, j, n_ctx, lseq)
            log_gamma = jax.nn.log_sigmoid(ret_decay[j].astype(F32))
            o_p, s_p = retention_scan(proj, 0, nb, seq, log_gamma, zero_ret, None)
            o_l, _ = retention_scan(proj, n_ctx, nl, lseq, log_gamma, state_ret[:, j], rope_tabs)
            ret_states.append(s_p)
            h, u2, aff_t = post_mixer(False, o_p, o_l, proj, None, h, mod, odd_w_out, j,
                                      ln_g[l, 0], ln_b[l, 0], moe_router[l], n_ctx, lseq)
        h = moe_layer(aff_t, u2, h, mod, moe_w1, moe_w3, moe_w2, l, ln_g[l, 1], ln_b[l, 1], nb, seq, nl, lseq)

    y_prompt = h[:n_ctx].reshape(nb, seq, d)
    y_sample = h[n_ctx:].reshape(nl, lseq, d)
    new_state_hgrn = jnp.stack(hgrn_states, axis=1)
    new_state_ret = jnp.stack(ret_states, axis=1)
    return (y_prompt, y_sample, new_state_hgrn, new_state_ret)
```
